```python
import math
import jax, jax.numpy as jnp
from jax import lax
import numpy as np

D_MODEL = 2048
BATCH = 1
SEQ = 16384
DEPTH = 2

N_EVEN = (DEPTH + 1) // 2
N_ODD = DEPTH // 2

DIFF_HEADS = 8
DIFF_QK_DIM = 64
DIFF_V_DIM = 2 * DIFF_QK_DIM
DIFF_WIDTH = DIFF_HEADS * DIFF_V_DIM
ROT_DIM = DIFF_QK_DIM // 4
ROPE_THETA = 500000.0
Q_BLOCK = 128

HG_HEADS = 8
HG_HEAD_DIM = 128
HG_WIDTH = HG_HEADS * HG_HEAD_DIM
HG_CHUNK = 64

EVEN_SPLIT_SIZES = [DIFF_WIDTH] * 3 + [HG_WIDTH] * 5
EVEN_IN_WIDTH = sum(EVEN_SPLIT_SIZES)
EVEN_SPLITS = [int(s) for s in np.cumsum(EVEN_SPLIT_SIZES)[:-1]]
EVEN_OUT_WIDTH = DIFF_WIDTH + HG_WIDTH

S5_GROUP = 16
S5_GROUPS = D_MODEL // S5_GROUP
S5_STATE = 64
S5_GROUPS_PER_BLOCK = 8

D_FF = 4 * D_MODEL

ALPHA = (2 * DEPTH) ** 0.25
BETA = (8 * DEPTH) ** -0.25
LN_EPS = 1e-5
RMS_EPS = 1e-6

kernel_name = 'hybrid_diffattn_hgrn2_s5_deepnorm_adaln'


def layer_norm(x, g, b):
    xf = x.astype(jnp.float32)
    mu = jnp.mean(xf, axis=-1, keepdims=True)
    var = jnp.mean(jnp.square(xf - mu), axis=-1, keepdims=True)
    return ((xf - mu) * lax.rsqrt(var + LN_EPS)).astype(x.dtype) * g + b


def rms_norm(x, g):
    xf = x.astype(jnp.float32)
    return (xf * lax.rsqrt(jnp.mean(xf * xf, axis=-1, keepdims=True) + RMS_EPS)).astype(x.dtype) * g


def adaln(c_act, w, b):
    m = c_act @ w + b
    shift, scale, gate = jnp.split(m, 3, axis=-1)
    return shift[:, None, :], scale[:, None, :], gate[:, None, :]


def rotary_tables(seq_len):
    pos = jnp.arange(seq_len, dtype=jnp.float32)
    inv_freq = 1.0 / (ROPE_THETA ** (jnp.arange(0, ROT_DIM, 2, dtype=jnp.float32) / ROT_DIM))
    ang = pos[:, None] * inv_freq[None, :]
    return jnp.cos(ang), jnp.sin(ang)


def partial_rotary(t, cos, sin):
    half = ROT_DIM // 2
    cos = cos[None, :, None, None, :]
    sin = sin[None, :, None, None, :]
    tf = t.astype(jnp.float32)
    x1, x2 = tf[..., :half], tf[..., half:ROT_DIM]
    out = jnp.concatenate([x1 * cos - x2 * sin, x2 * cos + x1 * sin, tf[..., ROT_DIM:]], axis=-1)
    return out.astype(t.dtype)


def diff_attention(q, k, v, lam):
    B, L, H, _, dqk = q.shape
    nb = L // Q_BLOCK
    scale = dqk ** -0.5
    qb = q.reshape(B, nb, Q_BLOCK, H, 2, dqk).transpose(1, 0, 2, 3, 4, 5)

    def one_block(q_blk):
        s = jnp.einsum('bqhcd,bkhcd->bhcqk', q_blk, k).astype(jnp.float32) * scale
        p = jax.nn.softmax(s, axis=-1)
        a = p[:, :, 0] - lam * p[:, :, 1]
        return jnp.einsum('bhqk,bkhe->bqhe', a.astype(v.dtype), v)

    o = lax.map(one_block, qb)
    return o.transpose(1, 0, 2, 3, 4).reshape(B, L, H, v.shape[-1])


def gla_chunk_scan(q, k, v, logf):
    B, L, H, dk = q.shape
    dv = v.shape[-1]
    n = L // HG_CHUNK

    def to_chunks(t):
        return t.reshape(B, n, HG_CHUNK, H, t.shape[-1]).transpose(1, 0, 3, 2, 4)

    lower = jnp.tril(jnp.ones((HG_CHUNK, HG_CHUNK), dtype=bool))[:, :, None]

    def step(S, inp):
        qc, kc, vc, gc = inp
        b = jnp.cumsum(gc, axis=2)
        o_inter = jnp.einsum('bhtd,bhde->bhte', qc * jnp.exp(b), S)
        rel = jnp.exp(jnp.where(lower, b[:, :, :, None, :] - b[:, :, None, :, :], -jnp.inf))
        scores = jnp.einsum('bhtd,bhtsd,bhsd->bhts', qc, rel, kc)
        o_intra = jnp.einsum('bhts,bhse->bhte', scores, vc)
        b_end = b[:, :, -1:, :]
        S_new = jnp.exp(b_end[:, :, 0, :, None]) * S + jnp.einsum('bhsd,bhse->bhde', kc * jnp.exp(b_end - b), vc)
        return S_new, o_inter + o_intra

    S0 = jnp.zeros((B, H, dk, dv), jnp.float32)
    _, o = lax.scan(step, S0, (to_chunks(q), to_chunks(k), to_chunks(v), to_chunks(logf)))
    return o.transpose(1, 0, 3, 2, 4).reshape(B, L, H, dv)


def hgrn2(q_raw, f_fwd_raw, f_bwd_raw, i_raw, g_raw, lb, norm_g):
    B, L, _ = q_raw.shape
    heads = lambda t: t.reshape(B, L, HG_HEADS, HG_HEAD_DIM)
    q = heads(jax.nn.silu(q_raw.astype(jnp.float32)))
    v = heads(i_raw.astype(jnp.float32))

    def direction(f_raw, lb_dir, reverse):
        f = lb_dir + (1.0 - lb_dir) * jax.nn.sigmoid(f_raw.astype(jnp.float32))
        k, logf = heads(1.0 - f), heads(jnp.log(f))
        if reverse:
            flip = lambda t: jnp.flip(t, axis=1)
            return flip(gla_chunk_scan(flip(q), flip(k), flip(v), flip(logf)))
        return gla_chunk_scan(q, k, v, logf)

    o = direction(f_fwd_raw, lb[0], False) + direction(f_bwd_raw, lb[1], True)
    o = rms_norm(o, norm_g) * jax.nn.silu(heads(g_raw))
    return o.astype(g_raw.dtype)


def even_mixer(h, w_in, w_out, diff_lambda, subln_g, lb, hg_norm_g, cos, sin, lam_init):
    B, L, _ = h.shape
    proj = h @ w_in
    qa, ka, va, qb, f_fwd, f_bwd, ib, gb = jnp.split(proj, EVEN_SPLITS, axis=-1)
    qa = partial_rotary(qa.reshape(B, L, DIFF_HEADS, 2, DIFF_QK_DIM), cos, sin)
    ka = partial_rotary(ka.reshape(B, L, DIFF_HEADS, 2, DIFF_QK_DIM), cos, sin)
    va = va.reshape(B, L, DIFF_HEADS, DIFF_V_DIM)
    lf = diff_lambda.astype(jnp.float32)
    lam = jnp.exp(jnp.sum(lf[0] * lf[1])) - jnp.exp(jnp.sum(lf[2] * lf[3])) + lam_init
    oa = rms_norm(diff_attention(qa, ka, va, lam), subln_g) * (1.0 - lam_init)
    ob = hgrn2(qb, f_fwd, f_bwd, ib, gb, lb, hg_norm_g)
    o = jnp.concatenate([oa.reshape(B, L, DIFF_WIDTH).astype(h.dtype), ob.reshape(B, L, HG_WIDTH)], axis=-1)
    return o @ w_out


def complex_affine_combine(e1, e2):
    a1r, a1i, x1r, x1i = e1
    a2r, a2i, x2r, x2i = e2
    return (a2r * a1r - a2i * a1i,
            a2r * a1i + a2i * a1r,
            a2r * x1r - a2i * x1i + x2r,
            a2r * x1i + a2i * x1r + x2i)


def s5_direction(u, lam_re, lam_im, log_dt, b_re, b_im, c_re, c_im, reverse):
    dt = jnp.exp(log_dt)[:, None]
    mag = jnp.exp(lam_re * dt)
    ab_re, ab_im = mag * jnp.cos(lam_im * dt), mag * jnp.sin(lam_im * dt)
    den = lam_re * lam_re + lam_im * lam_im
    coef_re = ((ab_re - 1.0) * lam_re + ab_im * lam_im) / den
    coef_im = (ab_im * lam_re - (ab_re - 1.0) * lam_im) / den
    bb_re = coef_re[..., None] * b_re - coef_im[..., None] * b_im
    bb_im = coef_re[..., None] * b_im + coef_im[..., None] * b_re
    bu_re = jnp.einsum('blgh,gph->blgp', u, bb_re)
    bu_im = jnp.einsum('blgh,gph->blgp', u, bb_im)
    a_re = jnp.broadcast_to(ab_re, bu_re.shape)
    a_im = jnp.broadcast_to(ab_im, bu_re.shape)
    _, _, s_re, s_im = lax.associative_scan(complex_affine_combine, (a_re, a_im, bu_re, bu_im),
                                            reverse=reverse, axis=1)
    return jnp.einsum('blgp,ghp->blgh', s_re, c_re) - jnp.einsum('blgp,ghp->blgh', s_im, c_im)


def s5_mixer(h, lam_re, lam_im, log_dt, b_re, b_im, c_re, c_im, d_skip, w_glu):
    B, L, D = h.shape
    nb = S5_GROUPS // S5_GROUPS_PER_BLOCK

    def group_blocks(p):
        p = p.astype(jnp.float32)
        return jnp.moveaxis(p.reshape(p.shape[0], nb, S5_GROUPS_PER_BLOCK, *p.shape[2:]), 1, 0)

    u = jnp.moveaxis(h.astype(jnp.float32).reshape(B, L, nb, S5_GROUPS_PER_BLOCK, S5_GROUP), 2, 0)
    d = d_skip.astype(jnp.float32).reshape(nb, S5_GROUPS_PER_BLOCK, S5_GROUP)

    def run_block(args):
        u_b, lre, lim, ldt, bre, bim, cre, cim, d_b = args
        y = d_b * u_b
        for dr in range(2):
            y = y + s5_direction(u_b, lre[dr], lim[dr], ldt[dr], bre[dr], bim[dr], cre[dr], cim[dr],
                                 reverse=(dr == 1))
        return y

    ys = lax.map(run_block, (u, group_blocks(lam_re), group_blocks(lam_im), group_blocks(log_dt),
                             group_blocks(b_re), group_blocks(b_im), group_blocks(c_re), group_blocks(c_im), d))
    y = jnp.moveaxis(ys, 0, 2).reshape(B, L, D)
    z = jax.nn.gelu(y).astype(h.dtype) @ w_glu
    a, g = jnp.split(z, 2, axis=-1)
    return a * jax.nn.sigmoid(g)


def squared_relu_mlp(h, w1, w2):
    return jnp.square(jax.nn.relu(h @ w1)) @ w2


def setup_inputs(seed: int = 0) -> dict:
    key = jax.random.key(seed)
    ks = jax.random.split(key, 24)
    f32 = jnp.float32
    nrm = lambda k, shape, s: jax.random.normal(k, shape, f32) * s
    n_state = jnp.arange(S5_STATE, dtype=f32)
    return {
        'x': nrm(ks[0], (BATCH, SEQ, D_MODEL), 1.0),
        'c': nrm(ks[1], (BATCH, D_MODEL), 1.0),
        'ada_w': nrm(ks[2], (DEPTH, 2, D_MODEL, 3 * D_MODEL), 0.5 * D_MODEL ** -0.5),
        'ada_b': nrm(ks[3], (DEPTH, 2, 3 * D_MODEL), 0.01),
        'ln_g': 1.0 + nrm(ks[4], (DEPTH, 2, D_MODEL), 0.02),
        'ln_b': nrm(ks[5], (DEPTH, 2, D_MODEL), 0.02),
        'mix_w_in': nrm(ks[6], (N_EVEN, D_MODEL, EVEN_IN_WIDTH), D_MODEL ** -0.5),
        'mix_w_out': nrm(ks[7], (N_EVEN, EVEN_OUT_WIDTH, D_MODEL), BETA * EVEN_OUT_WIDTH ** -0.5),
        'diff_lambda': nrm(ks[8], (N_EVEN, 4, DIFF_QK_DIM), 0.1),
        'diff_subln_g': 1.0 + nrm(ks[9], (N_EVEN, DIFF_V_DIM), 0.02),
        'hg_lower_bound': nrm(ks[10], (2, DEPTH + 1, HG_WIDTH), 0.1),
        'hg_norm_g': 1.0 + nrm(ks[11], (N_EVEN, HG_HEAD_DIM), 0.02),
        's5_lambda_re': -0.5 + nrm(ks[12], (N_ODD, 2, S5_GROUPS, S5_STATE), 0.01),
        's5_lambda_im': jnp.pi * n_state + nrm(ks[13], (N_ODD, 2, S5_GROUPS, S5_STATE), 0.01),
        's5_log_dt': jax.random.uniform(ks[14], (N_ODD, 2, S5_GROUPS), f32,
                                        minval=math.log(1e-3), maxval=math.log(1e-1)),
        's5_b_re': nrm(ks[15], (N_ODD, 2, S5_GROUPS, S5_STATE, S5_GROUP), (2 * S5_GROUP) ** -0.5),
        's5_b_im': nrm(ks[16], (N_ODD, 2, S5_GROUPS, S5_STATE, S5_GROUP), (2 * S5_GROUP) ** -0.5),
        's5_c_re': nrm(ks[17], (N_ODD, 2, S5_GROUPS, S5_GROUP, S5_STATE), S5_STATE ** -0.5),
        's5_c_im': nrm(ks[18], (N_ODD, 2, S5_GROUPS, S5_GROUP, S5_STATE), S5_STATE ** -0.5),
        's5_d': nrm(ks[19], (N_ODD, D_MODEL), 1.0),
        's5_w_glu': nrm(ks[20], (N_ODD, D_MODEL, 2 * D_MODEL), BETA * D_MODEL ** -0.5),
        'mlp_w1': nrm(ks[21], (DEPTH, D_MODEL, D_FF), D_MODEL ** -0.5),
        'mlp_w2': nrm(ks[22], (DEPTH, D_FF, D_MODEL), BETA * D_FF ** -0.5),
    }


def reference(x, c, ada_w, ada_b, ln_g, ln_b, mix_w_in, mix_w_out, diff_lambda, diff_subln_g,
              hg_lower_bound, hg_norm_g, s5_lambda_re, s5_lambda_im, s5_log_dt, s5_b_re, s5_b_im,
              s5_c_re, s5_c_im, s5_d, s5_w_glu, mlp_w1, mlp_w2):
    L = x.shape[1]
    cos, sin = rotary_tables(L)
    c_act = jax.nn.silu(c)
    lb_table = jnp.cumsum(jax.nn.softmax(hg_lower_bound.astype(jnp.float32), axis=1), axis=1)
    for layer in range(DEPTH):
        shift, scale, gate = adaln(c_act, ada_w[layer, 0], ada_b[layer, 0])
        h = x * (1.0 + scale) + shift
        if layer % 2 == 0:
            e = layer // 2
            lam_init = 0.8 - 0.6 * math.exp(-0.3 * layer)
            y = even_mixer(h, mix_w_in[e], mix_w_out[e], diff_lambda[e], diff_subln_g[e],
                           lb_table[:, layer], hg_norm_g[e], cos, sin, lam_init)
        else:
            o = layer // 2
            y = s5_mixer(h, s5_lambda_re[o], s5_lambda_im[o], s5_log_dt[o], s5_b_re[o], s5_b_im[o],
                         s5_c_re[o], s5_c_im[o], s5_d[o], s5_w_glu[o])
        x = layer_norm(ALPHA * x + (1.0 + gate) * y, ln_g[layer, 0], ln_b[layer, 0])
        shift, scale, gate = adaln(c_act, ada_w[layer, 1], ada_b[layer, 1])
        h = x * (1.0 + scale) + shift
        y = squared_relu_mlp(h, mlp_w1[layer], mlp_w2[layer])
        x = layer_norm(ALPHA * x + (1.0 + gate) * y, ln_g[layer, 1], ln_b[layer, 1])
    return x
```

```python
import functools
import math

import numpy as np
import jax
import jax.numpy as jnp
from jax import lax
from jax.experimental import pallas as pl
from jax.experimental.pallas import tpu as pltpu

F32 = jnp.float32
BF16 = jnp.bfloat16

D_MODEL = 2048
DEPTH = 2
LANES = 128
SUBLANES = 8

DIFF_HEADS = 8
DIFF_QK_DIM = 64
DIFF_WIDTH = 1024
ROT_DIM = DIFF_QK_DIM // 4
ROPE_THETA = 500000.0

HG_HEADS = 8
HG_WIDTH = 1024
HG_TILE = 128
HG_BAND = SUBLANES
HG_LEVELS = (64, 32, 16, 8)

S5_GROUP = 16
S5_GROUPS = D_MODEL // S5_GROUP
S5_STATE = 64
S5_T = 16
S5_GPB = LANES // S5_GROUP
S5_BLOCKS = D_MODEL // LANES
S5_ROW = S5_T * LANES
S5_SW = S5_GPB * S5_STATE * 2

D_FF = 4 * D_MODEL
ALPHA = (2 * DEPTH) ** 0.25
LN_EPS = 1e-5
RMS_EPS = 1e-6
HI = lax.Precision.HIGHEST

VMEM_LIMIT = 56 * 1024 * 1024


def _cp(sem):
    return pltpu.CompilerParams(dimension_semantics=sem, vmem_limit_bytes=VMEM_LIMIT)


def _layer_norm_rows(z, g, b):
    mu = jnp.mean(z, axis=-1, keepdims=True)
    zc = z - mu
    var = jnp.mean(zc * zc, axis=-1, keepdims=True)
    return zc * lax.rsqrt(var + LN_EPS) * g + b


def _adaln_kernel(c_ref, w_ref, b_ref, o_ref):
    c = c_ref[...]
    ca = c * jax.nn.sigmoid(c)
    w = w_ref[0]
    reps = w.shape[1] // LANES
    cb = jnp.concatenate([ca] * reps, axis=1)
    o_ref[0] = jnp.sum(w * cb, axis=0, keepdims=True) + b_ref[0]


def _adaln(c, ada_w, ada_b):
    n = 3 * D_MODEL
    tn = 768
    w = ada_w.reshape(2 * DEPTH, D_MODEL, n)
    b = ada_b.reshape(2 * DEPTH, 1, n)
    c_rep = jnp.broadcast_to(c.reshape(D_MODEL, 1), (D_MODEL, LANES))
    return pl.pallas_call(
        _adaln_kernel,
        grid=(2 * DEPTH, n // tn),
        in_specs=[pl.BlockSpec((D_MODEL, LANES), lambda a, j: (0, 0)),
                  pl.BlockSpec((1, D_MODEL, tn), lambda a, j: (a, 0, j)),
                  pl.BlockSpec((1, 1, tn), lambda a, j: (a, 0, j))],
        out_specs=pl.BlockSpec((1, 1, tn), lambda a, j: (a, 0, j)),
        out_shape=jax.ShapeDtypeStruct((2 * DEPTH, 1, n), F32),
        compiler_params=_cp(("parallel", "parallel")),
        name="adaln",
    )(c_rep, w, b)


def _inproj_kernel(x_ref, sc_ref, sh_ref, w_ref, cos_ref, sa_ref, sb_ref, o_ref, h_scr, *, n_rot):
    j = pl.program_id(1)

    @pl.when(j == 0)
    def _():
        h_scr[...] = (x_ref[...] * (1.0 + sc_ref[...]) + sh_ref[...]).astype(BF16)

    y = jnp.dot(h_scr[...], w_ref[...], preferred_element_type=F32)

    @pl.when(j < n_rot)
    def _():
        cs, sa, sb = cos_ref[...], sa_ref[...], sb_ref[...]
        for g in range(y.shape[1] // LANES):
            yg = y[:, g * LANES:(g + 1) * LANES]
            rot = (yg * cs + pltpu.roll(yg, ROT_DIM // 2, 1) * sa
                   + pltpu.roll(yg, LANES - ROT_DIM // 2, 1) * sb)
            o_ref[:, g * LANES:(g + 1) * LANES] = rot.astype(o_ref.dtype)

    @pl.when(j >= n_rot)
    def _():
        o_ref[...] = y.astype(o_ref.dtype)


def _inproj(x, scale, shift, w, rot_tabs, n_rot, out_dtype, tm, tn):
    L = x.shape[0]
    n = w.shape[1]
    tm = min(tm, L)
    return pl.pallas_call(
        functools.partial(_inproj_kernel, n_rot=n_rot),
        grid=(L // tm, n // tn),
        in_specs=[pl.BlockSpec((tm, D_MODEL), lambda i, j: (i, 0)),
                  pl.BlockSpec((1, D_MODEL), lambda i, j: (0, 0)),
                  pl.BlockSpec((1, D_MODEL), lambda i, j: (0, 0)),
                  pl.BlockSpec((D_MODEL, tn), lambda i, j: (0, j)),
                  pl.BlockSpec((tm, LANES), lambda i, j: (i, 0)),
                  pl.BlockSpec((tm, LANES), lambda i, j: (i, 0)),
                  pl.BlockSpec((tm, LANES), lambda i, j: (i, 0))],
        out_specs=pl.BlockSpec((tm, tn), lambda i, j: (i, j)),
        out_shape=jax.ShapeDtypeStruct((L, n), out_dtype),
        scratch_shapes=[pltpu.VMEM((tm, D_MODEL), BF16)],
        compiler_params=_cp(("parallel", "arbitrary")),
        name="inproj",
    )(x, scale, shift, w, *rot_tabs)


def _rotary_tables(L):
    half = ROT_DIM // 2
    pos = jnp.arange(L, dtype=F32)
    inv_freq = 1.0 / (ROPE_THETA ** (jnp.arange(0, ROT_DIM, 2, dtype=F32) / ROT_DIM))
    ang = pos[:, None] * inv_freq[None, :]
    cos, sin = jnp.cos(ang), jnp.sin(ang)
    ones = jnp.ones((L, DIFF_QK_DIM - ROT_DIM), F32)
    zeros = jnp.zeros((L, half), F32)
    zrest = jnp.zeros((L, DIFF_QK_DIM - ROT_DIM), F32)
    c_comp = jnp.concatenate([cos, cos, ones], axis=1)
    sa_comp = jnp.concatenate([zeros, sin, zrest], axis=1)
    sb_comp = jnp.concatenate([-sin, zeros, zrest], axis=1)
    two = lambda t: jnp.concatenate([t, t], axis=1)
    return two(c_comp), two(sa_comp), two(sb_comp)


def _attn_kernel(lam_ref, q_ref, k_ref, v_ref, g_ref, o_ref, q2_scr, m_scr, l_scr, acc_scr,
                 *, tq, out_scale):
    j = pl.program_id(2)

    @pl.when(j == 0)
    def _():
        q = q_ref[...]
        lane = lax.broadcasted_iota(jnp.int32, q.shape, 1)
        qs = q * (DIFF_QK_DIM ** -0.5)
        q2_scr[0:tq, :] = jnp.where(lane < DIFF_QK_DIM, qs, 0).astype(BF16)
        q2_scr[tq:2 * tq, :] = jnp.where(lane >= DIFF_QK_DIM, qs, 0).astype(BF16)
        m_scr[...] = jnp.full(m_scr.shape, -jnp.inf, F32)
        l_scr[...] = jnp.zeros(l_scr.shape, F32)
        acc_scr[...] = jnp.zeros(acc_scr.shape, F32)

    s = lax.dot_general(q2_scr[...], k_ref[...], (((1,), (1,)), ((), ())),
                        preferred_element_type=F32)
    m_prev = m_scr[...]
    m_new = jnp.maximum(m_prev, jnp.max(s, axis=1, keepdims=True))
    alpha = jnp.exp(m_prev - m_new)
    p = jnp.exp(s - m_new)
    l_scr[...] = alpha * l_scr[...] + jnp.sum(p, axis=1, keepdims=True)
    acc_scr[...] = alpha * acc_scr[...] + jnp.dot(p.astype(BF16), v_ref[...],
                                                  preferred_element_type=F32)
    m_scr[...] = m_new

    @pl.when(j == pl.num_programs(2) - 1)
    def _():
        o2 = acc_scr[...] / l_scr[...]
        o = o2[0:tq, :] - lam_ref[0] * o2[tq:2 * tq, :]
        ms = jnp.mean(o * o, axis=-1, keepdims=True)
        o_ref[...] = (o * lax.rsqrt(ms + RMS_EPS) * g_ref[...] * out_scale).astype(o_ref.dtype)


def _diff_attention(qkv, lam, subln_g, lam_init, tq, tk):
    L = qkv.shape[0]
    tq, tk = min(tq, L), min(tk, L)
    H = DIFF_HEADS
    return pl.pallas_call(
        functools.partial(_attn_kernel, tq=tq, out_scale=1.0 - lam_init),
        grid=(H, L // tq, L // tk),
        in_specs=[pl.BlockSpec(memory_space=pltpu.SMEM),
                  pl.BlockSpec((tq, LANES), lambda h, i, j: (i, h)),
                  pl.BlockSpec((tk, LANES), lambda h, i, j: (j, H + h)),
                  pl.BlockSpec((tk, LANES), lambda h, i, j: (j, 2 * H + h)),
                  pl.BlockSpec((1, LANES), lambda h, i, j: (0, 0))],
        out_specs=pl.BlockSpec((tq, LANES), lambda h, i, j: (i, h)),
        out_shape=jax.ShapeDtypeStruct((L, DIFF_WIDTH), BF16),
        scratch_shapes=[pltpu.VMEM((2 * tq, LANES), BF16),
                        pltpu.VMEM((2 * tq, 1), F32),
                        pltpu.VMEM((2 * tq, 1), F32),
                        pltpu.VMEM((2 * tq, LANES), F32)],
        compiler_params=_cp(("parallel", "parallel", "arbitrary")),
        name="diff_attn",
    )(lam, qkv, qkv, qkv, subln_g)


def _hg_masks():
    C = HG_TILE
    t = np.arange(C)[:, None]
    s = np.arange(C)[None, :]
    out = []
    for reverse in (False, True):
        for h in HG_LEVELS:
            same = (t // (2 * h)) == (s // (2 * h))
            t_late = (t % (2 * h)) >= h
            s_late = (s % (2 * h)) >= h
            m = same & (~t_late & s_late if reverse else t_late & ~s_late)
            out.append(m)
        for d in range(HG_BAND):
            same = (t // HG_BAND) == (s // HG_BAND)
            out.append(same & ((s == t + d) if reverse else (s == t - d)))
    return jnp.asarray(np.stack(out).astype(np.float32))


def _hg_tri():
    C = HG_TILE
    t = np.arange(C)[:, None]
    r = np.arange(C)[None, :]
    return jnp.asarray(np.stack([(r <= t), (r >= t)]).astype(np.float32))


def _hg_tile(qr, fr, v, lb, st, tri, mask_ref, reverse):
    C = HG_TILE
    n_lvl = len(HG_LEVELS)
    m0 = (n_lvl + HG_BAND) if reverse else 0
    q = qr * jax.nn.sigmoid(qr)
    f = lb + (1.0 - lb) * jax.nn.sigmoid(fr)
    k = 1.0 - f
    g = jnp.log(f)
    b = jnp.dot(tri, g, precision=HI, preferred_element_type=F32)
    b_edge = b[0:1, :] if reverse else b[C - 1:C, :]
    vb = v.astype(BF16)

    qc = (q * jnp.exp(b)).astype(BF16)
    kc = (k * jnp.exp(b_edge - b)).astype(BF16)
    o = lax.dot_general(qc, st.astype(BF16), (((1,), (1,)), ((), ())), preferred_element_type=F32)
    st_new = st * jnp.exp(b_edge) + jnp.dot(v.T.astype(BF16), kc, preferred_element_type=F32)

    a = jnp.zeros((C, C), F32)
    for li, h in enumerate(HG_LEVELS):
        rows = []
        for blk in range(C // (2 * h)):
            r = blk * 2 * h + (h if reverse else h - 1)
            rows.append(jnp.broadcast_to(b[r:r + 1, :], (2 * h, LANES)))
        bref = rows[0] if len(rows) == 1 else jnp.concatenate(rows, axis=0)
        e = jnp.exp(-jnp.abs(b - bref))
        ah = lax.dot_general((q * e).astype(BF16), (k * e).astype(BF16), (((1,), (1,)), ((), ())),
                             preferred_element_type=F32)
        a = a + ah * mask_ref[m0 + li]
    ed = None
    for d in range(HG_BAND):
        if d == 0:
            p = q * k
        else:
            sh = (C - d) if reverse else d
            shf = (C - (d - 1)) % C if reverse else d - 1
            fsh = f if shf == 0 else pltpu.roll(f, shf, 0)
            ed = fsh if ed is None else ed * fsh
            p = q * pltpu.roll(k, sh, 0) * ed
        a = a + jnp.sum(p, axis=1, keepdims=True) * mask_ref[m0 + n_lvl + d]
    o = o + jnp.dot(a.astype(BF16), vb, preferred_element_type=F32)
    return o, st_new


def _hgrn_kernel(qf_ref, qb_ref, ff_ref, fb_ref, vf_ref, vb_ref, lb_ref, tri_ref, mask_ref,
                 of_ref, ob_ref, stf_scr, stb_scr, *, n_tiles):
    @pl.when(pl.program_id(1) == 0)
    def _():
        stf_scr[...] = jnp.zeros(stf_scr.shape, F32)
        stb_scr[...] = jnp.zeros(stb_scr.shape, F32)

    C = HG_TILE
    lbf = lb_ref[0:1, :]
    lbb = lb_ref[1:2, :]

    def body(t, carry):
        rf = pl.multiple_of(t * C, C)
        rb = pl.multiple_of((n_tiles - 1 - t) * C, C)
        o_f, st_f = _hg_tile(qf_ref[pl.ds(rf, C), :], ff_ref[pl.ds(rf, C), :], vf_ref[pl.ds(rf, C), :],
                             lbf, stf_scr[...], tri_ref[0], mask_ref, False)
        of_ref[pl.ds(rf, C), :] = o_f
        stf_scr[...] = st_f
        o_b, st_b = _hg_tile(qb_ref[pl.ds(rb, C), :], fb_ref[pl.ds(rb, C), :], vb_ref[pl.ds(rb, C), :],
                             lbb, stb_scr[...], tri_ref[1], mask_ref, True)
        ob_ref[pl.ds(rb, C), :] = o_b
        stb_scr[...] = st_b
        return carry

    lax.fori_loop(0, n_tiles, body, 0)


def _hgrn2(proj, lb, rows):
    L = proj.shape[0]
    rows = min(rows, L)
    nb = L // rows
    H = HG_HEADS
    nm = len(HG_LEVELS) + HG_BAND
    blk = lambda col0, rev: pl.BlockSpec(
        (rows, LANES), (lambda h, i: (nb - 1 - i, col0 + h)) if rev else (lambda h, i: (i, col0 + h)))
    return pl.pallas_call(
        functools.partial(_hgrn_kernel, n_tiles=rows // HG_TILE),
        grid=(H, nb),
        in_specs=[blk(0, False), blk(0, True), blk(H, False), blk(2 * H, True),
                  blk(3 * H, False), blk(3 * H, True),
                  pl.BlockSpec((2, LANES), lambda h, i: (0, h)),
                  pl.BlockSpec((2, HG_TILE, HG_TILE), lambda h, i: (0, 0, 0)),
                  pl.BlockSpec((2 * nm, HG_TILE, HG_TILE), lambda h, i: (0, 0, 0))],
        out_specs=[blk(0, False), blk(0, True)],
        out_shape=[jax.ShapeDtypeStruct((L, HG_WIDTH), F32)] * 2,
        scratch_shapes=[pltpu.VMEM((LANES, LANES), F32)] * 2,
        compiler_params=_cp(("parallel", "arbitrary")),
        name="hgrn2",
    )(proj, proj, proj, proj, proj, proj, lb, _hg_tri(), _hg_masks())


def _outproj_kernel(x_ref, oa_ref, of_ref, ob_ref, gr_ref, ng_ref, gate_ref, lng_ref, lnb_ref, w_ref,
                    o_ref, lhs_scr):
    lhs_scr[:, 0:DIFF_WIDTH] = oa_ref[...]
    ng = ng_ref[...]
    for h in range(HG_HEADS):
        sl = slice(h * LANES, (h + 1) * LANES)
        o = of_ref[:, sl] + ob_ref[:, sl]
        ms = jnp.mean(o * o, axis=-1, keepdims=True)
        gr = gr_ref[:, sl]
        hg = o * lax.rsqrt(ms + RMS_EPS) * ng * (gr * jax.nn.sigmoid(gr))
        lhs_scr[:, DIFF_WIDTH + h * LANES:DIFF_WIDTH + (h + 1) * LANES] = hg.astype(BF16)
    y = jnp.dot(lhs_scr[...], w_ref[...], preferred_element_type=F32)
    z = ALPHA * x_ref[...] + (1.0 + gate_ref[...]) * y
    o_ref[...] = _layer_norm_rows(z, lng_ref[...], lnb_ref[...])


def _outproj_ln(x, oa, of, ob, proj_hg, norm_g, gate, ln_g, ln_b, w, tm):
    L = x.shape[0]
    tm = min(tm, L)
    row = lambda i: (i, 0)
    vec = pl.BlockSpec((1, D_MODEL), lambda i: (0, 0))
    return pl.pallas_call(
        _outproj_kernel,
        grid=(L // tm,),
        in_specs=[pl.BlockSpec((tm, D_MODEL), row),
                  pl.BlockSpec((tm, DIFF_WIDTH), row),
                  pl.BlockSpec((tm, HG_WIDTH), row),
                  pl.BlockSpec((tm, HG_WIDTH), row),
                  pl.BlockSpec((tm, HG_WIDTH), lambda i: (i, 4)),
                  pl.BlockSpec((1, LANES), lambda i: (0, 0)),
                  vec, vec, vec,
                  pl.BlockSpec((D_MODEL, D_MODEL), lambda i: (0, 0))],
        out_specs=pl.BlockSpec((tm, D_MODEL), row),
        out_shape=jax.ShapeDtypeStruct((L, D_MODEL), F32),
        scratch_shapes=[pltpu.VMEM((tm, D_MODEL), BF16)],
        compiler_params=_cp(("parallel",)),
        name="outproj_ln",
    )(x, oa, of, ob, proj_hg, norm_g, gate, ln_g, ln_b, w)


def _mlp_kernel(x_ref, sc_ref, sh_ref, gate_ref, lng_ref, lnb_ref, w1_ref, w2_ref, o_ref, h_scr, acc_scr):
    j = pl.program_id(1)

    @pl.when(j == 0)
    def _():
        h_scr[...] = (x_ref[...] * (1.0 + sc_ref[...]) + sh_ref[...]).astype(BF16)
        acc_scr[...] = jnp.zeros(acc_scr.shape, F32)

    a = jnp.maximum(jnp.dot(h_scr[...], w1_ref[...], preferred_element_type=F32), 0.0)
    acc_scr[...] += jnp.dot((a * a).astype(BF16), w2_ref[...], preferred_element_type=F32)

    @pl.when(j == pl.num_programs(1) - 1)
    def _():
        z = ALPHA * x_ref[...] + (1.0 + gate_ref[...]) * acc_scr[...]
        o_ref[...] = _layer_norm_rows(z, lng_ref[...], lnb_ref[...])


def _mlp_ln(x, scale, shift, gate, ln_g, ln_b, w1, w2, tm, tf):
    L = x.shape[0]
    tm = min(tm, L)
    vec = pl.BlockSpec((1, D_MODEL), lambda i, j: (0, 0))
    return pl.pallas_call(
        _mlp_kernel,
        grid=(L // tm, D_FF // tf),
        in_specs=[pl.BlockSpec((tm, D_MODEL), lambda i, j: (i, 0)),
                  vec, vec, vec, vec, vec,
                  pl.BlockSpec((D_MODEL, tf), lambda i, j: (0, j)),
                  pl.BlockSpec((tf, D_MODEL), lambda i, j: (j, 0))],
        out_specs=pl.BlockSpec((tm, D_MODEL), lambda i, j: (i, 0)),
        out_shape=jax.ShapeDtypeStruct((L, D_MODEL), F32),
        scratch_shapes=[pltpu.VMEM((tm, D_MODEL), BF16), pltpu.VMEM((tm, D_MODEL), F32)],
        compiler_params=_cp(("parallel", "arbitrary")),
        name="mlp_ln",
    )(x, scale, shift, gate, ln_g, ln_b, w1, w2)


def _s5_operators(lam_re, lam_im, log_dt, b_re, b_im, c_re, c_im):
    T, G, P, Hc, NB, GPB = S5_T, S5_GROUPS, S5_STATE, S5_GROUP, S5_BLOCKS, S5_GPB
    dt = jnp.exp(log_dt)[..., None]
    zr, zi = lam_re * dt, lam_im * dt
    mag = jnp.exp(zr)
    ab_re, ab_im = mag * jnp.cos(zi), mag * jnp.sin(zi)
    den = lam_re * lam_re + lam_im * lam_im
    coef_re = ((ab_re - 1.0) * lam_re + ab_im * lam_im) / den
    coef_im = (ab_im * lam_re - (ab_re - 1.0) * lam_im) / den
    bb_re = coef_re[..., None] * b_re - coef_im[..., None] * b_im
    bb_im = coef_re[..., None] * b_im + coef_im[..., None] * b_re

    def apow(k):
        k = jnp.asarray(k, F32)[:, None, None, None]
        m = jnp.exp(zr[None] * k)
        return m * jnp.cos(zi[None] * k), m * jnp.sin(zi[None] * k)

    eye = jnp.eye(GPB, dtype=F32)
    blocks = lambda t: t.reshape(t.shape[0], NB, GPB, *t.shape[2:])

    pr, pi = apow(np.arange(T))
    ca_re = c_re[None] * pr[:, :, :, None, :] - c_im[None] * pi[:, :, :, None, :]
    ca_im = c_re[None] * pi[:, :, :, None, :] + c_im[None] * pr[:, :, :, None, :]
    lagk = (jnp.einsum('kdgcp,dgph->kdgch', ca_re, bb_re, precision=HI)
            - jnp.einsum('kdgcp,dgph->kdgch', ca_im, bb_im, precision=HI))
    kf, kb = lagk[:, 0], lagk[:, 1]
    kall = jnp.concatenate([kb[:0:-1], (kf[0] + kb[0])[None], kf[1:]], axis=0)
    idx = np.arange(T)[None, :] - np.arange(T)[:, None] + T - 1
    full = kall[idx]
    full = full.reshape(T, T, NB, GPB, Hc, Hc)
    w_toep = jnp.einsum('stbgch,gj->bsghtjc', full, eye).reshape(NB, S5_ROW, S5_ROW)

    sr, si = apow(np.arange(T))
    ef_re, ef_im = sr[::-1, 0], si[::-1, 0]
    eb_re, eb_im = sr[:, 1], si[:, 1]
    def state_w(e_re, e_im, d):
        re = e_re[..., None] * bb_re[d][None] - e_im[..., None] * bb_im[d][None]
        im = e_re[..., None] * bb_im[d][None] + e_im[..., None] * bb_re[d][None]
        x = jnp.stack([re, im], axis=0).reshape(2, T, NB, GPB, P, Hc)
        w = jnp.einsum('rsbjph,gj->bsghrjp', x, eye)
        w = w.reshape(NB, T, GPB, Hc, 2, GPB * P // LANES, LANES)
        return jnp.swapaxes(w, 4, 5).reshape(NB, S5_ROW, S5_SW)
    w_state = jnp.concatenate([state_w(ef_re, ef_im, 0), state_w(eb_re, eb_im, 1)], axis=2)

    cr, ci = apow(np.arange(1, T + 1))
    def carry_w(p_re, p_im, d):
        re = c_re[d][None] * p_re[:, :, None, :] - c_im[d][None] * p_im[:, :, None, :]
        im = c_re[d][None] * p_im[:, :, None, :] + c_im[d][None] * p_re[:, :, None, :]
        x = jnp.stack([re, -im], axis=0).reshape(2, T, NB, GPB, Hc, P)
        w = jnp.einsum('rtbgcp,gj->brjptgc', x, eye)
        w = w.reshape(NB, 2, GPB * P // LANES, LANES, S5_ROW)
        return jnp.swapaxes(w, 1, 2).reshape(NB, S5_SW, S5_ROW)
    w_carry = jnp.concatenate([carry_w(cr[:, 0], ci[:, 0], 0),
                               carry_w(cr[::-1, 1], ci[::-1, 1], 1)], axis=1)
    return w_toep.astype(BF16), w_state.astype(BF16), w_carry.astype(BF16), (zr, zi)


def _s5_scan_powers(zr, zi, n_steps):
    k = (S5_T * (2.0 ** np.arange(n_steps))).astype(np.float32)[:, None, None, None]
    m = jnp.exp(zr[None] * k)
    re, im = m * jnp.cos(zi[None] * k), m * jnp.sin(zi[None] * k)
    x = jnp.stack([re, im], axis=0)
    x = x.reshape(2, n_steps, 2, S5_BLOCKS, S5_GPB * S5_STATE // LANES, LANES)
    x = jnp.transpose(x, (2, 1, 3, 4, 0, 5))
    return x.reshape(2, n_steps, S5_BLOCKS * S5_SW)


def _s5_fold(x_ref, sc, sh, tc):
    parts = [x_ref[pl.ds(t, tc, stride=S5_T), :] * (1.0 + sc) + sh for t in range(S5_T)]
    return parts


def _s5_state_kernel(x_ref, sc_ref, sh_ref, w_ref, zf_ref, zb_ref, *, tc):
    parts = _s5_fold(x_ref, sc_ref[...], sh_ref[...], tc)
    u = jnp.concatenate([p.astype(BF16) for p in parts], axis=1)
    z = jnp.dot(u, w_ref[0], preferred_element_type=F32)
    zf_ref[...] = z[:, :S5_SW]
    zb_ref[...] = z[:, S5_SW:]


def _s5_scan_kernel(zf_ref, zb_ref, af_ref, ab_ref, sf_ref, sb_ref, *, n_rows, n_steps):
    row = lax.broadcasted_iota(jnp.int32, (n_rows, LANES), 0)

    def scan(z_ref, a_ref, reverse):
        re, im = z_ref[:, 0:LANES], z_ref[:, LANES:2 * LANES]
        def shifted(v, sh):
            if reverse:
                return jnp.where(row < n_rows - sh, pltpu.roll(v, n_rows - sh, 0), 0.0)
            return jnp.where(row >= sh, pltpu.roll(v, sh, 0), 0.0)
        for k in range(n_steps):
            sh = 2 ** k
            ar, ai = a_ref[0, k:k + 1, 0:LANES], a_ref[0, k:k + 1, LANES:2 * LANES]
            rs, is_ = shifted(re, sh), shifted(im, sh)
            re, im = re + ar * rs - ai * is_, im + ar * is_ + ai * rs
        return shifted(re, 1), shifted(im, 1)

    re, im = scan(zf_ref, af_ref, False)
    sf_ref[:, 0:LANES] = re.astype(BF16)
    sf_ref[:, LANES:2 * LANES] = im.astype(BF16)
    re, im = scan(zb_ref, ab_ref, True)
    sb_ref[:, 0:LANES] = re.astype(BF16)
    sb_ref[:, LANES:2 * LANES] = im.astype(BF16)


def _s5_out_kernel(x_ref, sc_ref, sh_ref, d_ref, sf_ref, sb_ref, wt_ref, wc_ref, o_ref, *, tc, n_half):
    j = pl.program_id(2)
    sc, sh, d = sc_ref[...], sh_ref[...], d_ref[...]
    parts = _s5_fold(x_ref, sc, sh, tc)
    u = jnp.concatenate([p.astype(BF16) for p in parts], axis=1)
    s = jnp.concatenate([sf_ref[...], sb_ref[...]], axis=1)
    y = (jnp.dot(u, wt_ref[0], preferred_element_type=F32)
         + jnp.dot(s, wc_ref[0], preferred_element_type=F32))
    tph = S5_T // n_half
    for jj in range(n_half):
        @pl.when(j == jj)
        def _():
            for tl in range(tph):
                t = jj * tph + tl
                o_ref[pl.ds(t, tc, stride=S5_T), :] = y[:, tl * LANES:(tl + 1) * LANES] + d * parts[t]


def _s5_mixer_core(x, scale, shift, d_skip, ops, tc):
    w_toep, w_state, w_carry, (zr, zi) = ops
    L = x.shape[0]
    n_rows = L // S5_T
    NB = S5_BLOCKS
    n_steps = int(math.log2(n_rows))
    assert 2 ** n_steps == n_rows
    tc = min(tc, n_rows)
    vecb = pl.BlockSpec((1, LANES), lambda b, i: (0, b))

    zf, zb = pl.pallas_call(
        functools.partial(_s5_state_kernel, tc=tc),
        grid=(NB, n_rows // tc),
        in_specs=[pl.BlockSpec((tc * S5_T, LANES), lambda b, i: (i, b)), vecb, vecb,
                  pl.BlockSpec((1, S5_ROW, 2 * S5_SW), lambda b, i: (b, 0, 0))],
        out_specs=[pl.BlockSpec((tc, S5_SW), lambda b, i: (i, b))] * 2,
        out_shape=[jax.ShapeDtypeStruct((n_rows, NB * S5_SW), F32)] * 2,
        compiler_params=_cp(("parallel", "parallel")),
        name="s5_state",
    )(x, scale, shift, w_state)

    pw = _s5_scan_powers(zr, zi, n_steps)
    cw = 2 * LANES
    sf, sb = pl.pallas_call(
        functools.partial(_s5_scan_kernel, n_rows=n_rows, n_steps=n_steps),
        grid=(NB * S5_SW // cw,),
        in_specs=[pl.BlockSpec((n_rows, cw), lambda n: (0, n)),
                  pl.BlockSpec((n_rows, cw), lambda n: (0, n)),
                  pl.BlockSpec((1, n_steps, cw), lambda n: (0, 0, n)),
                  pl.BlockSpec((1, n_steps, cw), lambda n: (1, 0, n))],
        out_specs=[pl.BlockSpec((n_rows, cw), lambda n: (0, n))] * 2,
        out_shape=[jax.ShapeDtypeStruct((n_rows, NB * S5_SW), BF16)] * 2,
        compiler_params=_cp(("parallel",)),
        name="s5_scan",
    )(zf, zb, pw, pw)

    n_half = 2
    nh = S5_ROW // n_half
    vec3 = pl.BlockSpec((1, LANES), lambda b, i, j: (0, b))
    return pl.pallas_call(
        functools.partial(_s5_out_kernel, tc=tc, n_half=n_half),
        grid=(NB, n_rows // tc, n_half),
        in_specs=[pl.BlockSpec((tc * S5_T, LANES), lambda b, i, j: (i, b)), vec3, vec3, vec3,
                  pl.BlockSpec((tc, S5_SW), lambda b, i, j: (i, b)),
                  pl.BlockSpec((tc, S5_SW), lambda b, i, j: (i, b)),
                  pl.BlockSpec((1, S5_ROW, nh), lambda b, i, j: (b, 0, j)),
                  pl.BlockSpec((1, 2 * S5_SW, nh), lambda b, i, j: (b, 0, j))],
        out_specs=pl.BlockSpec((tc * S5_T, LANES), lambda b, i, j: (i, b)),
        out_shape=jax.ShapeDtypeStruct((L, D_MODEL), F32),
        compiler_params=_cp(("parallel", "parallel", "arbitrary")),
        name="s5_out",
    )(x, scale, shift, d_skip, sf, sb, w_toep, w_carry)


def _glu_kernel(x_ref, y_ref, gate_ref, lng_ref, lnb_ref, wa_ref, wg_ref, o_ref, lhs_scr, mix_scr, *, tn):
    j = pl.program_id(1)

    @pl.when(j == 0)
    def _():
        lhs_scr[...] = jax.nn.gelu(y_ref[...]).astype(BF16)

    a = jnp.dot(lhs_scr[...], wa_ref[...], preferred_element_type=F32)
    g = jnp.dot(lhs_scr[...], wg_ref[...], preferred_element_type=F32)
    mix = a * jax.nn.sigmoid(g)
    for jj in range(D_MODEL // tn):
        @pl.when(j == jj)
        def _():
            mix_scr[:, jj * tn:(jj + 1) * tn] = mix

    @pl.when(j == pl.num_programs(1) - 1)
    def _():
        z = ALPHA * x_ref[...] + (1.0 + gate_ref[...]) * mix_scr[...]
        o_ref[...] = _layer_norm_rows(z, lng_ref[...], lnb_ref[...])


def _glu_ln(x, y, gate, ln_g, ln_b, w_glu, tm, tn):
    L = x.shape[0]
    tm = min(tm, L)
    nj = D_MODEL // tn
    vec = pl.BlockSpec((1, D_MODEL), lambda i, j: (0, 0))
    return pl.pallas_call(
        functools.partial(_glu_kernel, tn=tn),
        grid=(L // tm, nj),
        in_specs=[pl.BlockSpec((tm, D_MODEL), lambda i, j: (i, 0)),
                  pl.BlockSpec((tm, D_MODEL), lambda i, j: (i, 0)),
                  vec, vec, vec,
                  pl.BlockSpec((D_MODEL, tn), lambda i, j: (0, j)),
                  pl.BlockSpec((D_MODEL, tn), lambda i, j: (0, nj + j))],
        out_specs=pl.BlockSpec((tm, D_MODEL), lambda i, j: (i, 0)),
        out_shape=jax.ShapeDtypeStruct((L, D_MODEL), F32),
        scratch_shapes=[pltpu.VMEM((tm, D_MODEL), BF16), pltpu.VMEM((tm, D_MODEL), F32)],
        compiler_params=_cp(("parallel", "arbitrary")),
        name="glu_ln",
    )(x, y, gate, ln_g, ln_b, w_glu, w_glu)


def kernel(x, c, ada_w, ada_b, ln_g, ln_b, mix_w_in, mix_w_out, diff_lambda, diff_subln_g, hg_lower_bound,
           hg_norm_g, s5_lambda_re, s5_lambda_im, s5_log_dt, s5_b_re, s5_b_im, s5_c_re, s5_c_im, s5_d,
           s5_w_glu, mlp_w1, mlp_w2):
    B, L, D = x.shape
    assert B == 1 and D == D_MODEL
    xs = x.reshape(L, D)
    mod = _adaln(c, ada_w, ada_b)
    mods = lambda a: (mod[a, :, 0:D], mod[a, :, D:2 * D], mod[a, :, 2 * D:3 * D])
    lb_table = jnp.cumsum(jax.nn.softmax(hg_lower_bound.astype(F32), axis=1), axis=1)
    rot_tabs = _rotary_tables(L)

    for layer in range(DEPTH):
        shift, scale, gate = mods(2 * layer)
        lng, lnb = ln_g[layer, 0][None], ln_b[layer, 0][None]
        if layer % 2 == 0:
            e = layer // 2
            lam_init = 0.8 - 0.6 * math.exp(-0.3 * layer)
            w_in = mix_w_in[e].astype(BF16)
            qkv = _inproj(xs, scale, shift, w_in[:, :3 * DIFF_WIDTH], rot_tabs, 2, BF16, 512, DIFF_WIDTH)
            proj_hg = _inproj(xs, scale, shift, w_in[:, 3 * DIFF_WIDTH:], rot_tabs, 0, F32, 512, HG_WIDTH)
            lf = diff_lambda[e].astype(F32)
            lam = jnp.exp(jnp.sum(lf[0] * lf[1])) - jnp.exp(jnp.sum(lf[2] * lf[3])) + lam_init
            oa = _diff_attention(qkv, lam.reshape(1), diff_subln_g[e][None], lam_init, 512, 512)
            of, ob = _hgrn2(proj_hg, lb_table[:, layer], 512)
            xs = _outproj_ln(xs, oa, of, ob, proj_hg, hg_norm_g[e][None], gate, lng, lnb,
                             mix_w_out[e].astype(BF16), 256)
        else:
            o = layer // 2
            ops = _s5_operators(s5_lambda_re[o], s5_lambda_im[o], s5_log_dt[o], s5_b_re[o], s5_b_im[o],
                                s5_c_re[o], s5_c_im[o])
            y = _s5_mixer_core(xs, scale, shift, s5_d[o][None], ops, 512)
            xs = _glu_ln(xs, y, gate, lng, lnb, s5_w_glu[o].astype(BF16), 256, 1024)
        shift, scale, gate = mods(2 * layer + 1)
        xs = _mlp_ln(xs, scale, shift, gate, ln_g[layer, 1][None], ln_b[layer, 1][None],
                     mlp_w1[layer].astype(BF16), mlp_w2[layer].astype(BF16), 512, 512)
    return xs.reshape(B, L, D)
```

```python
import functools
import math

import numpy as np
import jax
import jax.numpy as jnp
from jax import lax
from jax.experimental import pallas as pl
from jax.experimental.pallas import tpu as pltpu

F32 = jnp.float32
BF16 = jnp.bfloat16

D_MODEL = 2048
DEPTH = 2
LANES = 128
SUBLANES = 8

DIFF_HEADS = 8
DIFF_QK_DIM = 64
DIFF_WIDTH = 1024
ROT_DIM = DIFF_QK_DIM // 4
ROPE_THETA = 500000.0
ATTN_ROWS = 128

HG_HEADS = 8
HG_WIDTH = 1024
HG_TILE = 128
HG_BAND = SUBLANES
HG_LEVELS = (64, 32, 16, 8)

S5_GROUP = 16
S5_GROUPS = D_MODEL // S5_GROUP
S5_STATE = 64
S5_T = 16
S5_GPB = LANES // S5_GROUP
S5_BLOCKS = D_MODEL // LANES
S5_ROW = S5_T * LANES
S5_SW = S5_GPB * S5_STATE * 2

D_FF = 4 * D_MODEL
ALPHA = (2 * DEPTH) ** 0.25
LN_EPS = 1e-5
RMS_EPS = 1e-6
HI = lax.Precision.HIGHEST

VMEM_LIMIT = 56 * 1024 * 1024


def _cp(sem):
    return pltpu.CompilerParams(dimension_semantics=sem, vmem_limit_bytes=VMEM_LIMIT)


def _layer_norm_rows(z, g, b):
    mu = jnp.mean(z, axis=-1, keepdims=True)
    zc = z - mu
    var = jnp.mean(zc * zc, axis=-1, keepdims=True)
    return zc * lax.rsqrt(var + LN_EPS) * g + b


def _adaln_kernel(c_ref, w_ref, b_ref, o_ref):
    c = c_ref[...]
    ca = c * jax.nn.sigmoid(c)
    w = w_ref[0]
    reps = w.shape[1] // LANES
    cb = jnp.concatenate([ca] * reps, axis=1)
    o_ref[0] = jnp.sum(w * cb, axis=0, keepdims=True) + b_ref[0]


def _adaln(c, ada_w, ada_b):
    n = 3 * D_MODEL
    tn = 768
    w = ada_w.reshape(2 * DEPTH, D_MODEL, n)
    b = ada_b.reshape(2 * DEPTH, 1, n)
    c_rep = jnp.broadcast_to(c.reshape(D_MODEL, 1), (D_MODEL, LANES))
    return pl.pallas_call(
        _adaln_kernel,
        grid=(2 * DEPTH, n // tn),
        in_specs=[pl.BlockSpec((D_MODEL, LANES), lambda a, j: (0, 0)),
                  pl.BlockSpec((1, D_MODEL, tn), lambda a, j: (a, 0, j)),
                  pl.BlockSpec((1, 1, tn), lambda a, j: (a, 0, j))],
        out_specs=pl.BlockSpec((1, 1, tn), lambda a, j: (a, 0, j)),
        out_shape=jax.ShapeDtypeStruct((2 * DEPTH, 1, n), F32),
        compiler_params=_cp(("parallel", "parallel")),
        name="adaln",
    )(c_rep, w, b)


def _inproj_kernel(x_ref, sc_ref, sh_ref, w_ref, cos_ref, sa_ref, sb_ref, o_ref, h_scr, *, n_rot):
    j = pl.program_id(1)

    @pl.when(j == 0)
    def _():
        h_scr[...] = (x_ref[...] * (1.0 + sc_ref[...]) + sh_ref[...]).astype(BF16)

    y = jnp.dot(h_scr[...], w_ref[...], preferred_element_type=F32)

    @pl.when(j < n_rot)
    def _():
        cs, sa, sb = cos_ref[...], sa_ref[...], sb_ref[...]
        for g in range(y.shape[1] // LANES):
            yg = y[:, g * LANES:(g + 1) * LANES]
            rot = (yg * cs + pltpu.roll(yg, ROT_DIM // 2, 1) * sa
                   + pltpu.roll(yg, LANES - ROT_DIM // 2, 1) * sb)
            o_ref[:, g * LANES:(g + 1) * LANES] = rot.astype(o_ref.dtype)

    @pl.when(j >= n_rot)
    def _():
        o_ref[...] = y.astype(o_ref.dtype)


def _inproj(x, scale, shift, w, rot_tabs, n_rot, out_dtype, tm, tn):
    L = x.shape[0]
    n = w.shape[1]
    tm = min(tm, L)
    return pl.pallas_call(
        functools.partial(_inproj_kernel, n_rot=n_rot),
        grid=(L // tm, n // tn),
        in_specs=[pl.BlockSpec((tm, D_MODEL), lambda i, j: (i, 0)),
                  pl.BlockSpec((1, D_MODEL), lambda i, j: (0, 0)),
                  pl.BlockSpec((1, D_MODEL), lambda i, j: (0, 0)),
                  pl.BlockSpec((D_MODEL, tn), lambda i, j: (0, j)),
                  pl.BlockSpec((tm, LANES), lambda i, j: (i, 0)),
                  pl.BlockSpec((tm, LANES), lambda i, j: (i, 0)),
                  pl.BlockSpec((tm, LANES), lambda i, j: (i, 0))],
        out_specs=pl.BlockSpec((tm, tn), lambda i, j: (i, j)),
        out_shape=jax.ShapeDtypeStruct((L, n), out_dtype),
        scratch_shapes=[pltpu.VMEM((tm, D_MODEL), BF16)],
        compiler_params=_cp(("parallel", "arbitrary")),
        name="inproj",
    )(x, scale, shift, w, *rot_tabs)


def _rotary_tables(L):
    half = ROT_DIM // 2
    pos = jnp.arange(L, dtype=F32)
    inv_freq = 1.0 / (ROPE_THETA ** (jnp.arange(0, ROT_DIM, 2, dtype=F32) / ROT_DIM))
    ang = pos[:, None] * inv_freq[None, :]
    cos, sin = jnp.cos(ang), jnp.sin(ang)
    ones = jnp.ones((L, DIFF_QK_DIM - ROT_DIM), F32)
    zeros = jnp.zeros((L, half), F32)
    zrest = jnp.zeros((L, DIFF_QK_DIM - ROT_DIM), F32)
    c_comp = jnp.concatenate([cos, cos, ones], axis=1)
    sa_comp = jnp.concatenate([zeros, sin, zrest], axis=1)
    sb_comp = jnp.concatenate([-sin, zeros, zrest], axis=1)
    two = lambda t: jnp.concatenate([t, t], axis=1)
    return two(c_comp), two(sa_comp), two(sb_comp)


def _attn_kernel(lam_ref, q_ref, k_ref, v_ref, g_ref, o_ref, q2_scr, m_scr, acc_scr,
                 *, tq, tk, out_scale):
    n_rows = 2 * tq
    q = q_ref[...]
    lane = lax.broadcasted_iota(jnp.int32, q.shape, 1)
    qs = q * (DIFF_QK_DIM ** -0.5)
    q2_scr[0:tq, :] = jnp.where(lane < DIFF_QK_DIM, qs, 0).astype(BF16)
    q2_scr[tq:n_rows, :] = jnp.where(lane >= DIFF_QK_DIM, qs, 0).astype(BF16)
    m_scr[...] = jnp.full(m_scr.shape, -jnp.inf, F32)
    acc_scr[...] = jnp.zeros(acc_scr.shape, F32)
    ones = jnp.ones((tk, LANES), BF16)
    n_col = tk // LANES

    def body(j, carry):
        c0 = pl.multiple_of(j * tk, tk)
        kc = k_ref[pl.ds(c0, tk), :]
        v1 = jnp.concatenate([v_ref[pl.ds(c0, tk), :], ones], axis=1)
        for r in range(n_rows // ATTN_ROWS):
            rs = slice(r * ATTN_ROWS, (r + 1) * ATTN_ROWS)
            s = lax.dot_general(q2_scr[rs, :], kc, (((1,), (1,)), ((), ())),
                                preferred_element_type=F32)
            mx = s[:, 0:LANES]
            for c in range(1, n_col):
                mx = jnp.maximum(mx, s[:, c * LANES:(c + 1) * LANES])
            m_prev = m_scr[rs, :]
            m_new = jnp.maximum(m_prev, jnp.max(mx, axis=1, keepdims=True))
            alpha = jnp.exp(m_prev - m_new)
            p = jnp.exp(s - jnp.concatenate([m_new] * n_col, axis=1)).astype(BF16)
            pv = jnp.dot(p, v1, preferred_element_type=F32)
            acc_scr[rs, :] = jnp.concatenate([alpha, alpha], axis=1) * acc_scr[rs, :] + pv
            m_scr[rs, :] = m_new
        return carry

    lax.fori_loop(0, k_ref.shape[0] // tk, body, 0)

    acc = acc_scr[...]
    o2 = acc[:, 0:LANES] / acc[:, LANES:2 * LANES]
    o = o2[0:tq, :] - lam_ref[0] * o2[tq:n_rows, :]
    ms = jnp.mean(o * o, axis=-1, keepdims=True)
    o_ref[...] = (o * lax.rsqrt(ms + RMS_EPS) * g_ref[...] * out_scale).astype(o_ref.dtype)


def _diff_attention(qkv, lam, subln_g, lam_init, tq, tk):
    L = qkv.shape[0]
    tq, tk = min(tq, L), min(tk, L)
    H = DIFF_HEADS
    return pl.pallas_call(
        functools.partial(_attn_kernel, tq=tq, tk=tk, out_scale=1.0 - lam_init),
        grid=(H, L // tq),
        in_specs=[pl.BlockSpec(memory_space=pltpu.SMEM),
                  pl.BlockSpec((tq, LANES), lambda h, i: (i, h)),
                  pl.BlockSpec((L, LANES), lambda h, i: (0, H + h)),
                  pl.BlockSpec((L, LANES), lambda h, i: (0, 2 * H + h)),
                  pl.BlockSpec((1, LANES), lambda h, i: (0, 0))],
        out_specs=pl.BlockSpec((tq, LANES), lambda h, i: (i, h)),
        out_shape=jax.ShapeDtypeStruct((L, DIFF_WIDTH), BF16),
        scratch_shapes=[pltpu.VMEM((2 * tq, LANES), BF16),
                        pltpu.VMEM((2 * tq, LANES), F32),
                        pltpu.VMEM((2 * tq, 2 * LANES), F32)],
        compiler_params=_cp(("parallel", "parallel")),
        name="diff_attn",
    )(lam, qkv, qkv, qkv, subln_g)


def _hg_masks():
    C = HG_TILE
    t = np.arange(C)[:, None]
    s = np.arange(C)[None, :]
    out = []
    for reverse in (False, True):
        for h in HG_LEVELS:
            same = (t // (2 * h)) == (s // (2 * h))
            t_late = (t % (2 * h)) >= h
            s_late = (s % (2 * h)) >= h
            m = same & (~t_late & s_late if reverse else t_late & ~s_late)
            out.append(m)
        for d in range(HG_BAND):
            same = (t // HG_BAND) == (s // HG_BAND)
            out.append(same & ((s == t + d) if reverse else (s == t - d)))
    return jnp.asarray(np.stack(out).astype(np.float32))


def _hg_tri():
    C = HG_TILE
    t = np.arange(C)[:, None]
    r = np.arange(C)[None, :]
    return jnp.asarray(np.stack([(r <= t), (r >= t)]).astype(np.float32))


def _hg_tile(qr, fr, v, lb, st, tri, mask_ref, reverse):
    C = HG_TILE
    n_lvl = len(HG_LEVELS)
    m0 = (n_lvl + HG_BAND) if reverse else 0
    q = qr * jax.nn.sigmoid(qr)
    f = lb + (1.0 - lb) * jax.nn.sigmoid(fr)
    k = 1.0 - f
    g = jnp.log(f)
    b = jnp.dot(tri, g, precision=HI, preferred_element_type=F32)
    b_edge = b[0:1, :] if reverse else b[C - 1:C, :]
    vb = v.astype(BF16)

    qc = (q * jnp.exp(b)).astype(BF16)
    kc = (k * jnp.exp(b_edge - b)).astype(BF16)
    o = lax.dot_general(qc, st.astype(BF16), (((1,), (1,)), ((), ())), preferred_element_type=F32)
    st_new = st * jnp.exp(b_edge) + jnp.dot(v.T.astype(BF16), kc, preferred_element_type=F32)

    a = jnp.zeros((C, C), F32)
    for li, h in enumerate(HG_LEVELS):
        rows = []
        for blk in range(C // (2 * h)):
            r = blk * 2 * h + (h if reverse else h - 1)
            rows.append(jnp.broadcast_to(b[r:r + 1, :], (2 * h, LANES)))
        bref = rows[0] if len(rows) == 1 else jnp.concatenate(rows, axis=0)
        e = jnp.exp(-jnp.abs(b - bref))
        ah = lax.dot_general((q * e).astype(BF16), (k * e).astype(BF16), (((1,), (1,)), ((), ())),
                             preferred_element_type=F32)
        a = a + ah * mask_ref[m0 + li]
    ed = None
    for d in range(HG_BAND):
        if d == 0:
            p = q * k
        else:
            sh = (C - d) if reverse else d
            shf = (C - (d - 1)) % C if reverse else d - 1
            fsh = f if shf == 0 else pltpu.roll(f, shf, 0)
            ed = fsh if ed is None else ed * fsh
            p = q * pltpu.roll(k, sh, 0) * ed
        a = a + jnp.sum(p, axis=1, keepdims=True) * mask_ref[m0 + n_lvl + d]
    o = o + jnp.dot(a.astype(BF16), vb, preferred_element_type=F32)
    return o, st_new


def _hgrn_kernel(qf_ref, qb_ref, ff_ref, fb_ref, vf_ref, vb_ref, lb_ref, tri_ref, mask_ref,
                 of_ref, ob_ref, stf_scr, stb_scr, *, n_tiles):
    @pl.when(pl.program_id(1) == 0)
    def _():
        stf_scr[...] = jnp.zeros(stf_scr.shape, F32)
        stb_scr[...] = jnp.zeros(stb_scr.shape, F32)

    C = HG_TILE
    lbf = lb_ref[0:1, :]
    lbb = lb_ref[1:2, :]

    def body(t, carry):
        rf = pl.multiple_of(t * C, C)
        rb = pl.multiple_of((n_tiles - 1 - t) * C, C)
        o_f, st_f = _hg_tile(qf_ref[pl.ds(rf, C), :], ff_ref[pl.ds(rf, C), :], vf_ref[pl.ds(rf, C), :],
                             lbf, stf_scr[...], tri_ref[0], mask_ref, False)
        of_ref[pl.ds(rf, C), :] = o_f
        stf_scr[...] = st_f
        o_b, st_b = _hg_tile(qb_ref[pl.ds(rb, C), :], fb_ref[pl.ds(rb, C), :], vb_ref[pl.ds(rb, C), :],
                             lbb, stb_scr[...], tri_ref[1], mask_ref, True)
        ob_ref[pl.ds(rb, C), :] = o_b
        stb_scr[...] = st_b
        return carry

    lax.fori_loop(0, n_tiles, body, 0)


def _hgrn2(proj, lb, rows):
    L = proj.shape[0]
    rows = min(rows, L)
    nb = L // rows
    H = HG_HEADS
    nm = len(HG_LEVELS) + HG_BAND
    blk = lambda col0, rev: pl.BlockSpec(
        (rows, LANES), (lambda h, i: (nb - 1 - i, col0 + h)) if rev else (lambda h, i: (i, col0 + h)))
    return pl.pallas_call(
        functools.partial(_hgrn_kernel, n_tiles=rows // HG_TILE),
        grid=(H, nb),
        in_specs=[blk(0, False), blk(0, True), blk(H, False), blk(2 * H, True),
                  blk(3 * H, False), blk(3 * H, True),
                  pl.BlockSpec((2, LANES), lambda h, i: (0, h)),
                  pl.BlockSpec((2, HG_TILE, HG_TILE), lambda h, i: (0, 0, 0)),
                  pl.BlockSpec((2 * nm, HG_TILE, HG_TILE), lambda h, i: (0, 0, 0))],
        out_specs=[blk(0, False), blk(0, True)],
        out_shape=[jax.ShapeDtypeStruct((L, HG_WIDTH), F32)] * 2,
        scratch_shapes=[pltpu.VMEM((LANES, LANES), F32)] * 2,
        compiler_params=_cp(("parallel", "arbitrary")),
        name="hgrn2",
    )(proj, proj, proj, proj, proj, proj, lb, _hg_tri(), _hg_masks())


def _outproj_kernel(x_ref, oa_ref, of_ref, ob_ref, gr_ref, ng_ref, gate_ref, lng_ref, lnb_ref, w_ref,
                    o_ref, lhs_scr):
    lhs_scr[:, 0:DIFF_WIDTH] = oa_ref[...]
    ng = ng_ref[...]
    for h in range(HG_HEADS):
        sl = slice(h * LANES, (h + 1) * LANES)
        o = of_ref[:, sl] + ob_ref[:, sl]
        ms = jnp.mean(o * o, axis=-1, keepdims=True)
        gr = gr_ref[:, sl]
        hg = o * lax.rsqrt(ms + RMS_EPS) * ng * (gr * jax.nn.sigmoid(gr))
        lhs_scr[:, DIFF_WIDTH + h * LANES:DIFF_WIDTH + (h + 1) * LANES] = hg.astype(BF16)
    y = jnp.dot(lhs_scr[...], w_ref[...], preferred_element_type=F32)
    z = ALPHA * x_ref[...] + (1.0 + gate_ref[...]) * y
    o_ref[...] = _layer_norm_rows(z, lng_ref[...], lnb_ref[...])


def _outproj_ln(x, oa, of, ob, proj_hg, norm_g, gate, ln_g, ln_b, w, tm):
    L = x.shape[0]
    tm = min(tm, L)
    row = lambda i: (i, 0)
    vec = pl.BlockSpec((1, D_MODEL), lambda i: (0, 0))
    return pl.pallas_call(
        _outproj_kernel,
        grid=(L // tm,),
        in_specs=[pl.BlockSpec((tm, D_MODEL), row),
                  pl.BlockSpec((tm, DIFF_WIDTH), row),
                  pl.BlockSpec((tm, HG_WIDTH), row),
                  pl.BlockSpec((tm, HG_WIDTH), row),
                  pl.BlockSpec((tm, HG_WIDTH), lambda i: (i, 4)),
                  pl.BlockSpec((1, LANES), lambda i: (0, 0)),
                  vec, vec, vec,
                  pl.BlockSpec((D_MODEL, D_MODEL), lambda i: (0, 0))],
        out_specs=pl.BlockSpec((tm, D_MODEL), row),
        out_shape=jax.ShapeDtypeStruct((L, D_MODEL), F32),
        scratch_shapes=[pltpu.VMEM((tm, D_MODEL), BF16)],
        compiler_params=_cp(("parallel",)),
        name="outproj_ln",
    )(x, oa, of, ob, proj_hg, norm_g, gate, ln_g, ln_b, w)


def _mlp_kernel(x_ref, sc_ref, sh_ref, gate_ref, lng_ref, lnb_ref, w1_ref, w2_ref, o_ref, h_scr, acc_scr):
    j = pl.program_id(1)

    @pl.when(j == 0)
    def _():
        h_scr[...] = (x_ref[...] * (1.0 + sc_ref[...]) + sh_ref[...]).astype(BF16)
        acc_scr[...] = jnp.zeros(acc_scr.shape, F32)

    a = jnp.maximum(jnp.dot(h_scr[...], w1_ref[...], preferred_element_type=F32), 0.0)
    acc_scr[...] += jnp.dot((a * a).astype(BF16), w2_ref[...], preferred_element_type=F32)

    @pl.when(j == pl.num_programs(1) - 1)
    def _():
        z = ALPHA * x_ref[...] + (1.0 + gate_ref[...]) * acc_scr[...]
        o_ref[...] = _layer_norm_rows(z, lng_ref[...], lnb_ref[...])


def _mlp_ln(x, scale, shift, gate, ln_g, ln_b, w1, w2, tm, tf):
    L = x.shape[0]
    tm = min(tm, L)
    vec = pl.BlockSpec((1, D_MODEL), lambda i, j: (0, 0))
    return pl.pallas_call(
        _mlp_kernel,
        grid=(L // tm, D_FF // tf),
        in_specs=[pl.BlockSpec((tm, D_MODEL), lambda i, j: (i, 0)),
                  vec, vec, vec, vec, vec,
                  pl.BlockSpec((D_MODEL, tf), lambda i, j: (0, j)),
                  pl.BlockSpec((tf, D_MODEL), lambda i, j: (j, 0))],
        out_specs=pl.BlockSpec((tm, D_MODEL), lambda i, j: (i, 0)),
        out_shape=jax.ShapeDtypeStruct((L, D_MODEL), F32),
        scratch_shapes=[pltpu.VMEM((tm, D_MODEL), BF16), pltpu.VMEM((tm, D_MODEL), F32)],
        compiler_params=_cp(("parallel", "arbitrary")),
        name="mlp_ln",
    )(x, scale, shift, gate, ln_g, ln_b, w1, w2)


def _s5_operators(lam_re, lam_im, log_dt, b_re, b_im, c_re, c_im):
    T, G, P, Hc, NB, GPB = S5_T, S5_GROUPS, S5_STATE, S5_GROUP, S5_BLOCKS, S5_GPB
    dt = jnp.exp(log_dt)[..., None]
    zr, zi = lam_re * dt, lam_im * dt
    mag = jnp.exp(zr)
    ab_re, ab_im = mag * jnp.cos(zi), mag * jnp.sin(zi)
    den = lam_re * lam_re + lam_im * lam_im
    coef_re = ((ab_re - 1.0) * lam_re + ab_im * lam_im) / den
    coef_im = (ab_im * lam_re - (ab_re - 1.0) * lam_im) / den
    bb_re = coef_re[..., None] * b_re - coef_im[..., None] * b_im
    bb_im = coef_re[..., None] * b_im + coef_im[..., None] * b_re

    def apow(k):
        k = jnp.asarray(k, F32)[:, None, None, None]
        m = jnp.exp(zr[None] * k)
        return m * jnp.cos(zi[None] * k), m * jnp.sin(zi[None] * k)

    eye = jnp.eye(GPB, dtype=F32)

    pr, pi = apow(np.arange(T))
    ca_re = c_re[None] * pr[:, :, :, None, :] - c_im[None] * pi[:, :, :, None, :]
    ca_im = c_re[None] * pi[:, :, :, None, :] + c_im[None] * pr[:, :, :, None, :]
    lagk = (jnp.einsum('kdgcp,dgph->kdgch', ca_re, bb_re, precision=HI)
            - jnp.einsum('kdgcp,dgph->kdgch', ca_im, bb_im, precision=HI))
    kf, kb = lagk[:, 0], lagk[:, 1]
    kall = jnp.concatenate([kb[:0:-1], (kf[0] + kb[0])[None], kf[1:]], axis=0)
    kall = kall.reshape(2 * T - 1, NB, GPB, Hc, Hc)
    bd = jnp.einsum('lbgch,gj->blghjc', kall, eye).reshape(NB, 2 * T - 1, LANES, LANES)

    sr, si = apow(np.arange(T))
    def state_tab(e_re, e_im, d):
        re = e_re[..., None] * bb_re[d][None] - e_im[..., None] * bb_im[d][None]
        im = e_re[..., None] * bb_im[d][None] + e_im[..., None] * bb_re[d][None]
        return jnp.stack([re, im], axis=0).reshape(2, T, NB, GPB, P, Hc)
    cs = jnp.stack([state_tab(sr[::-1, 0], si[::-1, 0], 0), state_tab(sr[:, 1], si[:, 1], 1)], axis=0)
    cs = jnp.transpose(cs, (3, 2, 4, 6, 0, 1, 5))
    cs = jnp.broadcast_to(cs[..., None, :], cs.shape[:-1] + (2, P))
    cs = cs.reshape(NB, T, LANES, 4 * LANES)

    cr, ci = apow(np.arange(1, T + 1))
    def carry_tab(p_re, p_im, d):
        re = c_re[d][None] * p_re[:, :, None, :] - c_im[d][None] * p_im[:, :, None, :]
        im = c_re[d][None] * p_im[:, :, None, :] + c_im[d][None] * p_re[:, :, None, :]
        return jnp.stack([re, -im], axis=0).reshape(2, T, NB, GPB, Hc, P)
    cc = jnp.stack([carry_tab(cr[:, 0], ci[:, 0], 0), carry_tab(cr[::-1, 1], ci[::-1, 1], 1)], axis=0)
    cc = jnp.transpose(cc, (3, 0, 1, 2, 6, 4, 5))
    cc = jnp.broadcast_to(cc[:, :, :, :, None], cc.shape[:4] + (2,) + cc.shape[4:])
    cc = cc.reshape(NB, 2, 2, T, LANES, LANES)
    return tuple(_s5_expand(bd, cs, cc)) + ((zr, zi),)


def _s5_expand_kernel(bd_ref, cs_ref, cc_ref, wt_ref, ws_ref, wc_ref):
    s = pl.program_id(1)
    n_q = S5_GPB * S5_STATE // LANES
    row = lax.broadcasted_iota(jnp.int32, (LANES, LANES), 0)
    col = lax.broadcasted_iota(jnp.int32, (LANES, LANES), 1)
    for t in range(S5_T):
        wt_ref[0, :, t * LANES:(t + 1) * LANES] = bd_ref[0, t - s + S5_T - 1].astype(BF16)
    for d in range(2):
        for q in range(n_q):
            own = (row // S5_GROUP) == (n_q // 2 * q + col // S5_STATE)
            for ri in range(2):
                src = cs_ref[0, 0, :, (2 * d + ri) * LANES:(2 * d + ri + 1) * LANES]
                c0 = ((d * n_q + q) * 2 + ri) * LANES
                ws_ref[0, :, c0:c0 + LANES] = jnp.where(own, src, 0.0).astype(BF16)
    q = (s // 2) % n_q
    own = (n_q // 2 * q + row // S5_STATE) == (col // S5_GROUP)
    for t in range(S5_T):
        wc_ref[0, :, t * LANES:(t + 1) * LANES] = jnp.where(own, cc_ref[0, 0, 0, t], 0.0).astype(BF16)


def _s5_expand(bd, cs, cc):
    NB, T = S5_BLOCKS, S5_T
    assert 2 * S5_SW // LANES == T and S5_GPB * S5_STATE // LANES == 4
    slab = pl.BlockSpec((1, LANES, S5_ROW), lambda b, s: (b, s, 0))
    return pl.pallas_call(
        _s5_expand_kernel,
        grid=(NB, T),
        in_specs=[pl.BlockSpec((1, 2 * T - 1, LANES, LANES), lambda b, s: (b, 0, 0, 0)),
                  pl.BlockSpec((1, 1, LANES, 4 * LANES), lambda b, s: (b, s, 0, 0)),
                  pl.BlockSpec((1, 1, 1, T, LANES, LANES), lambda b, s: (b, s // 8, s % 2, 0, 0, 0))],
        out_specs=[slab, slab, slab],
        out_shape=[jax.ShapeDtypeStruct((NB, S5_ROW, S5_ROW), BF16)] * 3,
        compiler_params=_cp(("parallel", "arbitrary")),
        name="s5_expand",
    )(bd, cs, cc)


def _s5_scan_powers(zr, zi, n_steps):
    k = (S5_T * (2.0 ** np.arange(n_steps))).astype(np.float32)[:, None, None, None]
    m = jnp.exp(zr[None] * k)
    re, im = m * jnp.cos(zi[None] * k), m * jnp.sin(zi[None] * k)
    x = jnp.stack([re, im], axis=0)
    x = x.reshape(2, n_steps, 2, S5_BLOCKS, S5_GPB * S5_STATE // LANES, LANES)
    x = jnp.transpose(x, (2, 1, 3, 4, 0, 5))
    return x.reshape(2, n_steps, S5_BLOCKS * S5_SW)


def _s5_fold(x_ref, sc, sh, tc):
    parts = [x_ref[pl.ds(t, tc, stride=S5_T), :] * (1.0 + sc) + sh for t in range(S5_T)]
    return parts


def _s5_state_kernel(x_ref, sc_ref, sh_ref, w_ref, zf_ref, zb_ref, *, tc):
    parts = _s5_fold(x_ref, sc_ref[...], sh_ref[...], tc)
    u = jnp.concatenate([p.astype(BF16) for p in parts], axis=1)
    z = jnp.dot(u, w_ref[0], preferred_element_type=F32)
    zf_ref[...] = z[:, :S5_SW]
    zb_ref[...] = z[:, S5_SW:]


def _s5_scan_kernel(zf_ref, zb_ref, af_ref, ab_ref, sf_ref, sb_ref, *, n_rows, n_steps):
    row = lax.broadcasted_iota(jnp.int32, (n_rows, LANES), 0)

    def scan(z_ref, a_ref, reverse):
        re, im = z_ref[:, 0:LANES], z_ref[:, LANES:2 * LANES]
        def shifted(v, sh):
            if reverse:
                return jnp.where(row < n_rows - sh, pltpu.roll(v, n_rows - sh, 0), 0.0)
            return jnp.where(row >= sh, pltpu.roll(v, sh, 0), 0.0)
        for k in range(n_steps):
            sh = 2 ** k
            ar, ai = a_ref[0, k:k + 1, 0:LANES], a_ref[0, k:k + 1, LANES:2 * LANES]
            rs, is_ = shifted(re, sh), shifted(im, sh)
            re, im = re + ar * rs - ai * is_, im + ar * is_ + ai * rs
        return shifted(re, 1), shifted(im, 1)

    re, im = scan(zf_ref, af_ref, False)
    sf_ref[:, 0:LANES] = re.astype(BF16)
    sf_ref[:, LANES:2 * LANES] = im.astype(BF16)
    re, im = scan(zb_ref, ab_ref, True)
    sb_ref[:, 0:LANES] = re.astype(BF16)
    sb_ref[:, LANES:2 * LANES] = im.astype(BF16)


def _s5_out_kernel(x_ref, sc_ref, sh_ref, d_ref, sf_ref, sb_ref, wt_ref, wc_ref, o_ref, *, tc, n_half):
    j = pl.program_id(2)
    sc, sh, d = sc_ref[...], sh_ref[...], d_ref[...]
    parts = _s5_fold(x_ref, sc, sh, tc)
    u = jnp.concatenate([p.astype(BF16) for p in parts], axis=1)
    s = jnp.concatenate([sf_ref[...], sb_ref[...]], axis=1)
    y = (jnp.dot(u, wt_ref[0], preferred_element_type=F32)
         + jnp.dot(s, wc_ref[0], preferred_element_type=F32))
    tph = S5_T // n_half
    for jj in range(n_half):
        @pl.when(j == jj)
        def _():
            for tl in range(tph):
                t = jj * tph + tl
                o_ref[pl.ds(t, tc, stride=S5_T), :] = y[:, tl * LANES:(tl + 1) * LANES] + d * parts[t]


def _s5_mixer_core(x, scale, shift, d_skip, ops, tc):
    w_toep, w_state, w_carry, (zr, zi) = ops
    L = x.shape[0]
    n_rows = L // S5_T
    NB = S5_BLOCKS
    n_steps = int(math.log2(n_rows))
    assert 2 ** n_steps == n_rows
    tc = min(tc, n_rows)
    vecb = pl.BlockSpec((1, LANES), lambda b, i: (0, b))

    zf, zb = pl.pallas_call(
        functools.partial(_s5_state_kernel, tc=tc),
        grid=(NB, n_rows // tc),
        in_specs=[pl.BlockSpec((tc * S5_T, LANES), lambda b, i: (i, b)), vecb, vecb,
                  pl.BlockSpec((1, S5_ROW, 2 * S5_SW), lambda b, i: (b, 0, 0))],
        out_specs=[pl.BlockSpec((tc, S5_SW), lambda b, i: (i, b))] * 2,
        out_shape=[jax.ShapeDtypeStruct((n_rows, NB * S5_SW), F32)] * 2,
        compiler_params=_cp(("parallel", "parallel")),
        name="s5_state",
    )(x, scale, shift, w_state)

    pw = _s5_scan_powers(zr, zi, n_steps)
    cw = 2 * LANES
    sf, sb = pl.pallas_call(
        functools.partial(_s5_scan_kernel, n_rows=n_rows, n_steps=n_steps),
        grid=(NB * S5_SW // cw,),
        in_specs=[pl.BlockSpec((n_rows, cw), lambda n: (0, n)),
                  pl.BlockSpec((n_rows, cw), lambda n: (0, n)),
                  pl.BlockSpec((1, n_steps, cw), lambda n: (0, 0, n)),
                  pl.BlockSpec((1, n_steps, cw), lambda n: (1, 0, n))],
        out_specs=[pl.BlockSpec((n_rows, cw), lambda n: (0, n))] * 2,
        out_shape=[jax.ShapeDtypeStruct((n_rows, NB * S5_SW), BF16)] * 2,
        compiler_params=_cp(("parallel",)),
        name="s5_scan",
    )(zf, zb, pw, pw)

    n_half = 2
    nh = S5_ROW // n_half
    vec3 = pl.BlockSpec((1, LANES), lambda b, i, j: (0, b))
    return pl.pallas_call(
        functools.partial(_s5_out_kernel, tc=tc, n_half=n_half),
        grid=(NB, n_rows // tc, n_half),
        in_specs=[pl.BlockSpec((tc * S5_T, LANES), lambda b, i, j: (i, b)), vec3, vec3, vec3,
                  pl.BlockSpec((tc, S5_SW), lambda b, i, j: (i, b)),
                  pl.BlockSpec((tc, S5_SW), lambda b, i, j: (i, b)),
                  pl.BlockSpec((1, S5_ROW, nh), lambda b, i, j: (b, 0, j)),
                  pl.BlockSpec((1, 2 * S5_SW, nh), lambda b, i, j: (b, 0, j))],
        out_specs=pl.BlockSpec((tc * S5_T, LANES), lambda b, i, j: (i, b)),
        out_shape=jax.ShapeDtypeStruct((L, D_MODEL), F32),
        compiler_params=_cp(("parallel", "parallel", "arbitrary")),
        name="s5_out",
    )(x, scale, shift, d_skip, sf, sb, w_toep, w_carry)


def _glu_kernel(x_ref, y_ref, gate_ref, lng_ref, lnb_ref, wa_ref, wg_ref, o_ref, lhs_scr, mix_scr, *, tn):
    j = pl.program_id(1)

    @pl.when(j == 0)
    def _():
        lhs_scr[...] = jax.nn.gelu(y_ref[...]).astype(BF16)

    a = jnp.dot(lhs_scr[...], wa_ref[...], preferred_element_type=F32)
    g = jnp.dot(lhs_scr[...], wg_ref[...], preferred_element_type=F32)
    mix = a * jax.nn.sigmoid(g)
    for jj in range(D_MODEL // tn):
        @pl.when(j == jj)
        def _():
            mix_scr[:, jj * tn:(jj + 1) * tn] = mix

    @pl.when(j == pl.num_programs(1) - 1)
    def _():
        z = ALPHA * x_ref[...] + (1.0 + gate_ref[...]) * mix_scr[...]
        o_ref[...] = _layer_norm_rows(z, lng_ref[...], lnb_ref[...])


def _glu_ln(x, y, gate, ln_g, ln_b, w_glu, tm, tn):
    L = x.shape[0]
    tm = min(tm, L)
    nj = D_MODEL // tn
    vec = pl.BlockSpec((1, D_MODEL), lambda i, j: (0, 0))
    return pl.pallas_call(
        functools.partial(_glu_kernel, tn=tn),
        grid=(L // tm, nj),
        in_specs=[pl.BlockSpec((tm, D_MODEL), lambda i, j: (i, 0)),
                  pl.BlockSpec((tm, D_MODEL), lambda i, j: (i, 0)),
                  vec, vec, vec,
                  pl.BlockSpec((D_MODEL, tn), lambda i, j: (0, j)),
                  pl.BlockSpec((D_MODEL, tn), lambda i, j: (0, nj + j))],
        out_specs=pl.BlockSpec((tm, D_MODEL), lambda i, j: (i, 0)),
        out_shape=jax.ShapeDtypeStruct((L, D_MODEL), F32),
        scratch_shapes=[pltpu.VMEM((tm, D_MODEL), BF16), pltpu.VMEM((tm, D_MODEL), F32)],
        compiler_params=_cp(("parallel", "arbitrary")),
        name="glu_ln",
    )(x, y, gate, ln_g, ln_b, w_glu, w_glu)


def kernel(x, c, ada_w, ada_b, ln_g, ln_b, mix_w_in, mix_w_out, diff_lambda, diff_subln_g, hg_lower_bound,
           hg_norm_g, s5_lambda_re, s5_lambda_im, s5_log_dt, s5_b_re, s5_b_im, s5_c_re, s5_c_im, s5_d,
           s5_w_glu, mlp_w1, mlp_w2):
    B, L, D = x.shape
    assert B == 1 and D == D_MODEL
    xs = x.reshape(L, D)
    mod = _adaln(c, ada_w, ada_b)
    mods = lambda a: (mod[a, :, 0:D], mod[a, :, D:2 * D], mod[a, :, 2 * D:3 * D])
    lb_table = jnp.cumsum(jax.nn.softmax(hg_lower_bound.astype(F32), axis=1), axis=1)
    rot_tabs = _rotary_tables(L)

    for layer in range(DEPTH):
        shift, scale, gate = mods(2 * layer)
        lng, lnb = ln_g[layer, 0][None], ln_b[layer, 0][None]
        if layer % 2 == 0:
            e = layer // 2
            lam_init = 0.8 - 0.6 * math.exp(-0.3 * layer)
            w_in = mix_w_in[e].astype(BF16)
            qkv = _inproj(xs, scale, shift, w_in[:, :3 * DIFF_WIDTH], rot_tabs, 2, BF16, 512, DIFF_WIDTH)
            proj_hg = _inproj(xs, scale, shift, w_in[:, 3 * DIFF_WIDTH:], rot_tabs, 0, F32, 512, HG_WIDTH)
            lf = diff_lambda[e].astype(F32)
            lam = jnp.exp(jnp.sum(lf[0] * lf[1])) - jnp.exp(jnp.sum(lf[2] * lf[3])) + lam_init
            oa = _diff_attention(qkv, lam.reshape(1), diff_subln_g[e][None], lam_init, 1024, 512)
            of, ob = _hgrn2(proj_hg, lb_table[:, layer], 512)
            xs = _outproj_ln(xs, oa, of, ob, proj_hg, hg_norm_g[e][None], gate, lng, lnb,
                             mix_w_out[e].astype(BF16), 256)
        else:
            o = layer // 2
            ops = _s5_operators(s5_lambda_re[o], s5_lambda_im[o], s5_log_dt[o], s5_b_re[o], s5_b_im[o],
                                s5_c_re[o], s5_c_im[o])
            y = _s5_mixer_core(xs, scale, shift, s5_d[o][None], ops, 512)
            xs = _glu_ln(xs, y, gate, lng, lnb, s5_w_glu[o].astype(BF16), 256, 1024)
        shift, scale, gate = mods(2 * layer + 1)
        xs = _mlp_ln(xs, scale, shift, gate, ln_g[layer, 1][None], ln_b[layer, 1][None],
                     mlp_w1[layer].astype(BF16), mlp_w2[layer].astype(BF16), 512, 512)
    return xs.reshape(B, L, D)
```

```python
import functools
import math

import numpy as np
import jax
import jax.numpy as jnp
from jax import lax
from jax.experimental import pallas as pl
from jax.experimental.pallas import tpu as pltpu

F32 = jnp.float32
BF16 = jnp.bfloat16

D_MODEL = 2048
DEPTH = 2
LANES = 128
SUBLANES = 8

DIFF_HEADS = 8
DIFF_QK_DIM = 64
DIFF_WIDTH = 1024
ROT_DIM = DIFF_QK_DIM // 4
ROPE_THETA = 500000.0
ATTN_ROWS = 128

HG_HEADS = 8
HG_WIDTH = 1024
HG_TILE = 128
HG_BAND = SUBLANES
HG_LEVELS = (64, 32, 16, 8)

S5_GROUP = 16
S5_GROUPS = D_MODEL // S5_GROUP
S5_STATE = 64
S5_T = 16
S5_GPB = LANES // S5_GROUP
S5_BLOCKS = D_MODEL // LANES
S5_ROW = S5_T * LANES
S5_SW = S5_GPB * S5_STATE * 2

D_FF = 4 * D_MODEL
ALPHA = (2 * DEPTH) ** 0.25
LN_EPS = 1e-5
RMS_EPS = 1e-6
HI = lax.Precision.HIGHEST

VMEM_LIMIT = 56 * 1024 * 1024


def _cp(sem):
    return pltpu.CompilerParams(dimension_semantics=sem, vmem_limit_bytes=VMEM_LIMIT)


def _layer_norm_rows(z, g, b):
    mu = jnp.mean(z, axis=-1, keepdims=True)
    zc = z - mu
    var = jnp.mean(zc * zc, axis=-1, keepdims=True)
    return zc * lax.rsqrt(var + LN_EPS) * g + b


def _adaln_kernel(c_ref, w_ref, b_ref, o_ref):
    c = c_ref[...]
    ca = c * jax.nn.sigmoid(c)
    w = w_ref[0]
    reps = w.shape[1] // LANES
    cb = jnp.concatenate([ca] * reps, axis=1)
    o_ref[0] = jnp.sum(w * cb, axis=0, keepdims=True) + b_ref[0]


def _adaln(c, ada_w, ada_b):
    n = 3 * D_MODEL
    tn = 768
    w = ada_w.reshape(2 * DEPTH, D_MODEL, n)
    b = ada_b.reshape(2 * DEPTH, 1, n)
    c_rep = jnp.broadcast_to(c.reshape(D_MODEL, 1), (D_MODEL, LANES))
    return pl.pallas_call(
        _adaln_kernel,
        grid=(2 * DEPTH, n // tn),
        in_specs=[pl.BlockSpec((D_MODEL, LANES), lambda a, j: (0, 0)),
                  pl.BlockSpec((1, D_MODEL, tn), lambda a, j: (a, 0, j)),
                  pl.BlockSpec((1, 1, tn), lambda a, j: (a, 0, j))],
        out_specs=pl.BlockSpec((1, 1, tn), lambda a, j: (a, 0, j)),
        out_shape=jax.ShapeDtypeStruct((2 * DEPTH, 1, n), F32),
        compiler_params=_cp(("parallel", "parallel")),
        name="adaln",
    )(c_rep, w, b)


def _inproj_kernel(x_ref, sc_ref, sh_ref, w_ref, cos_ref, sa_ref, sb_ref, o_ref, h_scr, *, n_rot):
    j = pl.program_id(1)

    @pl.when(j == 0)
    def _():
        h_scr[...] = (x_ref[...] * (1.0 + sc_ref[...]) + sh_ref[...]).astype(BF16)

    y = jnp.dot(h_scr[...], w_ref[...], preferred_element_type=F32)

    @pl.when(j < n_rot)
    def _():
        cs, sa, sb = cos_ref[...], sa_ref[...], sb_ref[...]
        for g in range(y.shape[1] // LANES):
            yg = y[:, g * LANES:(g + 1) * LANES]
            rot = (yg * cs + pltpu.roll(yg, ROT_DIM // 2, 1) * sa
                   + pltpu.roll(yg, LANES - ROT_DIM // 2, 1) * sb)
            o_ref[:, g * LANES:(g + 1) * LANES] = rot.astype(o_ref.dtype)

    @pl.when(j >= n_rot)
    def _():
        o_ref[...] = y.astype(o_ref.dtype)


def _inproj(x, scale, shift, w, rot_tabs, n_rot, out_dtype, tm, tn):
    L = x.shape[0]
    n = w.shape[1]
    tm = min(tm, L)
    return pl.pallas_call(
        functools.partial(_inproj_kernel, n_rot=n_rot),
        grid=(L // tm, n // tn),
        in_specs=[pl.BlockSpec((tm, D_MODEL), lambda i, j: (i, 0)),
                  pl.BlockSpec((1, D_MODEL), lambda i, j: (0, 0)),
                  pl.BlockSpec((1, D_MODEL), lambda i, j: (0, 0)),
                  pl.BlockSpec((D_MODEL, tn), lambda i, j: (0, j)),
                  pl.BlockSpec((tm, LANES), lambda i, j: (i, 0)),
                  pl.BlockSpec((tm, LANES), lambda i, j: (i, 0)),
                  pl.BlockSpec((tm, LANES), lambda i, j: (i, 0))],
        out_specs=pl.BlockSpec((tm, tn), lambda i, j: (i, j)),
        out_shape=jax.ShapeDtypeStruct((L, n), out_dtype),
        scratch_shapes=[pltpu.VMEM((tm, D_MODEL), BF16)],
        compiler_params=_cp(("parallel", "arbitrary")),
        name="inproj",
    )(x, scale, shift, w, *rot_tabs)


def _rotary_tables(L):
    half = ROT_DIM // 2
    pos = jnp.arange(L, dtype=F32)
    inv_freq = 1.0 / (ROPE_THETA ** (jnp.arange(0, ROT_DIM, 2, dtype=F32) / ROT_DIM))
    ang = pos[:, None] * inv_freq[None, :]
    cos, sin = jnp.cos(ang), jnp.sin(ang)
    ones = jnp.ones((L, DIFF_QK_DIM - ROT_DIM), F32)
    zeros = jnp.zeros((L, half), F32)
    zrest = jnp.zeros((L, DIFF_QK_DIM - ROT_DIM), F32)
    c_comp = jnp.concatenate([cos, cos, ones], axis=1)
    sa_comp = jnp.concatenate([zeros, sin, zrest], axis=1)
    sb_comp = jnp.concatenate([-sin, zeros, zrest], axis=1)
    two = lambda t: jnp.concatenate([t, t], axis=1)
    return two(c_comp), two(sa_comp), two(sb_comp)


def _attn_kernel(lam_ref, q_ref, k_ref, v_ref, g_ref, o_ref, q2_scr, m_scr, acc_scr, s_scr,
                 *, tq, tk, out_scale):
    n_rows = 2 * tq
    n_chunks = n_rows // ATTN_ROWS
    n_kv = k_ref.shape[0] // tk
    q = q_ref[...]
    lane = lax.broadcasted_iota(jnp.int32, q.shape, 1)
    qs = q * (DIFF_QK_DIM ** -0.5)
    q2_scr[0:tq, :] = jnp.where(lane < DIFF_QK_DIM, qs, 0).astype(BF16)
    q2_scr[tq:n_rows, :] = jnp.where(lane >= DIFF_QK_DIM, qs, 0).astype(BF16)
    m_scr[...] = jnp.full(m_scr.shape, -jnp.inf, F32)
    acc_scr[...] = jnp.zeros(acc_scr.shape, F32)
    ones = jnp.ones((tk, LANES), BF16)
    n_col = tk // LANES

    def scores(r, kc):
        rs = slice(r * ATTN_ROWS, (r + 1) * ATTN_ROWS)
        return lax.dot_general(q2_scr[rs, :], kc, (((1,), (1,)), ((), ())), preferred_element_type=F32)

    kc0 = k_ref[0:tk, :]
    for r in range(n_chunks):
        s_scr[r * ATTN_ROWS:(r + 1) * ATTN_ROWS, :] = scores(r, kc0)

    def body(j, carry):
        c0 = pl.multiple_of(j * tk, tk)
        c1 = pl.multiple_of(jnp.minimum(j + 1, n_kv - 1) * tk, tk)
        k_next = k_ref[pl.ds(c1, tk), :]
        v1 = jnp.concatenate([v_ref[pl.ds(c0, tk), :], ones], axis=1)
        for r in range(n_chunks):
            rs = slice(r * ATTN_ROWS, (r + 1) * ATTN_ROWS)
            s = s_scr[rs, :]
            s_scr[rs, :] = scores(r, k_next)
            mx = s[:, 0:LANES]
            for c in range(1, n_col):
                mx = jnp.maximum(mx, s[:, c * LANES:(c + 1) * LANES])
            m_prev = m_scr[rs, :]
            m_new = jnp.maximum(m_prev, jnp.max(mx, axis=1, keepdims=True))
            alpha = jnp.exp(m_prev - m_new)
            p = jnp.exp(s - jnp.concatenate([m_new] * n_col, axis=1)).astype(BF16)
            pv = jnp.dot(p, v1, preferred_element_type=F32)
            acc_scr[rs, :] = jnp.concatenate([alpha, alpha], axis=1) * acc_scr[rs, :] + pv
            m_scr[rs, :] = m_new
        return carry

    lax.fori_loop(0, n_kv, body, 0)

    acc = acc_scr[...]
    o2 = acc[:, 0:LANES] / acc[:, LANES:2 * LANES]
    o = o2[0:tq, :] - lam_ref[0] * o2[tq:n_rows, :]
    ms = jnp.mean(o * o, axis=-1, keepdims=True)
    o_ref[...] = (o * lax.rsqrt(ms + RMS_EPS) * g_ref[...] * out_scale).astype(o_ref.dtype)


def _diff_attention(qkv, lam, subln_g, lam_init, tq, tk):
    L = qkv.shape[0]
    tq, tk = min(tq, L), min(tk, L)
    H = DIFF_HEADS
    return pl.pallas_call(
        functools.partial(_attn_kernel, tq=tq, tk=tk, out_scale=1.0 - lam_init),
        grid=(H, L // tq),
        in_specs=[pl.BlockSpec(memory_space=pltpu.SMEM),
                  pl.BlockSpec((tq, LANES), lambda h, i: (i, h)),
                  pl.BlockSpec((L, LANES), lambda h, i: (0, H + h)),
                  pl.BlockSpec((L, LANES), lambda h, i: (0, 2 * H + h)),
                  pl.BlockSpec((1, LANES), lambda h, i: (0, 0))],
        out_specs=pl.BlockSpec((tq, LANES), lambda h, i: (i, h)),
        out_shape=jax.ShapeDtypeStruct((L, DIFF_WIDTH), BF16),
        scratch_shapes=[pltpu.VMEM((2 * tq, LANES), BF16),
                        pltpu.VMEM((2 * tq, LANES), F32),
                        pltpu.VMEM((2 * tq, 2 * LANES), F32),
                        pltpu.VMEM((2 * tq, tk), F32)],
        compiler_params=_cp(("parallel", "parallel")),
        name="diff_attn",
    )(lam, qkv, qkv, qkv, subln_g)


def _hg_masks():
    C = HG_TILE
    t = np.arange(C)[:, None]
    s = np.arange(C)[None, :]
    out = []
    for reverse in (False, True):
        for h in HG_LEVELS:
            same = (t // (2 * h)) == (s // (2 * h))
            t_late = (t % (2 * h)) >= h
            s_late = (s % (2 * h)) >= h
            m = same & (~t_late & s_late if reverse else t_late & ~s_late)
            out.append(m)
        for d in range(HG_BAND):
            same = (t // HG_BAND) == (s // HG_BAND)
            out.append(same & ((s == t + d) if reverse else (s == t - d)))
    return jnp.asarray(np.stack(out).astype(np.float32))


def _hg_tri():
    C = HG_TILE
    t = np.arange(C)[:, None]
    r = np.arange(C)[None, :]
    return jnp.asarray(np.stack([(r <= t), (r >= t)]).astype(np.float32))


def _hg_tile(qr, fr, v, lb, st, tri, mask_ref, reverse):
    C = HG_TILE
    n_lvl = len(HG_LEVELS)
    m0 = (n_lvl + HG_BAND) if reverse else 0
    q = qr * jax.nn.sigmoid(qr)
    f = lb + (1.0 - lb) * jax.nn.sigmoid(fr)
    k = 1.0 - f
    g = jnp.log(f)
    b = jnp.dot(tri, g, precision=HI, preferred_element_type=F32)
    b_edge = b[0:1, :] if reverse else b[C - 1:C, :]
    vb = v.astype(BF16)

    qc = (q * jnp.exp(b)).astype(BF16)
    kc = (k * jnp.exp(b_edge - b)).astype(BF16)
    o = lax.dot_general(qc, st.astype(BF16), (((1,), (1,)), ((), ())), preferred_element_type=F32)
    st_new = st * jnp.exp(b_edge) + jnp.dot(v.T.astype(BF16), kc, preferred_element_type=F32)

    a = jnp.zeros((C, C), F32)
    for li, h in enumerate(HG_LEVELS):
        rows = []
        for blk in range(C // (2 * h)):
            r = blk * 2 * h + (h if reverse else h - 1)
            rows.append(jnp.broadcast_to(b[r:r + 1, :], (2 * h, LANES)))
        bref = rows[0] if len(rows) == 1 else jnp.concatenate(rows, axis=0)
        e = jnp.exp(-jnp.abs(b - bref))
        ah = lax.dot_general((q * e).astype(BF16), (k * e).astype(BF16), (((1,), (1,)), ((), ())),
                             preferred_element_type=F32)
        a = a + ah * mask_ref[m0 + li]
    ed = None
    for d in range(HG_BAND):
        if d == 0:
            p = q * k
        else:
            sh = (C - d) if reverse else d
            shf = (C - (d - 1)) % C if reverse else d - 1
            fsh = f if shf == 0 else pltpu.roll(f, shf, 0)
            ed = fsh if ed is None else ed * fsh
            p = q * pltpu.roll(k, sh, 0) * ed
        a = a + jnp.sum(p, axis=1, keepdims=True) * mask_ref[m0 + n_lvl + d]
    o = o + jnp.dot(a.astype(BF16), vb, preferred_element_type=F32)
    return o, st_new


def _hgrn_kernel(qf_ref, qb_ref, ff_ref, fb_ref, vf_ref, vb_ref, lb_ref, tri_ref, mask_ref,
                 of_ref, ob_ref, stf_scr, stb_scr, *, n_tiles):
    @pl.when(pl.program_id(1) == 0)
    def _():
        stf_scr[...] = jnp.zeros(stf_scr.shape, F32)
        stb_scr[...] = jnp.zeros(stb_scr.shape, F32)

    C = HG_TILE
    lbf = lb_ref[0:1, :]
    lbb = lb_ref[1:2, :]

    def body(t, carry):
        rf = pl.multiple_of(t * C, C)
        rb = pl.multiple_of((n_tiles - 1 - t) * C, C)
        o_f, st_f = _hg_tile(qf_ref[pl.ds(rf, C), :], ff_ref[pl.ds(rf, C), :], vf_ref[pl.ds(rf, C), :],
                             lbf, stf_scr[...], tri_ref[0], mask_ref, False)
        of_ref[pl.ds(rf, C), :] = o_f
        stf_scr[...] = st_f
        o_b, st_b = _hg_tile(qb_ref[pl.ds(rb, C), :], fb_ref[pl.ds(rb, C), :], vb_ref[pl.ds(rb, C), :],
                             lbb, stb_scr[...], tri_ref[1], mask_ref, True)
        ob_ref[pl.ds(rb, C), :] = o_b
        stb_scr[...] = st_b
        return carry

    lax.fori_loop(0, n_tiles, body, 0)


def _hgrn2(proj, lb, rows):
    L = proj.shape[0]
    rows = min(rows, L)
    nb = L // rows
    H = HG_HEADS
    nm = len(HG_LEVELS) + HG_BAND
    blk = lambda col0, rev: pl.BlockSpec(
        (rows, LANES), (lambda h, i: (nb - 1 - i, col0 + h)) if rev else (lambda h, i: (i, col0 + h)))
    return pl.pallas_call(
        functools.partial(_hgrn_kernel, n_tiles=rows // HG_TILE),
        grid=(H, nb),
        in_specs=[blk(0, False), blk(0, True), blk(H, False), blk(2 * H, True),
                  blk(3 * H, False), blk(3 * H, True),
                  pl.BlockSpec((2, LANES), lambda h, i: (0, h)),
                  pl.BlockSpec((2, HG_TILE, HG_TILE), lambda h, i: (0, 0, 0)),
                  pl.BlockSpec((2 * nm, HG_TILE, HG_TILE), lambda h, i: (0, 0, 0))],
        out_specs=[blk(0, False), blk(0, True)],
        out_shape=[jax.ShapeDtypeStruct((L, HG_WIDTH), F32)] * 2,
        scratch_shapes=[pltpu.VMEM((LANES, LANES), F32)] * 2,
        compiler_params=_cp(("parallel", "arbitrary")),
        name="hgrn2",
    )(proj, proj, proj, proj, proj, proj, lb, _hg_tri(), _hg_masks())


def _outproj_kernel(x_ref, oa_ref, of_ref, ob_ref, gr_ref, ng_ref, gate_ref, lng_ref, lnb_ref, w_ref,
                    o_ref, lhs_scr):
    lhs_scr[:, 0:DIFF_WIDTH] = oa_ref[...]
    ng = ng_ref[...]
    for h in range(HG_HEADS):
        sl = slice(h * LANES, (h + 1) * LANES)
        o = of_ref[:, sl] + ob_ref[:, sl]
        ms = jnp.mean(o * o, axis=-1, keepdims=True)
        gr = gr_ref[:, sl]
        hg = o * lax.rsqrt(ms + RMS_EPS) * ng * (gr * jax.nn.sigmoid(gr))
        lhs_scr[:, DIFF_WIDTH + h * LANES:DIFF_WIDTH + (h + 1) * LANES] = hg.astype(BF16)
    y = jnp.dot(lhs_scr[...], w_ref[...], preferred_element_type=F32)
    z = ALPHA * x_ref[...] + (1.0 + gate_ref[...]) * y
    o_ref[...] = _layer_norm_rows(z, lng_ref[...], lnb_ref[...])


def _outproj_ln(x, oa, of, ob, proj_hg, norm_g, gate, ln_g, ln_b, w, tm):
    L = x.shape[0]
    tm = min(tm, L)
    row = lambda i: (i, 0)
    vec = pl.BlockSpec((1, D_MODEL), lambda i: (0, 0))
    return pl.pallas_call(
        _outproj_kernel,
        grid=(L // tm,),
        in_specs=[pl.BlockSpec((tm, D_MODEL), row),
                  pl.BlockSpec((tm, DIFF_WIDTH), row),
                  pl.BlockSpec((tm, HG_WIDTH), row),
                  pl.BlockSpec((tm, HG_WIDTH), row),
                  pl.BlockSpec((tm, HG_WIDTH), lambda i: (i, 4)),
                  pl.BlockSpec((1, LANES), lambda i: (0, 0)),
                  vec, vec, vec,
                  pl.BlockSpec((D_MODEL, D_MODEL), lambda i: (0, 0))],
        out_specs=pl.BlockSpec((tm, D_MODEL), row),
        out_shape=jax.ShapeDtypeStruct((L, D_MODEL), F32),
        scratch_shapes=[pltpu.VMEM((tm, D_MODEL), BF16)],
        compiler_params=_cp(("parallel",)),
        name="outproj_ln",
    )(x, oa, of, ob, proj_hg, norm_g, gate, ln_g, ln_b, w)


def _mlp_kernel(x_ref, sc_ref, sh_ref, gate_ref, lng_ref, lnb_ref, w1_ref, w2_ref, o_ref, h_scr, acc_scr):
    j = pl.program_id(1)

    @pl.when(j == 0)
    def _():
        h_scr[...] = (x_ref[...] * (1.0 + sc_ref[...]) + sh_ref[...]).astype(BF16)
        acc_scr[...] = jnp.zeros(acc_scr.shape, F32)

    a = jnp.maximum(jnp.dot(h_scr[...], w1_ref[...], preferred_element_type=F32), 0.0)
    acc_scr[...] += jnp.dot((a * a).astype(BF16), w2_ref[...], preferred_element_type=F32)

    @pl.when(j == pl.num_programs(1) - 1)
    def _():
        z = ALPHA * x_ref[...] + (1.0 + gate_ref[...]) * acc_scr[...]
        o_ref[...] = _layer_norm_rows(z, lng_ref[...], lnb_ref[...])


def _mlp_ln(x, scale, shift, gate, ln_g, ln_b, w1, w2, tm, tf):
    L = x.shape[0]
    tm = min(tm, L)
    vec = pl.BlockSpec((1, D_MODEL), lambda i, j: (0, 0))
    return pl.pallas_call(
        _mlp_kernel,
        grid=(L // tm, D_FF // tf),
        in_specs=[pl.BlockSpec((tm, D_MODEL), lambda i, j: (i, 0)),
                  vec, vec, vec, vec, vec,
                  pl.BlockSpec((D_MODEL, tf), lambda i, j: (0, j)),
                  pl.BlockSpec((tf, D_MODEL), lambda i, j: (j, 0))],
        out_specs=pl.BlockSpec((tm, D_MODEL), lambda i, j: (i, 0)),
        out_shape=jax.ShapeDtypeStruct((L, D_MODEL), F32),
        scratch_shapes=[pltpu.VMEM((tm, D_MODEL), BF16), pltpu.VMEM((tm, D_MODEL), F32)],
        compiler_params=_cp(("parallel", "arbitrary")),
        name="mlp_ln",
    )(x, scale, shift, gate, ln_g, ln_b, w1, w2)


def _s5_operators(lam_re, lam_im, log_dt, b_re, b_im, c_re, c_im):
    T, G, P, Hc, NB, GPB = S5_T, S5_GROUPS, S5_STATE, S5_GROUP, S5_BLOCKS, S5_GPB
    dt = jnp.exp(log_dt)[..., None]
    zr, zi = lam_re * dt, lam_im * dt
    mag = jnp.exp(zr)
    ab_re, ab_im = mag * jnp.cos(zi), mag * jnp.sin(zi)
    den = lam_re * lam_re + lam_im * lam_im
    coef_re = ((ab_re - 1.0) * lam_re + ab_im * lam_im) / den
    coef_im = (ab_im * lam_re - (ab_re - 1.0) * lam_im) / den
    bb_re = coef_re[..., None] * b_re - coef_im[..., None] * b_im
    bb_im = coef_re[..., None] * b_im + coef_im[..., None] * b_re

    def apow(k):
        k = jnp.asarray(k, F32)[:, None, None, None]
        m = jnp.exp(zr[None] * k)
        return m * jnp.cos(zi[None] * k), m * jnp.sin(zi[None] * k)

    eye = jnp.eye(GPB, dtype=F32)

    pr, pi = apow(np.arange(T))
    ca_re = c_re[None] * pr[:, :, :, None, :] - c_im[None] * pi[:, :, :, None, :]
    ca_im = c_re[None] * pi[:, :, :, None, :] + c_im[None] * pr[:, :, :, None, :]
    lagk = (jnp.einsum('kdgcp,dgph->kdgch', ca_re, bb_re, precision=HI)
            - jnp.einsum('kdgcp,dgph->kdgch', ca_im, bb_im, precision=HI))
    kf, kb = lagk[:, 0], lagk[:, 1]
    kall = jnp.concatenate([kb[:0:-1], (kf[0] + kb[0])[None], kf[1:]], axis=0)
    kall_t = jnp.transpose(kall.reshape(2 * T - 1, NB, GPB, Hc, Hc), (1, 0, 2, 4, 3))
    bd = kall_t[:, :, :, :, None, :] * eye[None, None, :, None, :, None]
    bd = bd.reshape(NB, 2 * T - 1, LANES, LANES)

    sr, si = apow(np.arange(T))
    def e_tab(v):
        v = jnp.stack([v[::-1, 0], v[:, 1]], axis=1).reshape(T, 2, NB, GPB, P)
        return jnp.transpose(v, (2, 0, 3, 1, 4))[:, :, :, None]
    def bt_tab(v):
        return jnp.transpose(v.reshape(2, NB, GPB, P, Hc), (1, 2, 4, 0, 3))[:, None]
    e_re, e_im, t_re, t_im = e_tab(sr), e_tab(si), bt_tab(bb_re), bt_tab(bb_im)
    cs = jnp.stack([e_re * t_re - e_im * t_im, e_re * t_im + e_im * t_re], axis=5)
    cs = jnp.broadcast_to(cs[..., None, :], cs.shape[:-1] + (2, P))
    cs = cs.reshape(NB, T, LANES, 4 * LANES)

    cr, ci = apow(np.arange(1, T + 1))
    def p_tab(v):
        v = jnp.stack([v[:, 0], v[::-1, 1]], axis=1).reshape(T, 2, NB, GPB, P)
        return jnp.transpose(v, (2, 1, 0, 4, 3))[..., None]
    def c_tab(v):
        return jnp.transpose(v.reshape(2, NB, GPB, Hc, P), (1, 0, 4, 2, 3))[:, :, None]
    p_re, p_im, k_re, k_im = p_tab(cr), p_tab(ci), c_tab(c_re), c_tab(c_im)
    cc = jnp.stack([k_re * p_re - k_im * p_im, -(k_re * p_im + k_im * p_re)], axis=2)
    cc = jnp.broadcast_to(cc[:, :, :, :, None], cc.shape[:4] + (2,) + cc.shape[4:])
    cc = cc.reshape(NB, 2, 2, T, LANES, LANES)
    return tuple(_s5_expand(bd, cs, cc)) + ((zr, zi),)


def _s5_expand_kernel(bd_ref, cs_ref, cc_ref, wt_ref, ws_ref, wc_ref):
    s = pl.program_id(1)
    n_q = S5_GPB * S5_STATE // LANES
    row = lax.broadcasted_iota(jnp.int32, (LANES, LANES), 0)
    col = lax.broadcasted_iota(jnp.int32, (LANES, LANES), 1)
    for t in range(S5_T):
        wt_ref[0, :, t * LANES:(t + 1) * LANES] = bd_ref[0, t - s + S5_T - 1].astype(BF16)
    for d in range(2):
        for q in range(n_q):
            own = (row // S5_GROUP) == (n_q // 2 * q + col // S5_STATE)
            for ri in range(2):
                src = cs_ref[0, 0, :, (2 * d + ri) * LANES:(2 * d + ri + 1) * LANES]
                c0 = ((d * n_q + q) * 2 + ri) * LANES
                ws_ref[0, :, c0:c0 + LANES] = jnp.where(own, src, 0.0).astype(BF16)
    q = (s // 2) % n_q
    own = (n_q // 2 * q + row // S5_STATE) == (col // S5_GROUP)
    for t in range(S5_T):
        wc_ref[0, :, t * LANES:(t + 1) * LANES] = jnp.where(own, cc_ref[0, 0, 0, t], 0.0).astype(BF16)


def _s5_expand(bd, cs, cc):
    NB, T = S5_BLOCKS, S5_T
    assert 2 * S5_SW // LANES == T and S5_GPB * S5_STATE // LANES == 4
    slab = pl.BlockSpec((1, LANES, S5_ROW), lambda b, s: (b, s, 0))
    return pl.pallas_call(
        _s5_expand_kernel,
        grid=(NB, T),
        in_specs=[pl.BlockSpec((1, 2 * T - 1, LANES, LANES), lambda b, s: (b, 0, 0, 0)),
                  pl.BlockSpec((1, 1, LANES, 4 * LANES), lambda b, s: (b, s, 0, 0)),
                  pl.BlockSpec((1, 1, 1, T, LANES, LANES), lambda b, s: (b, s // 8, s % 2, 0, 0, 0))],
        out_specs=[slab, slab, slab],
        out_shape=[jax.ShapeDtypeStruct((NB, S5_ROW, S5_ROW), BF16)] * 3,
        compiler_params=_cp(("parallel", "arbitrary")),
        name="s5_expand",
    )(bd, cs, cc)


def _s5_scan_powers(zr, zi, n_steps):
    k = (S5_T * (2.0 ** np.arange(n_steps))).astype(np.float32)[:, None, None, None]
    m = jnp.exp(zr[None] * k)
    re, im = m * jnp.cos(zi[None] * k), m * jnp.sin(zi[None] * k)
    x = jnp.stack([re, im], axis=0)
    x = x.reshape(2, n_steps, 2, S5_BLOCKS, S5_GPB * S5_STATE // LANES, LANES)
    x = jnp.transpose(x, (2, 1, 3, 4, 0, 5))
    return x.reshape(2, n_steps, S5_BLOCKS * S5_SW)


def _s5_fold(x_ref, sc, sh, tc):
    parts = [x_ref[pl.ds(t, tc, stride=S5_T), :] * (1.0 + sc) + sh for t in range(S5_T)]
    return parts


def _s5_state_kernel(x_ref, sc_ref, sh_ref, w_ref, zf_ref, zb_ref, *, tc):
    parts = _s5_fold(x_ref, sc_ref[...], sh_ref[...], tc)
    u = jnp.concatenate([p.astype(BF16) for p in parts], axis=1)
    z = jnp.dot(u, w_ref[0], preferred_element_type=F32)
    zf_ref[...] = z[:, :S5_SW]
    zb_ref[...] = z[:, S5_SW:]


def _s5_scan_kernel(zf_ref, zb_ref, af_ref, ab_ref, sf_ref, sb_ref, *, n_rows, n_steps):
    row = lax.broadcasted_iota(jnp.int32, (n_rows, LANES), 0)

    def scan(z_ref, a_ref, reverse):
        re, im = z_ref[:, 0:LANES], z_ref[:, LANES:2 * LANES]
        def shifted(v, sh):
            if reverse:
                return jnp.where(row < n_rows - sh, pltpu.roll(v, n_rows - sh, 0), 0.0)
            return jnp.where(row >= sh, pltpu.roll(v, sh, 0), 0.0)
        for k in range(n_steps):
            sh = 2 ** k
            ar, ai = a_ref[0, k:k + 1, 0:LANES], a_ref[0, k:k + 1, LANES:2 * LANES]
            rs, is_ = shifted(re, sh), shifted(im, sh)
            re, im = re + ar * rs - ai * is_, im + ar * is_ + ai * rs
        return shifted(re, 1), shifted(im, 1)

    re, im = scan(zf_ref, af_ref, False)
    sf_ref[:, 0:LANES] = re.astype(BF16)
    sf_ref[:, LANES:2 * LANES] = im.astype(BF16)
    re, im = scan(zb_ref, ab_ref, True)
    sb_ref[:, 0:LANES] = re.astype(BF16)
    sb_ref[:, LANES:2 * LANES] = im.astype(BF16)


def _s5_out_kernel(x_ref, sc_ref, sh_ref, d_ref, sf_ref, sb_ref, wt_ref, wc_ref, o_ref, *, tc, n_half):
    j = pl.program_id(2)
    sc, sh, d = sc_ref[...], sh_ref[...], d_ref[...]
    parts = _s5_fold(x_ref, sc, sh, tc)
    u = jnp.concatenate([p.astype(BF16) for p in parts], axis=1)
    s = jnp.concatenate([sf_ref[...], sb_ref[...]], axis=1)
    y = (jnp.dot(u, wt_ref[0], preferred_element_type=F32)
         + jnp.dot(s, wc_ref[0], preferred_element_type=F32))
    tph = S5_T // n_half
    for jj in range(n_half):
        @pl.when(j == jj)
        def _():
            for tl in range(tph):
                t = jj * tph + tl
                o_ref[pl.ds(t, tc, stride=S5_T), :] = y[:, tl * LANES:(tl + 1) * LANES] + d * parts[t]


def _s5_mixer_core(x, scale, shift, d_skip, ops, tc):
    w_toep, w_state, w_carry, (zr, zi) = ops
    L = x.shape[0]
    n_rows = L // S5_T
    NB = S5_BLOCKS
    n_steps = int(math.log2(n_rows))
    assert 2 ** n_steps == n_rows
    tc = min(tc, n_rows)
    vecb = pl.BlockSpec((1, LANES), lambda b, i: (0, b))

    zf, zb = pl.pallas_call(
        functools.partial(_s5_state_kernel, tc=tc),
        grid=(NB, n_rows // tc),
        in_specs=[pl.BlockSpec((tc * S5_T, LANES), lambda b, i: (i, b)), vecb, vecb,
                  pl.BlockSpec((1, S5_ROW, 2 * S5_SW), lambda b, i: (b, 0, 0))],
        out_specs=[pl.BlockSpec((tc, S5_SW), lambda b, i: (i, b))] * 2,
        out_shape=[jax.ShapeDtypeStruct((n_rows, NB * S5_SW), F32)] * 2,
        compiler_params=_cp(("parallel", "parallel")),
        name="s5_state",
    )(x, scale, shift, w_state)

    pw = _s5_scan_powers(zr, zi, n_steps)
    cw = 2 * LANES
    sf, sb = pl.pallas_call(
        functools.partial(_s5_scan_kernel, n_rows=n_rows, n_steps=n_steps),
        grid=(NB * S5_SW // cw,),
        in_specs=[pl.BlockSpec((n_rows, cw), lambda n: (0, n)),
                  pl.BlockSpec((n_rows, cw), lambda n: (0, n)),
                  pl.BlockSpec((1, n_steps, cw), lambda n: (0, 0, n)),
                  pl.BlockSpec((1, n_steps, cw), lambda n: (1, 0, n))],
        out_specs=[pl.BlockSpec((n_rows, cw), lambda n: (0, n))] * 2,
        out_shape=[jax.ShapeDtypeStruct((n_rows, NB * S5_SW), BF16)] * 2,
        compiler_params=_cp(("parallel",)),
        name="s5_scan",
    )(zf, zb, pw, pw)

    n_half = 2
    nh = S5_ROW // n_half
    vec3 = pl.BlockSpec((1, LANES), lambda b, i, j: (0, b))
    return pl.pallas_call(
        functools.partial(_s5_out_kernel, tc=tc, n_half=n_half),
        grid=(NB, n_rows // tc, n_half),
        in_specs=[pl.BlockSpec((tc * S5_T, LANES), lambda b, i, j: (i, b)), vec3, vec3, vec3,
                  pl.BlockSpec((tc, S5_SW), lambda b, i, j: (i, b)),
                  pl.BlockSpec((tc, S5_SW), lambda b, i, j: (i, b)),
                  pl.BlockSpec((1, S5_ROW, nh), lambda b, i, j: (b, 0, j)),
                  pl.BlockSpec((1, 2 * S5_SW, nh), lambda b, i, j: (b, 0, j))],
        out_specs=pl.BlockSpec((tc * S5_T, LANES), lambda b, i, j: (i, b)),
        out_shape=jax.ShapeDtypeStruct((L, D_MODEL), F32),
        compiler_params=_cp(("parallel", "parallel", "arbitrary")),
        name="s5_out",
    )(x, scale, shift, d_skip, sf, sb, w_toep, w_carry)


def _glu_kernel(x_ref, y_ref, gate_ref, lng_ref, lnb_ref, wa_ref, wg_ref, o_ref, lhs_scr, mix_scr, *, tn):
    j = pl.program_id(1)

    @pl.when(j == 0)
    def _():
        lhs_scr[...] = jax.nn.gelu(y_ref[...]).astype(BF16)

    a = jnp.dot(lhs_scr[...], wa_ref[...], preferred_element_type=F32)
    g = jnp.dot(lhs_scr[...], wg_ref[...], preferred_element_type=F32)
    mix = a * jax.nn.sigmoid(g)
    for jj in range(D_MODEL // tn):
        @pl.when(j == jj)
        def _():
            mix_scr[:, jj * tn:(jj + 1) * tn] = mix

    @pl.when(j == pl.num_programs(1) - 1)
    def _():
        z = ALPHA * x_ref[...] + (1.0 + gate_ref[...]) * mix_scr[...]
        o_ref[...] = _layer_norm_rows(z, lng_ref[...], lnb_ref[...])


def _glu_ln(x, y, gate, ln_g, ln_b, w_glu, tm, tn):
    L = x.shape[0]
    tm = min(tm, L)
    nj = D_MODEL // tn
    vec = pl.BlockSpec((1, D_MODEL), lambda i, j: (0, 0))
    return pl.pallas_call(
        functools.partial(_glu_kernel, tn=tn),
        grid=(L // tm, nj),
        in_specs=[pl.BlockSpec((tm, D_MODEL), lambda i, j: (i, 0)),
                  pl.BlockSpec((tm, D_MODEL), lambda i, j: (i, 0)),
                  vec, vec, vec,
                  pl.BlockSpec((D_MODEL, tn), lambda i, j: (0, j)),
                  pl.BlockSpec((D_MODEL, tn), lambda i, j: (0, nj + j))],
        out_specs=pl.BlockSpec((tm, D_MODEL), lambda i, j: (i, 0)),
        out_shape=jax.ShapeDtypeStruct((L, D_MODEL), F32),
        scratch_shapes=[pltpu.VMEM((tm, D_MODEL), BF16), pltpu.VMEM((tm, D_MODEL), F32)],
        compiler_params=_cp(("parallel", "arbitrary")),
        name="glu_ln",
    )(x, y, gate, ln_g, ln_b, w_glu, w_glu)


def kernel(x, c, ada_w, ada_b, ln_g, ln_b, mix_w_in, mix_w_out, diff_lambda, diff_subln_g, hg_lower_bound,
           hg_norm_g, s5_lambda_re, s5_lambda_im, s5_log_dt, s5_b_re, s5_b_im, s5_c_re, s5_c_im, s5_d,
           s5_w_glu, mlp_w1, mlp_w2):
    B, L, D = x.shape
    assert B == 1 and D == D_MODEL
    xs = x.reshape(L, D)
    mod = _adaln(c, ada_w, ada_b)
    mods = lambda a: (mod[a, :, 0:D], mod[a, :, D:2 * D], mod[a, :, 2 * D:3 * D])
    lb_table = jnp.cumsum(jax.nn.softmax(hg_lower_bound.astype(F32), axis=1), axis=1)
    rot_tabs = _rotary_tables(L)

    for layer in range(DEPTH):
        shift, scale, gate = mods(2 * layer)
        lng, lnb = ln_g[layer, 0][None], ln_b[layer, 0][None]
        if layer % 2 == 0:
            e = layer // 2
            lam_init = 0.8 - 0.6 * math.exp(-0.3 * layer)
            w_in = mix_w_in[e].astype(BF16)
            qkv = _inproj(xs, scale, shift, w_in[:, :3 * DIFF_WIDTH], rot_tabs, 2, BF16, 512, DIFF_WIDTH)
            proj_hg = _inproj(xs, scale, shift, w_in[:, 3 * DIFF_WIDTH:], rot_tabs, 0, F32, 512, HG_WIDTH)
            lf = diff_lambda[e].astype(F32)
            lam = jnp.exp(jnp.sum(lf[0] * lf[1])) - jnp.exp(jnp.sum(lf[2] * lf[3])) + lam_init
            oa = _diff_attention(qkv, lam.reshape(1), diff_subln_g[e][None], lam_init, 1024, 1024)
            of, ob = _hgrn2(proj_hg, lb_table[:, layer], 512)
            xs = _outproj_ln(xs, oa, of, ob, proj_hg, hg_norm_g[e][None], gate, lng, lnb,
                             mix_w_out[e].astype(BF16), 256)
        else:
            o = layer // 2
            ops = _s5_operators(s5_lambda_re[o], s5_lambda_im[o], s5_log_dt[o], s5_b_re[o], s5_b_im[o],
                                s5_c_re[o], s5_c_im[o])
            y = _s5_mixer_core(xs, scale, shift, s5_d[o][None], ops, 512)
            xs = _glu_ln(xs, y, gate, lng, lnb, s5_w_glu[o].astype(BF16), 512, 1024)
        shift, scale, gate = mods(2 * layer + 1)
        xs = _mlp_ln(xs, scale, shift, gate, ln_g[layer, 1][None], ln_b[layer, 1][None],
                     mlp_w1[layer].astype(BF16), mlp_w2[layer].astype(BF16), 512, 1024)
    return xs.reshape(B, L, D)
```

```python
import functools
import math

import numpy as np
import jax
import jax.numpy as jnp
from jax import lax
from jax.experimental import pallas as pl
from jax.experimental.pallas import tpu as pltpu

F32 = jnp.float32
BF16 = jnp.bfloat16

D_MODEL = 2048
DEPTH = 2
LANES = 128
SUBLANES = 8

DIFF_HEADS = 8
DIFF_QK_DIM = 64
DIFF_WIDTH = 1024
ROT_DIM = DIFF_QK_DIM // 4
ROPE_THETA = 500000.0
ATTN_ROWS = 128

HG_HEADS = 8
HG_WIDTH = 1024
HG_TILE = 128
HG_BAND = SUBLANES
HG_LEVELS = (64, 32, 16, 8)

S5_GROUP = 16
S5_GROUPS = D_MODEL // S5_GROUP
S5_STATE = 64
S5_T = 16
S5_GPB = LANES // S5_GROUP
S5_BLOCKS = D_MODEL // LANES
S5_ROW = S5_T * LANES
S5_SW = S5_GPB * S5_STATE * 2

D_FF = 4 * D_MODEL
ALPHA = (2 * DEPTH) ** 0.25
LN_EPS = 1e-5
RMS_EPS = 1e-6
HI = lax.Precision.HIGHEST

VMEM_LIMIT = 56 * 1024 * 1024


def _cp(sem):
    return pltpu.CompilerParams(dimension_semantics=sem, vmem_limit_bytes=VMEM_LIMIT)


def _layer_norm_rows(z, g, b):
    mu = jnp.mean(z, axis=-1, keepdims=True)
    zc = z - mu
    var = jnp.mean(zc * zc, axis=-1, keepdims=True)
    return zc * lax.rsqrt(var + LN_EPS) * g + b


def _adaln_kernel(c_ref, w_ref, b_ref, o_ref):
    c = c_ref[...]
    ca = c * jax.nn.sigmoid(c)
    w = w_ref[0]
    reps = w.shape[1] // LANES
    cb = jnp.concatenate([ca] * reps, axis=1)
    o_ref[0] = jnp.sum(w * cb, axis=0, keepdims=True) + b_ref[0]


def _adaln(c, ada_w, ada_b):
    n = 3 * D_MODEL
    tn = 768
    w = ada_w.reshape(2 * DEPTH, D_MODEL, n)
    b = ada_b.reshape(2 * DEPTH, 1, n)
    c_rep = jnp.broadcast_to(c.reshape(D_MODEL, 1), (D_MODEL, LANES))
    return pl.pallas_call(
        _adaln_kernel,
        grid=(2 * DEPTH, n // tn),
        in_specs=[pl.BlockSpec((D_MODEL, LANES), lambda a, j: (0, 0)),
                  pl.BlockSpec((1, D_MODEL, tn), lambda a, j: (a, 0, j)),
                  pl.BlockSpec((1, 1, tn), lambda a, j: (a, 0, j))],
        out_specs=pl.BlockSpec((1, 1, tn), lambda a, j: (a, 0, j)),
        out_shape=jax.ShapeDtypeStruct((2 * DEPTH, 1, n), F32),
        compiler_params=_cp(("parallel", "parallel")),
        name="adaln",
    )(c_rep, w, b)


def _inproj_kernel(x_ref, sc_ref, sh_ref, w_ref, cos_ref, sa_ref, sb_ref, o_ref, h_scr, *, n_rot):
    j = pl.program_id(1)

    @pl.when(j == 0)
    def _():
        h_scr[...] = (x_ref[...] * (1.0 + sc_ref[...]) + sh_ref[...]).astype(BF16)

    y = jnp.dot(h_scr[...], w_ref[...], preferred_element_type=F32)

    @pl.when(j < n_rot)
    def _():
        cs, sa, sb = cos_ref[...], sa_ref[...], sb_ref[...]
        for g in range(y.shape[1] // LANES):
            yg = y[:, g * LANES:(g + 1) * LANES]
            rot = (yg * cs + pltpu.roll(yg, ROT_DIM // 2, 1) * sa
                   + pltpu.roll(yg, LANES - ROT_DIM // 2, 1) * sb)
            o_ref[:, g * LANES:(g + 1) * LANES] = rot.astype(o_ref.dtype)

    @pl.when(j >= n_rot)
    def _():
        o_ref[...] = y.astype(o_ref.dtype)


def _inproj(x, scale, shift, w, rot_tabs, n_rot, out_dtype, tm, tn):
    L = x.shape[0]
    n = w.shape[1]
    tm = min(tm, L)
    return pl.pallas_call(
        functools.partial(_inproj_kernel, n_rot=n_rot),
        grid=(L // tm, n // tn),
        in_specs=[pl.BlockSpec((tm, D_MODEL), lambda i, j: (i, 0)),
                  pl.BlockSpec((1, D_MODEL), lambda i, j: (0, 0)),
                  pl.BlockSpec((1, D_MODEL), lambda i, j: (0, 0)),
                  pl.BlockSpec((D_MODEL, tn), lambda i, j: (0, j)),
                  pl.BlockSpec((tm, LANES), lambda i, j: (i, 0)),
                  pl.BlockSpec((tm, LANES), lambda i, j: (i, 0)),
                  pl.BlockSpec((tm, LANES), lambda i, j: (i, 0))],
        out_specs=pl.BlockSpec((tm, tn), lambda i, j: (i, j)),
        out_shape=jax.ShapeDtypeStruct((L, n), out_dtype),
        scratch_shapes=[pltpu.VMEM((tm, D_MODEL), BF16)],
        compiler_params=_cp(("parallel", "arbitrary")),
        name="inproj",
    )(x, scale, shift, w, *rot_tabs)


def _rotary_tables(L):
    half = ROT_DIM // 2
    pos = jnp.arange(L, dtype=F32)
    inv_freq = 1.0 / (ROPE_THETA ** (jnp.arange(0, ROT_DIM, 2, dtype=F32) / ROT_DIM))
    ang = pos[:, None] * inv_freq[None, :]
    cos, sin = jnp.cos(ang), jnp.sin(ang)
    ones = jnp.ones((L, DIFF_QK_DIM - ROT_DIM), F32)
    zeros = jnp.zeros((L, half), F32)
    zrest = jnp.zeros((L, DIFF_QK_DIM - ROT_DIM), F32)
    c_comp = jnp.concatenate([cos, cos, ones], axis=1)
    sa_comp = jnp.concatenate([zeros, sin, zrest], axis=1)
    sb_comp = jnp.concatenate([-sin, zeros, zrest], axis=1)
    two = lambda t: jnp.concatenate([t, t], axis=1)
    return two(c_comp), two(sa_comp), two(sb_comp)


def _attn_kernel(lam_ref, q_ref, k_ref, v_ref, g_ref, o_ref, q2_scr, m_scr, acc_scr, s_scr,
                 *, tq, tk, out_scale):
    n_rows = 2 * tq
    n_chunks = n_rows // ATTN_ROWS
    n_kv = k_ref.shape[0] // tk
    q = q_ref[...]
    lane = lax.broadcasted_iota(jnp.int32, q.shape, 1)
    qs = q * (DIFF_QK_DIM ** -0.5)
    q2_scr[0:tq, :] = jnp.where(lane < DIFF_QK_DIM, qs, 0).astype(BF16)
    q2_scr[tq:n_rows, :] = jnp.where(lane >= DIFF_QK_DIM, qs, 0).astype(BF16)
    m_scr[...] = jnp.full(m_scr.shape, -jnp.inf, F32)
    acc_scr[...] = jnp.zeros(acc_scr.shape, F32)
    ones = jnp.ones((tk, LANES), BF16)
    n_col = tk // LANES

    def scores(r, kc):
        rs = slice(r * ATTN_ROWS, (r + 1) * ATTN_ROWS)
        return lax.dot_general(q2_scr[rs, :], kc, (((1,), (1,)), ((), ())), preferred_element_type=F32)

    kc0 = k_ref[0:tk, :]
    for r in range(n_chunks):
        s_scr[r * ATTN_ROWS:(r + 1) * ATTN_ROWS, :] = scores(r, kc0)

    def body(j, carry):
        c0 = pl.multiple_of(j * tk, tk)
        c1 = pl.multiple_of(jnp.minimum(j + 1, n_kv - 1) * tk, tk)
        k_next = k_ref[pl.ds(c1, tk), :]
        v1 = jnp.concatenate([v_ref[pl.ds(c0, tk), :], ones], axis=1)
        for r in range(n_chunks):
            rs = slice(r * ATTN_ROWS, (r + 1) * ATTN_ROWS)
            s = s_scr[rs, :]
            s_scr[rs, :] = scores(r, k_next)
            mx = s[:, 0:LANES]
            for c in range(1, n_col):
                mx = jnp.maximum(mx, s[:, c * LANES:(c + 1) * LANES])
            m_prev = m_scr[rs, :]
            m_new = jnp.maximum(m_prev, jnp.max(mx, axis=1, keepdims=True))
            alpha = jnp.exp(m_prev - m_new)
            p = jnp.exp(s - jnp.concatenate([m_new] * n_col, axis=1)).astype(BF16)
            pv = jnp.dot(p, v1, preferred_element_type=F32)
            acc_scr[rs, :] = jnp.concatenate([alpha, alpha], axis=1) * acc_scr[rs, :] + pv
            m_scr[rs, :] = m_new
        return carry

    lax.fori_loop(0, n_kv, body, 0)

    acc = acc_scr[...]
    o2 = acc[:, 0:LANES] / acc[:, LANES:2 * LANES]
    o = o2[0:tq, :] - lam_ref[0] * o2[tq:n_rows, :]
    ms = jnp.mean(o * o, axis=-1, keepdims=True)
    o_ref[...] = (o * lax.rsqrt(ms + RMS_EPS) * g_ref[...] * out_scale).astype(o_ref.dtype)


def _diff_attention(qkv, lam, subln_g, lam_init, tq, tk):
    L = qkv.shape[0]
    tq, tk = min(tq, L), min(tk, L)
    H = DIFF_HEADS
    return pl.pallas_call(
        functools.partial(_attn_kernel, tq=tq, tk=tk, out_scale=1.0 - lam_init),
        grid=(H, L // tq),
        in_specs=[pl.BlockSpec(memory_space=pltpu.SMEM),
                  pl.BlockSpec((tq, LANES), lambda h, i: (i, h)),
                  pl.BlockSpec((L, LANES), lambda h, i: (0, H + h)),
                  pl.BlockSpec((L, LANES), lambda h, i: (0, 2 * H + h)),
                  pl.BlockSpec((1, LANES), lambda h, i: (0, 0))],
        out_specs=pl.BlockSpec((tq, LANES), lambda h, i: (i, h)),
        out_shape=jax.ShapeDtypeStruct((L, DIFF_WIDTH), BF16),
        scratch_shapes=[pltpu.VMEM((2 * tq, LANES), BF16),
                        pltpu.VMEM((2 * tq, LANES), F32),
                        pltpu.VMEM((2 * tq, 2 * LANES), F32),
                        pltpu.VMEM((2 * tq, tk), F32)],
        compiler_params=_cp(("parallel", "parallel")),
        name="diff_attn",
    )(lam, qkv, qkv, qkv, subln_g)


def _hg_masks():
    C = HG_TILE
    t = np.arange(C)[:, None]
    s = np.arange(C)[None, :]
    out = []
    for reverse in (False, True):
        for h in HG_LEVELS:
            same = (t // (2 * h)) == (s // (2 * h))
            t_late = (t % (2 * h)) >= h
            s_late = (s % (2 * h)) >= h
            m = same & (~t_late & s_late if reverse else t_late & ~s_late)
            out.append(m)
        for d in range(HG_BAND):
            same = (t // HG_BAND) == (s // HG_BAND)
            out.append(same & ((s == t + d) if reverse else (s == t - d)))
    return jnp.asarray(np.stack(out).astype(np.float32))


def _hg_tri():
    C = HG_TILE
    t = np.arange(C)[:, None]
    r = np.arange(C)[None, :]
    return jnp.asarray(np.stack([(r <= t), (r >= t)]).astype(np.float32))


def _hg_tile(qr, fr, v, lb, st, tri, mask_ref, reverse):
    C = HG_TILE
    n_lvl = len(HG_LEVELS)
    m0 = (n_lvl + HG_BAND) if reverse else 0
    q = qr * jax.nn.sigmoid(qr)
    f = lb + (1.0 - lb) * jax.nn.sigmoid(fr)
    k = 1.0 - f
    g = jnp.log(f)
    b = jnp.dot(tri, g, precision=HI, preferred_element_type=F32)
    b_edge = b[0:1, :] if reverse else b[C - 1:C, :]
    vb = v.astype(BF16)

    qc = (q * jnp.exp(b)).astype(BF16)
    kc = (k * jnp.exp(b_edge - b)).astype(BF16)
    o = lax.dot_general(qc, st.astype(BF16), (((1,), (1,)), ((), ())), preferred_element_type=F32)
    st_new = st * jnp.exp(b_edge) + jnp.dot(v.T.astype(BF16), kc, preferred_element_type=F32)

    a = jnp.zeros((C, C), F32)
    for li, h in enumerate(HG_LEVELS):
        rows = []
        for blk in range(C // (2 * h)):
            r = blk * 2 * h + (h if reverse else h - 1)
            rows.append(jnp.broadcast_to(b[r:r + 1, :], (2 * h, LANES)))
        bref = rows[0] if len(rows) == 1 else jnp.concatenate(rows, axis=0)
        e = jnp.exp(-jnp.abs(b - bref))
        ah = lax.dot_general((q * e).astype(BF16), (k * e).astype(BF16), (((1,), (1,)), ((), ())),
                             preferred_element_type=F32)
        a = a + ah * mask_ref[m0 + li]
    def roll_rows(x, shift):
        x3 = pltpu.roll(x.reshape(C // HG_BAND, HG_BAND, LANES), shift % HG_BAND, 1)
        return x3.reshape(C, LANES)
    ed = None
    for d in range(HG_BAND):
        if d == 0:
            p = q * k
        else:
            sh = (C - d) if reverse else d
            shf = (C - (d - 1)) % C if reverse else d - 1
            fsh = f if shf == 0 else roll_rows(f, shf)
            ed = fsh if ed is None else ed * fsh
            p = q * roll_rows(k, sh) * ed
        a = a + jnp.sum(p, axis=1, keepdims=True) * mask_ref[m0 + n_lvl + d]
    o = o + jnp.dot(a.astype(BF16), vb, preferred_element_type=F32)
    return o, st_new


def _hgrn_kernel(qf_ref, qb_ref, ff_ref, fb_ref, vf_ref, vb_ref, lb_ref, tri_ref, mask_ref,
                 of_ref, ob_ref, stf_scr, stb_scr, *, n_tiles):
    @pl.when(pl.program_id(1) == 0)
    def _():
        stf_scr[...] = jnp.zeros(stf_scr.shape, F32)
        stb_scr[...] = jnp.zeros(stb_scr.shape, F32)

    C = HG_TILE
    lbf = lb_ref[0:1, :]
    lbb = lb_ref[1:2, :]

    def body(t, carry):
        rf = pl.multiple_of(t * C, C)
        rb = pl.multiple_of((n_tiles - 1 - t) * C, C)
        o_f, st_f = _hg_tile(qf_ref[pl.ds(rf, C), :], ff_ref[pl.ds(rf, C), :], vf_ref[pl.ds(rf, C), :],
                             lbf, stf_scr[...], tri_ref[0], mask_ref, False)
        of_ref[pl.ds(rf, C), :] = o_f
        stf_scr[...] = st_f
        o_b, st_b = _hg_tile(qb_ref[pl.ds(rb, C), :], fb_ref[pl.ds(rb, C), :], vb_ref[pl.ds(rb, C), :],
                             lbb, stb_scr[...], tri_ref[1], mask_ref, True)
        ob_ref[pl.ds(rb, C), :] = o_b
        stb_scr[...] = st_b
        return carry

    lax.fori_loop(0, n_tiles, body, 0, unroll=True)


def _hgrn2(proj, lb, rows):
    L = proj.shape[0]
    rows = min(rows, L)
    nb = L // rows
    H = HG_HEADS
    nm = len(HG_LEVELS) + HG_BAND
    blk = lambda col0, rev: pl.BlockSpec(
        (rows, LANES), (lambda h, i: (nb - 1 - i, col0 + h)) if rev else (lambda h, i: (i, col0 + h)))
    return pl.pallas_call(
        functools.partial(_hgrn_kernel, n_tiles=rows // HG_TILE),
        grid=(H, nb),
        in_specs=[blk(0, False), blk(0, True), blk(H, False), blk(2 * H, True),
                  blk(3 * H, False), blk(3 * H, True),
                  pl.BlockSpec((2, LANES), lambda h, i: (0, h)),
                  pl.BlockSpec((2, HG_TILE, HG_TILE), lambda h, i: (0, 0, 0)),
                  pl.BlockSpec((2 * nm, HG_TILE, HG_TILE), lambda h, i: (0, 0, 0))],
        out_specs=[blk(0, False), blk(0, True)],
        out_shape=[jax.ShapeDtypeStruct((L, HG_WIDTH), F32)] * 2,
        scratch_shapes=[pltpu.VMEM((LANES, LANES), F32)] * 2,
        compiler_params=_cp(("parallel", "arbitrary")),
        name="hgrn2",
    )(proj, proj, proj, proj, proj, proj, lb, _hg_tri(), _hg_masks())


def _outproj_kernel(x_ref, oa_ref, of_ref, ob_ref, gr_ref, ng_ref, gate_ref, lng_ref, lnb_ref, w_ref,
                    o_ref, lhs_scr):
    lhs_scr[:, 0:DIFF_WIDTH] = oa_ref[...]
    ng = ng_ref[...]
    for h in range(HG_HEADS):
        sl = slice(h * LANES, (h + 1) * LANES)
        o = of_ref[:, sl] + ob_ref[:, sl]
        ms = jnp.mean(o * o, axis=-1, keepdims=True)
        gr = gr_ref[:, sl]
        hg = o * lax.rsqrt(ms + RMS_EPS) * ng * (gr * jax.nn.sigmoid(gr))
        lhs_scr[:, DIFF_WIDTH + h * LANES:DIFF_WIDTH + (h + 1) * LANES] = hg.astype(BF16)
    y = jnp.dot(lhs_scr[...], w_ref[...], preferred_element_type=F32)
    z = ALPHA * x_ref[...] + (1.0 + gate_ref[...]) * y
    o_ref[...] = _layer_norm_rows(z, lng_ref[...], lnb_ref[...])


def _outproj_ln(x, oa, of, ob, proj_hg, norm_g, gate, ln_g, ln_b, w, tm):
    L = x.shape[0]
    tm = min(tm, L)
    row = lambda i: (i, 0)
    vec = pl.BlockSpec((1, D_MODEL), lambda i: (0, 0))
    return pl.pallas_call(
        _outproj_kernel,
        grid=(L // tm,),
        in_specs=[pl.BlockSpec((tm, D_MODEL), row),
                  pl.BlockSpec((tm, DIFF_WIDTH), row),
                  pl.BlockSpec((tm, HG_WIDTH), row),
                  pl.BlockSpec((tm, HG_WIDTH), row),
                  pl.BlockSpec((tm, HG_WIDTH), lambda i: (i, 4)),
                  pl.BlockSpec((1, LANES), lambda i: (0, 0)),
                  vec, vec, vec,
                  pl.BlockSpec((D_MODEL, D_MODEL), lambda i: (0, 0))],
        out_specs=pl.BlockSpec((tm, D_MODEL), row),
        out_shape=jax.ShapeDtypeStruct((L, D_MODEL), F32),
        scratch_shapes=[pltpu.VMEM((tm, D_MODEL), BF16)],
        compiler_params=_cp(("parallel",)),
        name="outproj_ln",
    )(x, oa, of, ob, proj_hg, norm_g, gate, ln_g, ln_b, w)


def _mlp_kernel(x_ref, sc_ref, sh_ref, gate_ref, lng_ref, lnb_ref, w1_ref, w2_ref, o_ref, h_scr, acc_scr):
    j = pl.program_id(1)

    @pl.when(j == 0)
    def _():
        h_scr[...] = (x_ref[...] * (1.0 + sc_ref[...]) + sh_ref[...]).astype(BF16)
        acc_scr[...] = jnp.zeros(acc_scr.shape, F32)

    a = jnp.maximum(jnp.dot(h_scr[...], w1_ref[...], preferred_element_type=F32), 0.0)
    acc_scr[...] += jnp.dot((a * a).astype(BF16), w2_ref[...], preferred_element_type=F32)

    @pl.when(j == pl.num_programs(1) - 1)
    def _():
        z = ALPHA * x_ref[...] + (1.0 + gate_ref[...]) * acc_scr[...]
        o_ref[...] = _layer_norm_rows(z, lng_ref[...], lnb_ref[...])


def _mlp_ln(x, scale, shift, gate, ln_g, ln_b, w1, w2, tm, tf):
    L = x.shape[0]
    tm = min(tm, L)
    vec = pl.BlockSpec((1, D_MODEL), lambda i, j: (0, 0))
    return pl.pallas_call(
        _mlp_kernel,
        grid=(L // tm, D_FF // tf),
        in_specs=[pl.BlockSpec((tm, D_MODEL), lambda i, j: (i, 0)),
                  vec, vec, vec, vec, vec,
                  pl.BlockSpec((D_MODEL, tf), lambda i, j: (0, j)),
                  pl.BlockSpec((tf, D_MODEL), lambda i, j: (j, 0))],
        out_specs=pl.BlockSpec((tm, D_MODEL), lambda i, j: (i, 0)),
        out_shape=jax.ShapeDtypeStruct((L, D_MODEL), F32),
        scratch_shapes=[pltpu.VMEM((tm, D_MODEL), BF16), pltpu.VMEM((tm, D_MODEL), F32)],
        compiler_params=_cp(("parallel", "arbitrary")),
        name="mlp_ln",
    )(x, scale, shift, gate, ln_g, ln_b, w1, w2)


def _s5_operators(lam_re, lam_im, log_dt, b_re, b_im, c_re, c_im):
    T, G, P, Hc, NB, GPB = S5_T, S5_GROUPS, S5_STATE, S5_GROUP, S5_BLOCKS, S5_GPB
    dt = jnp.exp(log_dt)[..., None]
    zr, zi = lam_re * dt, lam_im * dt
    mag = jnp.exp(zr)
    ab_re, ab_im = mag * jnp.cos(zi), mag * jnp.sin(zi)
    den = lam_re * lam_re + lam_im * lam_im
    coef_re = ((ab_re - 1.0) * lam_re + ab_im * lam_im) / den
    coef_im = (ab_im * lam_re - (ab_re - 1.0) * lam_im) / den
    bb_re = coef_re[..., None] * b_re - coef_im[..., None] * b_im
    bb_im = coef_re[..., None] * b_im + coef_im[..., None] * b_re

    def apow(k):
        k = jnp.asarray(k, F32)[:, None, None, None]
        m = jnp.exp(zr[None] * k)
        return m * jnp.cos(zi[None] * k), m * jnp.sin(zi[None] * k)

    pr, pi = apow(np.arange(T))
    ca_re = c_re[None] * pr[:, :, :, None, :] - c_im[None] * pi[:, :, :, None, :]
    ca_im = c_re[None] * pi[:, :, :, None, :] + c_im[None] * pr[:, :, :, None, :]
    bt_re, bt_im = jnp.swapaxes(bb_re, 2, 3), jnp.swapaxes(bb_im, 2, 3)
    lagk = jnp.sum(ca_re[:, :, :, :, None, :] * bt_re[None, :, :, None, :, :]
                   - ca_im[:, :, :, :, None, :] * bt_im[None, :, :, None, :, :], axis=-1)
    kf, kb = lagk[:, 0], lagk[:, 1]
    kall = jnp.concatenate([kb[:0:-1], (kf[0] + kb[0])[None], kf[1:]], axis=0)
    kc = jnp.transpose(kall.reshape(2 * T - 1, NB, GPB, Hc, Hc), (1, 0, 4, 2, 3))
    kc = kc.reshape(NB, 2 * T - 1, Hc, LANES)

    def gp_lanes(v):
        return jnp.transpose(v.reshape(2, NB, GPB, Hc, P), (1, 0, 3, 2, 4)).reshape(NB, 2, Hc, GPB * P)
    bt = jnp.stack([gp_lanes(bt_re), gp_lanes(bt_im)], axis=1)
    def e_tab(v):
        v = jnp.stack([v[::-1, 0], v[:, 1]], axis=1).reshape(T, 2, NB, GPB * P)
        return jnp.transpose(v, (2, 0, 1, 3))
    et = jnp.stack([e_tab(pr), e_tab(pi)], axis=2)[:, :, :, :, None, :]

    def gc_lanes(v):
        return jnp.transpose(v.reshape(2, NB, GPB, Hc, P), (1, 0, 4, 2, 3)).reshape(NB, 2, P, LANES)
    cb = jnp.stack([gc_lanes(c_re), gc_lanes(c_im)], axis=1)
    cr, ci = apow(np.arange(1, T + 1))
    def p_tab(v):
        v = jnp.stack([v[:, 0], v[::-1, 1]], axis=1).reshape(T, 2, NB, GPB * P // LANES, LANES)
        return jnp.transpose(v, (2, 1, 3, 4, 0))
    p_re, p_im = p_tab(cr), p_tab(ci)
    pt = jnp.stack([jnp.stack([p_re, p_im], axis=3), jnp.stack([-p_im, p_re], axis=3)], axis=3)
    return tuple(_s5_expand(kc, bt, et, cb, pt)) + ((zr, zi),)


def _s5_expand_kernel(kc_ref, bt_ref, et_ref, cb_ref, pt_ref, wt_ref, ws_ref, wc_ref):
    s = pl.program_id(1)
    n_q = S5_GPB * S5_STATE // LANES
    gpq = S5_GPB // n_q
    row = lax.broadcasted_iota(jnp.int32, (LANES, LANES), 0)
    col = lax.broadcasted_iota(jnp.int32, (LANES, LANES), 1)
    over_groups = lambda a: jnp.concatenate([a] * S5_GPB, axis=0)

    same_group = (row // S5_GROUP) == (col // S5_GROUP)
    for t in range(S5_T):
        k = over_groups(kc_ref[0, t - s + S5_T - 1])
        wt_ref[0, :, t * LANES:(t + 1) * LANES] = jnp.where(same_group, k, 0.0).astype(BF16)

    for d in range(2):
        for q in range(n_q):
            sl = slice(q * LANES, (q + 1) * LANES)
            b_re, b_im = bt_ref[0, 0, d, :, sl], bt_ref[0, 1, d, :, sl]
            e_re, e_im = et_ref[0, 0, 0, d, :, sl], et_ref[0, 0, 1, d, :, sl]
            own = (row // S5_GROUP) == (gpq * q + col // S5_STATE)
            for ri, val in ((0, e_re * b_re - e_im * b_im), (1, e_re * b_im + e_im * b_re)):
                c0 = ((d * n_q + q) * 2 + ri) * LANES
                ws_ref[0, :, c0:c0 + LANES] = jnp.where(own, over_groups(val), 0.0).astype(BF16)

    q = (s // 2) % n_q
    own = (gpq * q + row // S5_STATE) == (col // S5_GROUP)
    c_re = jnp.concatenate([cb_ref[0, 0, 0]] * gpq, axis=0)
    c_im = jnp.concatenate([cb_ref[0, 1, 0]] * gpq, axis=0)
    for t in range(S5_T):
        x = pt_ref[0, 0, 0, 0, 0, :, t:t + 1]
        y = pt_ref[0, 0, 0, 0, 1, :, t:t + 1]
        wc_ref[0, :, t * LANES:(t + 1) * LANES] = jnp.where(own, c_re * x - c_im * y, 0.0).astype(BF16)


def _s5_expand(kc, bt, et, cb, pt):
    NB, T, Hc, P = S5_BLOCKS, S5_T, S5_GROUP, S5_STATE
    n_q = S5_GPB * P // LANES
    assert 2 * S5_SW // LANES == T and 2 * 2 * n_q == T
    slab = pl.BlockSpec((1, LANES, S5_ROW), lambda b, s: (b, s, 0))
    return pl.pallas_call(
        _s5_expand_kernel,
        grid=(NB, T),
        in_specs=[pl.BlockSpec((1, 2 * T - 1, Hc, LANES), lambda b, s: (b, 0, 0, 0)),
                  pl.BlockSpec((1, 2, 2, Hc, S5_GPB * P), lambda b, s: (b, 0, 0, 0, 0)),
                  pl.BlockSpec((1, 1, 2, 2, 1, S5_GPB * P), lambda b, s: (b, s, 0, 0, 0, 0)),
                  pl.BlockSpec((1, 2, 1, P, LANES), lambda b, s: (b, 0, s // 8, 0, 0)),
                  pl.BlockSpec((1, 1, 1, 1, 2, LANES, T),
                               lambda b, s: (b, s // 8, (s // 2) % n_q, s % 2, 0, 0, 0))],
        out_specs=[slab, slab, slab],
        out_shape=[jax.ShapeDtypeStruct((NB, S5_ROW, S5_ROW), BF16)] * 3,
        compiler_params=_cp(("parallel", "arbitrary")),
        name="s5_expand",
    )(kc, bt, et, cb, pt)


def _s5_scan_powers(zr, zi, n_steps):
    k = (S5_T * (2.0 ** np.arange(n_steps))).astype(np.float32)[:, None, None, None]
    m = jnp.exp(zr[None] * k)
    re, im = m * jnp.cos(zi[None] * k), m * jnp.sin(zi[None] * k)
    x = jnp.stack([re, im], axis=0)
    x = x.reshape(2, n_steps, 2, S5_BLOCKS, S5_GPB * S5_STATE // LANES, LANES)
    x = jnp.transpose(x, (2, 1, 3, 4, 0, 5))
    return x.reshape(2, n_steps, S5_BLOCKS * S5_SW)


def _s5_fold(x_ref, sc, sh, tc):
    parts = [x_ref[pl.ds(t, tc, stride=S5_T), :] * (1.0 + sc) + sh for t in range(S5_T)]
    return parts


def _s5_state_kernel(x_ref, sc_ref, sh_ref, w_ref, zf_ref, zb_ref, *, tc):
    parts = _s5_fold(x_ref, sc_ref[...], sh_ref[...], tc)
    u = jnp.concatenate([p.astype(BF16) for p in parts], axis=1)
    z = jnp.dot(u, w_ref[0], preferred_element_type=F32)
    zf_ref[...] = z[:, :S5_SW]
    zb_ref[...] = z[:, S5_SW:]


def _s5_scan_kernel(zf_ref, zb_ref, af_ref, ab_ref, sf_ref, sb_ref, *, n_rows, n_steps):
    row = lax.broadcasted_iota(jnp.int32, (n_rows, LANES), 0)

    def scan(z_ref, a_ref, reverse):
        re, im = z_ref[:, 0:LANES], z_ref[:, LANES:2 * LANES]
        def shifted(v, sh):
            if reverse:
                return jnp.where(row < n_rows - sh, pltpu.roll(v, n_rows - sh, 0), 0.0)
            return jnp.where(row >= sh, pltpu.roll(v, sh, 0), 0.0)
        for k in range(n_steps):
            sh = 2 ** k
            ar, ai = a_ref[0, k:k + 1, 0:LANES], a_ref[0, k:k + 1, LANES:2 * LANES]
            rs, is_ = shifted(re, sh), shifted(im, sh)
            re, im = re + ar * rs - ai * is_, im + ar * is_ + ai * rs
        return shifted(re, 1), shifted(im, 1)

    re, im = scan(zf_ref, af_ref, False)
    sf_ref[:, 0:LANES] = re.astype(BF16)
    sf_ref[:, LANES:2 * LANES] = im.astype(BF16)
    re, im = scan(zb_ref, ab_ref, True)
    sb_ref[:, 0:LANES] = re.astype(BF16)
    sb_ref[:, LANES:2 * LANES] = im.astype(BF16)


def _s5_out_kernel(x_ref, sc_ref, sh_ref, d_ref, sf_ref, sb_ref, wt_ref, wc_ref, o_ref, *, tc, n_half):
    j = pl.program_id(2)
    sc, sh, d = sc_ref[...], sh_ref[...], d_ref[...]
    parts = _s5_fold(x_ref, sc, sh, tc)
    u = jnp.concatenate([p.astype(BF16) for p in parts], axis=1)
    s = jnp.concatenate([sf_ref[...], sb_ref[...]], axis=1)
    y = (jnp.dot(u, wt_ref[0], preferred_element_type=F32)
         + jnp.dot(s, wc_ref[0], preferred_element_type=F32))
    tph = S5_T // n_half
    for jj in range(n_half):
        @pl.when(j == jj)
        def _():
            for tl in range(tph):
                t = jj * tph + tl
                o_ref[pl.ds(t, tc, stride=S5_T), :] = y[:, tl * LANES:(tl + 1) * LANES] + d * parts[t]


def _s5_mixer_core(x, scale, shift, d_skip, ops, tc):
    w_toep, w_state, w_carry, (zr, zi) = ops
    L = x.shape[0]
    n_rows = L // S5_T
    NB = S5_BLOCKS
    n_steps = int(math.log2(n_rows))
    assert 2 ** n_steps == n_rows
    tc = min(tc, n_rows)
    vecb = pl.BlockSpec((1, LANES), lambda b, i: (0, b))

    zf, zb = pl.pallas_call(
        functools.partial(_s5_state_kernel, tc=tc),
        grid=(NB, n_rows // tc),
        in_specs=[pl.BlockSpec((tc * S5_T, LANES), lambda b, i: (i, b)), vecb, vecb,
                  pl.BlockSpec((1, S5_ROW, 2 * S5_SW), lambda b, i: (b, 0, 0))],
        out_specs=[pl.BlockSpec((tc, S5_SW), lambda b, i: (i, b))] * 2,
        out_shape=[jax.ShapeDtypeStruct((n_rows, NB * S5_SW), F32)] * 2,
        compiler_params=_cp(("parallel", "parallel")),
        name="s5_state",
    )(x, scale, shift, w_state)

    pw = _s5_scan_powers(zr, zi, n_steps)
    cw = 2 * LANES
    sf, sb = pl.pallas_call(
        functools.partial(_s5_scan_kernel, n_rows=n_rows, n_steps=n_steps),
        grid=(NB * S5_SW // cw,),
        in_specs=[pl.BlockSpec((n_rows, cw), lambda n: (0, n)),
                  pl.BlockSpec((n_rows, cw), lambda n: (0, n)),
                  pl.BlockSpec((1, n_steps, cw), lambda n: (0, 0, n)),
                  pl.BlockSpec((1, n_steps, cw), lambda n: (1, 0, n))],
        out_specs=[pl.BlockSpec((n_rows, cw), lambda n: (0, n))] * 2,
        out_shape=[jax.ShapeDtypeStruct((n_rows, NB * S5_SW), BF16)] * 2,
        compiler_params=_cp(("parallel",)),
        name="s5_scan",
    )(zf, zb, pw, pw)

    n_half = 2
    nh = S5_ROW // n_half
    vec3 = pl.BlockSpec((1, LANES), lambda b, i, j: (0, b))
    return pl.pallas_call(
        functools.partial(_s5_out_kernel, tc=tc, n_half=n_half),
        grid=(NB, n_rows // tc, n_half),
        in_specs=[pl.BlockSpec((tc * S5_T, LANES), lambda b, i, j: (i, b)), vec3, vec3, vec3,
                  pl.BlockSpec((tc, S5_SW), lambda b, i, j: (i, b)),
                  pl.BlockSpec((tc, S5_SW), lambda b, i, j: (i, b)),
                  pl.BlockSpec((1, S5_ROW, nh), lambda b, i, j: (b, 0, j)),
                  pl.BlockSpec((1, 2 * S5_SW, nh), lambda b, i, j: (b, 0, j))],
        out_specs=pl.BlockSpec((tc * S5_T, LANES), lambda b, i, j: (i, b)),
        out_shape=jax.ShapeDtypeStruct((L, D_MODEL), F32),
        compiler_params=_cp(("parallel", "parallel", "arbitrary")),
        name="s5_out",
    )(x, scale, shift, d_skip, sf, sb, w_toep, w_carry)


def _glu_kernel(x_ref, y_ref, gate_ref, lng_ref, lnb_ref, wa_ref, wg_ref, o_ref, lhs_scr, mix_scr, *, tn):
    j = pl.program_id(1)

    @pl.when(j == 0)
    def _():
        lhs_scr[...] = jax.nn.gelu(y_ref[...]).astype(BF16)

    a = jnp.dot(lhs_scr[...], wa_ref[...], preferred_element_type=F32)
    g = jnp.dot(lhs_scr[...], wg_ref[...], preferred_element_type=F32)
    mix = a * jax.nn.sigmoid(g)
    for jj in range(D_MODEL // tn):
        @pl.when(j == jj)
        def _():
            mix_scr[:, jj * tn:(jj + 1) * tn] = mix

    @pl.when(j == pl.num_programs(1) - 1)
    def _():
        z = ALPHA * x_ref[...] + (1.0 + gate_ref[...]) * mix_scr[...]
        o_ref[...] = _layer_norm_rows(z, lng_ref[...], lnb_ref[...])


def _glu_ln(x, y, gate, ln_g, ln_b, w_glu, tm, tn):
    L = x.shape[0]
    tm = min(tm, L)
    nj = D_MODEL // tn
    vec = pl.BlockSpec((1, D_MODEL), lambda i, j: (0, 0))
    return pl.pallas_call(
        functools.partial(_glu_kernel, tn=tn),
        grid=(L // tm, nj),
        in_specs=[pl.BlockSpec((tm, D_MODEL), lambda i, j: (i, 0)),
                  pl.BlockSpec((tm, D_MODEL), lambda i, j: (i, 0)),
                  vec, vec, vec,
                  pl.BlockSpec((D_MODEL, tn), lambda i, j: (0, j)),
                  pl.BlockSpec((D_MODEL, tn), lambda i, j: (0, nj + j))],
        out_specs=pl.BlockSpec((tm, D_MODEL), lambda i, j: (i, 0)),
        out_shape=jax.ShapeDtypeStruct((L, D_MODEL), F32),
        scratch_shapes=[pltpu.VMEM((tm, D_MODEL), BF16), pltpu.VMEM((tm, D_MODEL), F32)],
        compiler_params=_cp(("parallel", "arbitrary")),
        name="glu_ln",
    )(x, y, gate, ln_g, ln_b, w_glu, w_glu)


def kernel(x, c, ada_w, ada_b, ln_g, ln_b, mix_w_in, mix_w_out, diff_lambda, diff_subln_g, hg_lower_bound,
           hg_norm_g, s5_lambda_re, s5_lambda_im, s5_log_dt, s5_b_re, s5_b_im, s5_c_re, s5_c_im, s5_d,
           s5_w_glu, mlp_w1, mlp_w2):
    B, L, D = x.shape
    assert B == 1 and D == D_MODEL
    xs = x.reshape(L, D)
    mod = _adaln(c, ada_w, ada_b)
    mods = lambda a: (mod[a, :, 0:D], mod[a, :, D:2 * D], mod[a, :, 2 * D:3 * D])
    lb_table = jnp.cumsum(jax.nn.softmax(hg_lower_bound.astype(F32), axis=1), axis=1)
    rot_tabs = _rotary_tables(L)

    for layer in range(DEPTH):
        shift, scale, gate = mods(2 * layer)
        lng, lnb = ln_g[layer, 0][None], ln_b[layer, 0][None]
        if layer % 2 == 0:
            e = layer // 2
            lam_init = 0.8 - 0.6 * math.exp(-0.3 * layer)
            w_in = mix_w_in[e].astype(BF16)
            qkv = _inproj(xs, scale, shift, w_in[:, :3 * DIFF_WIDTH], rot_tabs, 2, BF16, 512, DIFF_WIDTH)
            proj_hg = _inproj(xs, scale, shift, w_in[:, 3 * DIFF_WIDTH:], rot_tabs, 0, F32, 512, HG_WIDTH)
            lf = diff_lambda[e].astype(F32)
            lam = jnp.exp(jnp.sum(lf[0] * lf[1])) - jnp.exp(jnp.sum(lf[2] * lf[3])) + lam_init
            oa = _diff_attention(qkv, lam.reshape(1), diff_subln_g[e][None], lam_init, 1024, 1024)
            of, ob = _hgrn2(proj_hg, lb_table[:, layer], 1024)
            xs = _outproj_ln(xs, oa, of, ob, proj_hg, hg_norm_g[e][None], gate, lng, lnb,
                             mix_w_out[e].astype(BF16), 256)
        else:
            o = layer // 2
            ops = _s5_operators(s5_lambda_re[o], s5_lambda_im[o], s5_log_dt[o], s5_b_re[o], s5_b_im[o],
                                s5_c_re[o], s5_c_im[o])
            y = _s5_mixer_core(xs, scale, shift, s5_d[o][None], ops, 512)
            xs = _glu_ln(xs, y, gate, lng, lnb, s5_w_glu[o].astype(BF16), 512, 1024)
        shift, scale, gate = mods(2 * layer + 1)
        xs = _mlp_ln(xs, scale, shift, gate, ln_g[layer, 1][None], ln_b[layer, 1][None],
                     mlp_w1[layer].astype(BF16), mlp_w2[layer].astype(BF16), 512, 1024)
    return xs.reshape(B, L, D)
```

```python
import functools
import math

import numpy as np
import jax
import jax.numpy as jnp
from jax import lax
from jax.experimental import pallas as pl
from jax.experimental.pallas import tpu as pltpu

F32 = jnp.float32
BF16 = jnp.bfloat16

D_MODEL = 2048
DEPTH = 2
LANES = 128
SUBLANES = 8

DIFF_HEADS = 8
DIFF_QK_DIM = 64
DIFF_WIDTH = 1024
ROT_DIM = DIFF_QK_DIM // 4
ROPE_THETA = 500000.0
ATTN_ROWS = 128

HG_HEADS = 8
HG_WIDTH = 1024
HG_TILE = 128
HG_BAND = SUBLANES
HG_LEVELS = (64, 32, 16, 8)

S5_GROUP = 16
S5_GROUPS = D_MODEL // S5_GROUP
S5_STATE = 64
S5_T = 16
S5_GPB = LANES // S5_GROUP
S5_BLOCKS = D_MODEL // LANES
S5_ROW = S5_T * LANES
S5_SW = S5_GPB * S5_STATE * 2

D_FF = 4 * D_MODEL
ALPHA = (2 * DEPTH) ** 0.25
LN_EPS = 1e-5
RMS_EPS = 1e-6
HI = lax.Precision.HIGHEST

VMEM_LIMIT = 56 * 1024 * 1024


def _cp(sem):
    return pltpu.CompilerParams(dimension_semantics=sem, vmem_limit_bytes=VMEM_LIMIT)


def _layer_norm_rows(z, g, b):
    mu = jnp.mean(z, axis=-1, keepdims=True)
    zc = z - mu
    var = jnp.mean(zc * zc, axis=-1, keepdims=True)
    return zc * lax.rsqrt(var + LN_EPS) * g + b


def _adaln_kernel(c_ref, w_ref, b_ref, o_ref):
    c = c_ref[...]
    ca = c * jax.nn.sigmoid(c)
    w = w_ref[0]
    reps = w.shape[1] // LANES
    cb = jnp.concatenate([ca] * reps, axis=1)
    o_ref[0] = jnp.sum(w * cb, axis=0, keepdims=True) + b_ref[0]


def _adaln(c, ada_w, ada_b):
    n = 3 * D_MODEL
    tn = 768
    w = ada_w.reshape(2 * DEPTH, D_MODEL, n)
    b = ada_b.reshape(2 * DEPTH, 1, n)
    c_rep = jnp.broadcast_to(c.reshape(D_MODEL, 1), (D_MODEL, LANES))
    return pl.pallas_call(
        _adaln_kernel,
        grid=(2 * DEPTH, n // tn),
        in_specs=[pl.BlockSpec((D_MODEL, LANES), lambda a, j: (0, 0)),
                  pl.BlockSpec((1, D_MODEL, tn), lambda a, j: (a, 0, j)),
                  pl.BlockSpec((1, 1, tn), lambda a, j: (a, 0, j))],
        out_specs=pl.BlockSpec((1, 1, tn), lambda a, j: (a, 0, j)),
        out_shape=jax.ShapeDtypeStruct((2 * DEPTH, 1, n), F32),
        compiler_params=_cp(("parallel", "parallel")),
        name="adaln",
    )(c_rep, w, b)


def _inproj_kernel(x_ref, sc_ref, sh_ref, w_ref, cos_ref, sa_ref, sb_ref, o_ref, h_scr, *, n_rot):
    j = pl.program_id(1)

    @pl.when(j == 0)
    def _():
        h_scr[...] = (x_ref[...] * (1.0 + sc_ref[...]) + sh_ref[...]).astype(BF16)

    y = jnp.dot(h_scr[...], w_ref[...], preferred_element_type=F32)

    @pl.when(j < n_rot)
    def _():
        cs, sa, sb = cos_ref[...], sa_ref[...], sb_ref[...]
        for g in range(y.shape[1] // LANES):
            yg = y[:, g * LANES:(g + 1) * LANES]
            rot = (yg * cs + pltpu.roll(yg, ROT_DIM // 2, 1) * sa
                   + pltpu.roll(yg, LANES - ROT_DIM // 2, 1) * sb)
            o_ref[:, g * LANES:(g + 1) * LANES] = rot.astype(o_ref.dtype)

    @pl.when(j >= n_rot)
    def _():
        o_ref[...] = y.astype(o_ref.dtype)


def _inproj(x, scale, shift, w, rot_tabs, n_rot, out_dtype, tm, tn):
    L = x.shape[0]
    n = w.shape[1]
    tm = min(tm, L)
    return pl.pallas_call(
        functools.partial(_inproj_kernel, n_rot=n_rot),
        grid=(L // tm, n // tn),
        in_specs=[pl.BlockSpec((tm, D_MODEL), lambda i, j: (i, 0)),
                  pl.BlockSpec((1, D_MODEL), lambda i, j: (0, 0)),
                  pl.BlockSpec((1, D_MODEL), lambda i, j: (0, 0)),
                  pl.BlockSpec((D_MODEL, tn), lambda i, j: (0, j)),
                  pl.BlockSpec((tm, LANES), lambda i, j: (i, 0)),
                  pl.BlockSpec((tm, LANES), lambda i, j: (i, 0)),
                  pl.BlockSpec((tm, LANES), lambda i, j: (i, 0))],
        out_specs=pl.BlockSpec((tm, tn), lambda i, j: (i, j)),
        out_shape=jax.ShapeDtypeStruct((L, n), out_dtype),
        scratch_shapes=[pltpu.VMEM((tm, D_MODEL), BF16)],
        compiler_params=_cp(("parallel", "arbitrary")),
        name="inproj",
    )(x, scale, shift, w, *rot_tabs)


def _rotary_tables(L):
    half = ROT_DIM // 2
    pos = jnp.arange(L, dtype=F32)
    inv_freq = 1.0 / (ROPE_THETA ** (jnp.arange(0, ROT_DIM, 2, dtype=F32) / ROT_DIM))
    ang = pos[:, None] * inv_freq[None, :]
    cos, sin = jnp.cos(ang), jnp.sin(ang)
    ones = jnp.ones((L, DIFF_QK_DIM - ROT_DIM), F32)
    zeros = jnp.zeros((L, half), F32)
    zrest = jnp.zeros((L, DIFF_QK_DIM - ROT_DIM), F32)
    c_comp = jnp.concatenate([cos, cos, ones], axis=1)
    sa_comp = jnp.concatenate([zeros, sin, zrest], axis=1)
    sb_comp = jnp.concatenate([-sin, zeros, zrest], axis=1)
    two = lambda t: jnp.concatenate([t, t], axis=1)
    return two(c_comp), two(sa_comp), two(sb_comp)


def _attn_kernel(lam_ref, q_ref, k_ref, v_ref, g_ref, o_ref, q2_scr, m_scr, acc_scr, s_scr,
                 *, tq, tk, out_scale):
    n_rows = 2 * tq
    n_chunks = n_rows // ATTN_ROWS
    n_kv = k_ref.shape[0] // tk
    q = q_ref[...]
    lane = lax.broadcasted_iota(jnp.int32, q.shape, 1)
    qs = q * (DIFF_QK_DIM ** -0.5)
    q2_scr[0:tq, :] = jnp.where(lane < DIFF_QK_DIM, qs, 0).astype(BF16)
    q2_scr[tq:n_rows, :] = jnp.where(lane >= DIFF_QK_DIM, qs, 0).astype(BF16)
    m_scr[...] = jnp.full(m_scr.shape, -jnp.inf, F32)
    acc_scr[...] = jnp.zeros(acc_scr.shape, F32)
    ones = jnp.ones((tk, LANES), BF16)
    n_col = tk // LANES

    def scores(r, kc):
        rs = slice(r * ATTN_ROWS, (r + 1) * ATTN_ROWS)
        return lax.dot_general(q2_scr[rs, :], kc, (((1,), (1,)), ((), ())), preferred_element_type=F32)

    kc0 = k_ref[0:tk, :]
    for r in range(n_chunks):
        s_scr[r * ATTN_ROWS:(r + 1) * ATTN_ROWS, :] = scores(r, kc0)

    def step(j, prefetch):
        c0 = pl.multiple_of(j * tk, tk)
        if prefetch:
            k_next = k_ref[pl.ds(pl.multiple_of((j + 1) * tk, tk), tk), :]
        v1 = jnp.concatenate([v_ref[pl.ds(c0, tk), :], ones], axis=1)
        for r in range(n_chunks):
            rs = slice(r * ATTN_ROWS, (r + 1) * ATTN_ROWS)
            s = s_scr[rs, :]
            if prefetch:
                s_scr[rs, :] = scores(r, k_next)
            mx = s[:, 0:LANES]
            for c in range(1, n_col):
                mx = jnp.maximum(mx, s[:, c * LANES:(c + 1) * LANES])
            m_prev = m_scr[rs, :]
            m_new = jnp.maximum(m_prev, jnp.max(mx, axis=1, keepdims=True))
            alpha = jnp.exp(m_prev - m_new)
            p = jnp.exp(s - jnp.concatenate([m_new] * n_col, axis=1)).astype(BF16)
            pv = jnp.dot(p, v1, preferred_element_type=F32)
            acc_scr[rs, :] = jnp.concatenate([alpha, alpha], axis=1) * acc_scr[rs, :] + pv
            m_scr[rs, :] = m_new

    def body(j, carry):
        step(j, True)
        return carry

    lax.fori_loop(0, n_kv - 1, body, 0)
    step(n_kv - 1, False)

    acc = acc_scr[...]
    o2 = acc[:, 0:LANES] / acc[:, LANES:2 * LANES]
    o = o2[0:tq, :] - lam_ref[0] * o2[tq:n_rows, :]
    ms = jnp.mean(o * o, axis=-1, keepdims=True)
    o_ref[...] = (o * lax.rsqrt(ms + RMS_EPS) * g_ref[...] * out_scale).astype(o_ref.dtype)


def _diff_attention(qkv, lam, subln_g, lam_init, tq, tk):
    L = qkv.shape[0]
    tq, tk = min(tq, L), min(tk, L)
    H = DIFF_HEADS
    return pl.pallas_call(
        functools.partial(_attn_kernel, tq=tq, tk=tk, out_scale=1.0 - lam_init),
        grid=(H, L // tq),
        in_specs=[pl.BlockSpec(memory_space=pltpu.SMEM),
                  pl.BlockSpec((tq, LANES), lambda h, i: (i, h)),
                  pl.BlockSpec((L, LANES), lambda h, i: (0, H + h)),
                  pl.BlockSpec((L, LANES), lambda h, i: (0, 2 * H + h)),
                  pl.BlockSpec((1, LANES), lambda h, i: (0, 0))],
        out_specs=pl.BlockSpec((tq, LANES), lambda h, i: (i, h)),
        out_shape=jax.ShapeDtypeStruct((L, DIFF_WIDTH), BF16),
        scratch_shapes=[pltpu.VMEM((2 * tq, LANES), BF16),
                        pltpu.VMEM((2 * tq, LANES), F32),
                        pltpu.VMEM((2 * tq, 2 * LANES), F32),
                        pltpu.VMEM((2 * tq, tk), F32)],
        compiler_params=_cp(("parallel", "parallel")),
        name="diff_attn",
    )(lam, qkv, qkv, qkv, subln_g)


def _hg_masks():
    C = HG_TILE
    t = np.arange(C)[:, None]
    s = np.arange(C)[None, :]
    out = []
    for reverse in (False, True):
        for h in HG_LEVELS:
            same = (t // (2 * h)) == (s // (2 * h))
            t_late = (t % (2 * h)) >= h
            s_late = (s % (2 * h)) >= h
            m = same & (~t_late & s_late if reverse else t_late & ~s_late)
            out.append(m)
        for d in range(HG_BAND):
            same = (t // HG_BAND) == (s // HG_BAND)
            out.append(same & ((s == t + d) if reverse else (s == t - d)))
    return jnp.asarray(np.stack(out).astype(np.float32))


def _hg_tri():
    C = HG_TILE
    t = np.arange(C)[:, None]
    r = np.arange(C)[None, :]
    return jnp.asarray(np.stack([(r <= t), (r >= t)]).astype(np.float32))


def _hg_tile(qr, fr, v, lb, st, tri, mask_ref, reverse):
    C = HG_TILE
    n_lvl = len(HG_LEVELS)
    m0 = (n_lvl + HG_BAND) if reverse else 0
    q = qr * jax.nn.sigmoid(qr)
    f = lb + (1.0 - lb) * jax.nn.sigmoid(fr)
    k = 1.0 - f
    g = jnp.log(f)
    b = jnp.dot(tri, g, precision=HI, preferred_element_type=F32)
    b_edge = b[0:1, :] if reverse else b[C - 1:C, :]
    vb = v.astype(BF16)

    qc = (q * jnp.exp(b)).astype(BF16)
    kc = (k * jnp.exp(b_edge - b)).astype(BF16)
    o = lax.dot_general(qc, st.astype(BF16), (((1,), (1,)), ((), ())), preferred_element_type=F32)
    st_new = st * jnp.exp(b_edge) + jnp.dot(v.T.astype(BF16), kc, preferred_element_type=F32)

    a = jnp.zeros((C, C), F32)
    for li, h in enumerate(HG_LEVELS):
        rows = []
        for blk in range(C // (2 * h)):
            r = blk * 2 * h + (h if reverse else h - 1)
            rows.append(jnp.broadcast_to(b[r:r + 1, :], (2 * h, LANES)))
        bref = rows[0] if len(rows) == 1 else jnp.concatenate(rows, axis=0)
        e = jnp.exp(-jnp.abs(b - bref))
        ah = lax.dot_general((q * e).astype(BF16), (k * e).astype(BF16), (((1,), (1,)), ((), ())),
                             preferred_element_type=F32)
        a = a + ah * mask_ref[m0 + li]
    def roll_rows(x, shift):
        x3 = pltpu.roll(x.reshape(C // HG_BAND, HG_BAND, LANES), shift % HG_BAND, 1)
        return x3.reshape(C, LANES)
    ed = None
    for d in range(HG_BAND):
        if d == 0:
            p = q * k
        else:
            sh = (C - d) if reverse else d
            shf = (C - (d - 1)) % C if reverse else d - 1
            fsh = f if shf == 0 else roll_rows(f, shf)
            ed = fsh if ed is None else ed * fsh
            p = q * roll_rows(k, sh) * ed
        a = a + jnp.sum(p, axis=1, keepdims=True) * mask_ref[m0 + n_lvl + d]
    o = o + jnp.dot(a.astype(BF16), vb, preferred_element_type=F32)
    return o, st_new


def _hgrn_kernel(qf_ref, qb_ref, ff_ref, fb_ref, vf_ref, vb_ref, lb_ref, tri_ref, mask_ref,
                 of_ref, ob_ref, stf_scr, stb_scr, *, n_tiles):
    @pl.when(pl.program_id(1) == 0)
    def _():
        stf_scr[...] = jnp.zeros(stf_scr.shape, F32)
        stb_scr[...] = jnp.zeros(stb_scr.shape, F32)

    C = HG_TILE
    lbf = lb_ref[0:1, :]
    lbb = lb_ref[1:2, :]

    def body(t, carry):
        rf = pl.multiple_of(t * C, C)
        rb = pl.multiple_of((n_tiles - 1 - t) * C, C)
        o_f, st_f = _hg_tile(qf_ref[pl.ds(rf, C), :], ff_ref[pl.ds(rf, C), :], vf_ref[pl.ds(rf, C), :],
                             lbf, stf_scr[...], tri_ref[0], mask_ref, False)
        of_ref[pl.ds(rf, C), :] = o_f
        stf_scr[...] = st_f
        o_b, st_b = _hg_tile(qb_ref[pl.ds(rb, C), :], fb_ref[pl.ds(rb, C), :], vb_ref[pl.ds(rb, C), :],
                             lbb, stb_scr[...], tri_ref[1], mask_ref, True)
        ob_ref[pl.ds(rb, C), :] = o_b
        stb_scr[...] = st_b
        return carry

    lax.fori_loop(0, n_tiles, body, 0, unroll=True)


def _hgrn2(proj, lb, rows):
    L = proj.shape[0]
    rows = min(rows, L)
    nb = L // rows
    H = HG_HEADS
    nm = len(HG_LEVELS) + HG_BAND
    blk = lambda col0, rev: pl.BlockSpec(
        (rows, LANES), (lambda h, i: (nb - 1 - i, col0 + h)) if rev else (lambda h, i: (i, col0 + h)))
    return pl.pallas_call(
        functools.partial(_hgrn_kernel, n_tiles=rows // HG_TILE),
        grid=(H, nb),
        in_specs=[blk(0, False), blk(0, True), blk(H, False), blk(2 * H, True),
                  blk(3 * H, False), blk(3 * H, True),
                  pl.BlockSpec((2, LANES), lambda h, i: (0, h)),
                  pl.BlockSpec((2, HG_TILE, HG_TILE), lambda h, i: (0, 0, 0)),
                  pl.BlockSpec((2 * nm, HG_TILE, HG_TILE), lambda h, i: (0, 0, 0))],
        out_specs=[blk(0, False), blk(0, True)],
        out_shape=[jax.ShapeDtypeStruct((L, HG_WIDTH), F32)] * 2,
        scratch_shapes=[pltpu.VMEM((LANES, LANES), F32)] * 2,
        compiler_params=_cp(("parallel", "arbitrary")),
        name="hgrn2",
    )(proj, proj, proj, proj, proj, proj, lb, _hg_tri(), _hg_masks())


def _outproj_kernel(x_ref, oa_ref, of_ref, ob_ref, gr_ref, ng_ref, gate_ref, lng_ref, lnb_ref, w_ref,
                    o_ref, lhs_scr):
    lhs_scr[:, 0:DIFF_WIDTH] = oa_ref[...]
    ng = ng_ref[...]
    for h in range(HG_HEADS):
        sl = slice(h * LANES, (h + 1) * LANES)
        o = of_ref[:, sl] + ob_ref[:, sl]
        ms = jnp.mean(o * o, axis=-1, keepdims=True)
        gr = gr_ref[:, sl]
        hg = o * lax.rsqrt(ms + RMS_EPS) * ng * (gr * jax.nn.sigmoid(gr))
        lhs_scr[:, DIFF_WIDTH + h * LANES:DIFF_WIDTH + (h + 1) * LANES] = hg.astype(BF16)
    y = jnp.dot(lhs_scr[...], w_ref[...], preferred_element_type=F32)
    z = ALPHA * x_ref[...] + (1.0 + gate_ref[...]) * y
    o_ref[...] = _layer_norm_rows(z, lng_ref[...], lnb_ref[...])


def _outproj_ln(x, oa, of, ob, proj_hg, norm_g, gate, ln_g, ln_b, w, tm):
    L = x.shape[0]
    tm = min(tm, L)
    row = lambda i: (i, 0)
    vec = pl.BlockSpec((1, D_MODEL), lambda i: (0, 0))
    return pl.pallas_call(
        _outproj_kernel,
        grid=(L // tm,),
        in_specs=[pl.BlockSpec((tm, D_MODEL), row),
                  pl.BlockSpec((tm, DIFF_WIDTH), row),
                  pl.BlockSpec((tm, HG_WIDTH), row),
                  pl.BlockSpec((tm, HG_WIDTH), row),
                  pl.BlockSpec((tm, HG_WIDTH), lambda i: (i, 4)),
                  pl.BlockSpec((1, LANES), lambda i: (0, 0)),
                  vec, vec, vec,
                  pl.BlockSpec((D_MODEL, D_MODEL), lambda i: (0, 0))],
        out_specs=pl.BlockSpec((tm, D_MODEL), row),
        out_shape=jax.ShapeDtypeStruct((L, D_MODEL), F32),
        scratch_shapes=[pltpu.VMEM((tm, D_MODEL), BF16)],
        compiler_params=_cp(("parallel",)),
        name="outproj_ln",
    )(x, oa, of, ob, proj_hg, norm_g, gate, ln_g, ln_b, w)


def _mlp_kernel(x_ref, sc_ref, sh_ref, gate_ref, lng_ref, lnb_ref, w1_ref, w2_ref, o_ref, h_scr, acc_scr):
    j = pl.program_id(1)

    @pl.when(j == 0)
    def _():
        h_scr[...] = (x_ref[...] * (1.0 + sc_ref[...]) + sh_ref[...]).astype(BF16)
        acc_scr[...] = jnp.zeros(acc_scr.shape, F32)

    a = jnp.maximum(jnp.dot(h_scr[...], w1_ref[...], preferred_element_type=F32), 0.0)
    acc_scr[...] += jnp.dot((a * a).astype(BF16), w2_ref[...], preferred_element_type=F32)

    @pl.when(j == pl.num_programs(1) - 1)
    def _():
        z = ALPHA * x_ref[...] + (1.0 + gate_ref[...]) * acc_scr[...]
        o_ref[...] = _layer_norm_rows(z, lng_ref[...], lnb_ref[...])


def _mlp_ln(x, scale, shift, gate, ln_g, ln_b, w1, w2, tm, tf):
    L = x.shape[0]
    tm = min(tm, L)
    vec = pl.BlockSpec((1, D_MODEL), lambda i, j: (0, 0))
    return pl.pallas_call(
        _mlp_kernel,
        grid=(L // tm, D_FF // tf),
        in_specs=[pl.BlockSpec((tm, D_MODEL), lambda i, j: (i, 0)),
                  vec, vec, vec, vec, vec,
                  pl.BlockSpec((D_MODEL, tf), lambda i, j: (0, j)),
                  pl.BlockSpec((tf, D_MODEL), lambda i, j: (j, 0))],
        out_specs=pl.BlockSpec((tm, D_MODEL), lambda i, j: (i, 0)),
        out_shape=jax.ShapeDtypeStruct((L, D_MODEL), F32),
        scratch_shapes=[pltpu.VMEM((tm, D_MODEL), BF16), pltpu.VMEM((tm, D_MODEL), F32)],
        compiler_params=_cp(("parallel", "arbitrary")),
        name="mlp_ln",
    )(x, scale, shift, gate, ln_g, ln_b, w1, w2)


def _s5_operators(lam_re, lam_im, log_dt, b_re, b_im, c_re, c_im):
    T, G, P, Hc, NB, GPB = S5_T, S5_GROUPS, S5_STATE, S5_GROUP, S5_BLOCKS, S5_GPB
    dt = jnp.exp(log_dt)[..., None]
    zr, zi = lam_re * dt, lam_im * dt
    mag = jnp.exp(zr)
    ab_re, ab_im = mag * jnp.cos(zi), mag * jnp.sin(zi)
    den = lam_re * lam_re + lam_im * lam_im
    coef_re = ((ab_re - 1.0) * lam_re + ab_im * lam_im) / den
    coef_im = (ab_im * lam_re - (ab_re - 1.0) * lam_im) / den
    bb_re = coef_re[..., None] * b_re - coef_im[..., None] * b_im
    bb_im = coef_re[..., None] * b_im + coef_im[..., None] * b_re

    def apow(k):
        k = jnp.asarray(k, F32)[:, None, None, None]
        m = jnp.exp(zr[None] * k)
        return m * jnp.cos(zi[None] * k), m * jnp.sin(zi[None] * k)

    pr, pi = apow(np.arange(T))
    ca_re = c_re[None] * pr[:, :, :, None, :] - c_im[None] * pi[:, :, :, None, :]
    ca_im = c_re[None] * pi[:, :, :, None, :] + c_im[None] * pr[:, :, :, None, :]
    bt_re, bt_im = jnp.swapaxes(bb_re, 2, 3), jnp.swapaxes(bb_im, 2, 3)
    lagk = jnp.sum(ca_re[:, :, :, :, None, :] * bt_re[None, :, :, None, :, :]
                   - ca_im[:, :, :, :, None, :] * bt_im[None, :, :, None, :, :], axis=-1)
    kf, kb = lagk[:, 0], lagk[:, 1]
    kall = jnp.concatenate([kb[:0:-1], (kf[0] + kb[0])[None], kf[1:]], axis=0)
    kc = jnp.transpose(kall.reshape(2 * T - 1, NB, GPB, Hc, Hc), (1, 0, 4, 2, 3))
    kc = kc.reshape(NB, 2 * T - 1, Hc, LANES)

    def gp_lanes(v):
        return jnp.transpose(v.reshape(2, NB, GPB, Hc, P), (1, 0, 3, 2, 4)).reshape(NB, 2, Hc, GPB * P)
    bt = jnp.stack([gp_lanes(bt_re), gp_lanes(bt_im)], axis=1)
    def e_tab(v):
        v = jnp.stack([v[::-1, 0], v[:, 1]], axis=1).reshape(T, 2, NB, GPB * P)
        return jnp.transpose(v, (2, 0, 1, 3))
    et = jnp.stack([e_tab(pr), e_tab(pi)], axis=2)[:, :, :, :, None, :]

    def gc_lanes(v):
        return jnp.transpose(v.reshape(2, NB, GPB, Hc, P), (1, 0, 4, 2, 3)).reshape(NB, 2, P, LANES)
    cb = jnp.stack([gc_lanes(c_re), gc_lanes(c_im)], axis=1)
    cr, ci = apow(np.arange(1, T + 1))
    def p_tab(v):
        v = jnp.stack([v[:, 0], v[::-1, 1]], axis=1).reshape(T, 2, NB, GPB * P // LANES, LANES)
        return jnp.transpose(v, (2, 1, 3, 4, 0))
    p_re, p_im = p_tab(cr), p_tab(ci)
    pt = jnp.stack([jnp.stack([p_re, p_im], axis=3), jnp.stack([-p_im, p_re], axis=3)], axis=3)
    return tuple(_s5_expand(kc, bt, et, cb, pt)) + ((zr, zi),)


def _s5_expand_kernel(kc_ref, bt_ref, et_ref, cb_ref, pt_ref, wt_ref, ws_ref, wc_ref):
    s = pl.program_id(1)
    n_q = S5_GPB * S5_STATE // LANES
    gpq = S5_GPB // n_q
    row = lax.broadcasted_iota(jnp.int32, (LANES, LANES), 0)
    col = lax.broadcasted_iota(jnp.int32, (LANES, LANES), 1)
    over_groups = lambda a: jnp.concatenate([a] * S5_GPB, axis=0)

    same_group = (row // S5_GROUP) == (col // S5_GROUP)
    for t in range(S5_T):
        k = over_groups(kc_ref[0, t - s + S5_T - 1])
        wt_ref[0, :, t * LANES:(t + 1) * LANES] = jnp.where(same_group, k, 0.0).astype(BF16)

    for d in range(2):
        for q in range(n_q):
            sl = slice(q * LANES, (q + 1) * LANES)
            b_re, b_im = bt_ref[0, 0, d, :, sl], bt_ref[0, 1, d, :, sl]
            e_re, e_im = et_ref[0, 0, 0, d, :, sl], et_ref[0, 0, 1, d, :, sl]
            own = (row // S5_GROUP) == (gpq * q + col // S5_STATE)
            for ri, val in ((0, e_re * b_re - e_im * b_im), (1, e_re * b_im + e_im * b_re)):
                c0 = ((d * n_q + q) * 2 + ri) * LANES
                ws_ref[0, :, c0:c0 + LANES] = jnp.where(own, over_groups(val), 0.0).astype(BF16)

    q = (s // 2) % n_q
    own = (gpq * q + row // S5_STATE) == (col // S5_GROUP)
    c_re = jnp.concatenate([cb_ref[0, 0, 0]] * gpq, axis=0)
    c_im = jnp.concatenate([cb_ref[0, 1, 0]] * gpq, axis=0)
    for t in range(S5_T):
        x = pt_ref[0, 0, 0, 0, 0, :, t:t + 1]
        y = pt_ref[0, 0, 0, 0, 1, :, t:t + 1]
        wc_ref[0, :, t * LANES:(t + 1) * LANES] = jnp.where(own, c_re * x - c_im * y, 0.0).astype(BF16)


def _s5_expand(kc, bt, et, cb, pt):
    NB, T, Hc, P = S5_BLOCKS, S5_T, S5_GROUP, S5_STATE
    n_q = S5_GPB * P // LANES
    assert 2 * S5_SW // LANES == T and 2 * 2 * n_q == T
    slab = pl.BlockSpec((1, LANES, S5_ROW), lambda b, s: (b, s, 0))
    return pl.pallas_call(
        _s5_expand_kernel,
        grid=(NB, T),
        in_specs=[pl.BlockSpec((1, 2 * T - 1, Hc, LANES), lambda b, s: (b, 0, 0, 0)),
                  pl.BlockSpec((1, 2, 2, Hc, S5_GPB * P), lambda b, s: (b, 0, 0, 0, 0)),
                  pl.BlockSpec((1, 1, 2, 2, 1, S5_GPB * P), lambda b, s: (b, s, 0, 0, 0, 0)),
                  pl.BlockSpec((1, 2, 1, P, LANES), lambda b, s: (b, 0, s // 8, 0, 0)),
                  pl.BlockSpec((1, 1, 1, 1, 2, LANES, T),
                               lambda b, s: (b, s // 8, (s // 2) % n_q, s % 2, 0, 0, 0))],
        out_specs=[slab, slab, slab],
        out_shape=[jax.ShapeDtypeStruct((NB, S5_ROW, S5_ROW), BF16)] * 3,
        compiler_params=_cp(("parallel", "arbitrary")),
        name="s5_expand",
    )(kc, bt, et, cb, pt)


def _s5_scan_powers(zr, zi, n_steps):
    k = (S5_T * (2.0 ** np.arange(n_steps))).astype(np.float32)[:, None, None, None]
    m = jnp.exp(zr[None] * k)
    re, im = m * jnp.cos(zi[None] * k), m * jnp.sin(zi[None] * k)
    x = jnp.stack([re, im], axis=0)
    x = x.reshape(2, n_steps, 2, S5_BLOCKS, S5_GPB * S5_STATE // LANES, LANES)
    x = jnp.transpose(x, (2, 1, 3, 4, 0, 5))
    return x.reshape(2, n_steps, S5_BLOCKS * S5_SW)


def _s5_fold(x_ref, sc, sh, tc):
    parts = [x_ref[pl.ds(t, tc, stride=S5_T), :] * (1.0 + sc) + sh for t in range(S5_T)]
    return parts


def _s5_state_kernel(x_ref, sc_ref, sh_ref, w_ref, zf_ref, zb_ref, *, tc):
    parts = _s5_fold(x_ref, sc_ref[...], sh_ref[...], tc)
    u = jnp.concatenate([p.astype(BF16) for p in parts], axis=1)
    z = jnp.dot(u, w_ref[0], preferred_element_type=F32)
    zf_ref[...] = z[:, :S5_SW]
    zb_ref[...] = z[:, S5_SW:]


def _s5_scan_kernel(zf_ref, zb_ref, af_ref, ab_ref, sf_ref, sb_ref, *, n_rows, n_steps):
    row = lax.broadcasted_iota(jnp.int32, (n_rows, LANES), 0)

    def scan(z_ref, a_ref, reverse):
        re, im = z_ref[:, 0:LANES], z_ref[:, LANES:2 * LANES]
        def shifted(v, sh):
            if reverse:
                return jnp.where(row < n_rows - sh, pltpu.roll(v, n_rows - sh, 0), 0.0)
            return jnp.where(row >= sh, pltpu.roll(v, sh, 0), 0.0)
        for k in range(n_steps):
            sh = 2 ** k
            ar, ai = a_ref[0, k:k + 1, 0:LANES], a_ref[0, k:k + 1, LANES:2 * LANES]
            rs, is_ = shifted(re, sh), shifted(im, sh)
            re, im = re + ar * rs - ai * is_, im + ar * is_ + ai * rs
        return shifted(re, 1), shifted(im, 1)

    re, im = scan(zf_ref, af_ref, False)
    sf_ref[:, 0:LANES] = re.astype(BF16)
    sf_ref[:, LANES:2 * LANES] = im.astype(BF16)
    re, im = scan(zb_ref, ab_ref, True)
    sb_ref[:, 0:LANES] = re.astype(BF16)
    sb_ref[:, LANES:2 * LANES] = im.astype(BF16)


def _s5_out_kernel(x_ref, sc_ref, sh_ref, d_ref, sf_ref, sb_ref, wt_ref, wc_ref, o_ref, *, tc, n_half):
    j = pl.program_id(2)
    sc, sh, d = sc_ref[...], sh_ref[...], d_ref[...]
    parts = _s5_fold(x_ref, sc, sh, tc)
    u = jnp.concatenate([p.astype(BF16) for p in parts], axis=1)
    s = jnp.concatenate([sf_ref[...], sb_ref[...]], axis=1)
    y = (jnp.dot(u, wt_ref[0], preferred_element_type=F32)
         + jnp.dot(s, wc_ref[0], preferred_element_type=F32))
    tph = S5_T // n_half
    for jj in range(n_half):
        @pl.when(j == jj)
        def _():
            for tl in range(tph):
                t = jj * tph + tl
                o_ref[pl.ds(t, tc, stride=S5_T), :] = y[:, tl * LANES:(tl + 1) * LANES] + d * parts[t]


def _s5_mixer_core(x, scale, shift, d_skip, ops, tc):
    w_toep, w_state, w_carry, (zr, zi) = ops
    L = x.shape[0]
    n_rows = L // S5_T
    NB = S5_BLOCKS
    n_steps = int(math.log2(n_rows))
    assert 2 ** n_steps == n_rows
    tc = min(tc, n_rows)
    vecb = pl.BlockSpec((1, LANES), lambda b, i: (0, b))

    zf, zb = pl.pallas_call(
        functools.partial(_s5_state_kernel, tc=tc),
        grid=(NB, n_rows // tc),
        in_specs=[pl.BlockSpec((tc * S5_T, LANES), lambda b, i: (i, b)), vecb, vecb,
                  pl.BlockSpec((1, S5_ROW, 2 * S5_SW), lambda b, i: (b, 0, 0))],
        out_specs=[pl.BlockSpec((tc, S5_SW), lambda b, i: (i, b))] * 2,
        out_shape=[jax.ShapeDtypeStruct((n_rows, NB * S5_SW), F32)] * 2,
        compiler_params=_cp(("parallel", "parallel")),
        name="s5_state",
    )(x, scale, shift, w_state)

    pw = _s5_scan_powers(zr, zi, n_steps)
    cw = 2 * LANES
    sf, sb = pl.pallas_call(
        functools.partial(_s5_scan_kernel, n_rows=n_rows, n_steps=n_steps),
        grid=(NB * S5_SW // cw,),
        in_specs=[pl.BlockSpec((n_rows, cw), lambda n: (0, n)),
                  pl.BlockSpec((n_rows, cw), lambda n: (0, n)),
                  pl.BlockSpec((1, n_steps, cw), lambda n: (0, 0, n)),
                  pl.BlockSpec((1, n_steps, cw), lambda n: (1, 0, n))],
        out_specs=[pl.BlockSpec((n_rows, cw), lambda n: (0, n))] * 2,
        out_shape=[jax.ShapeDtypeStruct((n_rows, NB * S5_SW), BF16)] * 2,
        compiler_params=_cp(("parallel",)),
        name="s5_scan",
    )(zf, zb, pw, pw)

    n_half = 2
    nh = S5_ROW // n_half
    vec3 = pl.BlockSpec((1, LANES), lambda b, i, j: (0, b))
    return pl.pallas_call(
        functools.partial(_s5_out_kernel, tc=tc, n_half=n_half),
        grid=(NB, n_rows // tc, n_half),
        in_specs=[pl.BlockSpec((tc * S5_T, LANES), lambda b, i, j: (i, b)), vec3, vec3, vec3,
                  pl.BlockSpec((tc, S5_SW), lambda b, i, j: (i, b)),
                  pl.BlockSpec((tc, S5_SW), lambda b, i, j: (i, b)),
                  pl.BlockSpec((1, S5_ROW, nh), lambda b, i, j: (b, 0, j)),
                  pl.BlockSpec((1, 2 * S5_SW, nh), lambda b, i, j: (b, 0, j))],
        out_specs=pl.BlockSpec((tc * S5_T, LANES), lambda b, i, j: (i, b)),
        out_shape=jax.ShapeDtypeStruct((L, D_MODEL), F32),
        compiler_params=_cp(("parallel", "parallel", "arbitrary")),
        name="s5_out",
    )(x, scale, shift, d_skip, sf, sb, w_toep, w_carry)


def _glu_kernel(x_ref, y_ref, gate_ref, lng_ref, lnb_ref, wa_ref, wg_ref, o_ref, lhs_scr, mix_scr, *, tn):
    j = pl.program_id(1)

    @pl.when(j == 0)
    def _():
        lhs_scr[...] = jax.nn.gelu(y_ref[...]).astype(BF16)

    a = jnp.dot(lhs_scr[...], wa_ref[...], preferred_element_type=F32)
    g = jnp.dot(lhs_scr[...], wg_ref[...], preferred_element_type=F32)
    mix = a * jax.nn.sigmoid(g)
    for jj in range(D_MODEL // tn):
        @pl.when(j == jj)
        def _():
            mix_scr[:, jj * tn:(jj + 1) * tn] = mix

    @pl.when(j == pl.num_programs(1) - 1)
    def _():
        z = ALPHA * x_ref[...] + (1.0 + gate_ref[...]) * mix_scr[...]
        o_ref[...] = _layer_norm_rows(z, lng_ref[...], lnb_ref[...])


def _glu_ln(x, y, gate, ln_g, ln_b, w_glu, tm, tn):
    L = x.shape[0]
    tm = min(tm, L)
    nj = D_MODEL // tn
    vec = pl.BlockSpec((1, D_MODEL), lambda i, j: (0, 0))
    return pl.pallas_call(
        functools.partial(_glu_kernel, tn=tn),
        grid=(L // tm, nj),
        in_specs=[pl.BlockSpec((tm, D_MODEL), lambda i, j: (i, 0)),
                  pl.BlockSpec((tm, D_MODEL), lambda i, j: (i, 0)),
                  vec, vec, vec,
                  pl.BlockSpec((D_MODEL, tn), lambda i, j: (0, j)),
                  pl.BlockSpec((D_MODEL, tn), lambda i, j: (0, nj + j))],
        out_specs=pl.BlockSpec((tm, D_MODEL), lambda i, j: (i, 0)),
        out_shape=jax.ShapeDtypeStruct((L, D_MODEL), F32),
        scratch_shapes=[pltpu.VMEM((tm, D_MODEL), BF16), pltpu.VMEM((tm, D_MODEL), F32)],
        compiler_params=_cp(("parallel", "arbitrary")),
        name="glu_ln",
    )(x, y, gate, ln_g, ln_b, w_glu, w_glu)


def kernel(x, c, ada_w, ada_b, ln_g, ln_b, mix_w_in, mix_w_out, diff_lambda, diff_subln_g, hg_lower_bound,
           hg_norm_g, s5_lambda_re, s5_lambda_im, s5_log_dt, s5_b_re, s5_b_im, s5_c_re, s5_c_im, s5_d,
           s5_w_glu, mlp_w1, mlp_w2):
    B, L, D = x.shape
    assert B == 1 and D == D_MODEL
    xs = x.reshape(L, D)
    mod = _adaln(c, ada_w, ada_b)
    mods = lambda a: (mod[a, :, 0:D], mod[a, :, D:2 * D], mod[a, :, 2 * D:3 * D])
    lb_table = jnp.cumsum(jax.nn.softmax(hg_lower_bound.astype(F32), axis=1), axis=1)
    rot_tabs = _rotary_tables(L)

    for layer in range(DEPTH):
        shift, scale, gate = mods(2 * layer)
        lng, lnb = ln_g[layer, 0][None], ln_b[layer, 0][None]
        if layer % 2 == 0:
            e = layer // 2
            lam_init = 0.8 - 0.6 * math.exp(-0.3 * layer)
            w_in = mix_w_in[e].astype(BF16)
            qkv = _inproj(xs, scale, shift, w_in[:, :3 * DIFF_WIDTH], rot_tabs, 2, BF16, 1024, DIFF_WIDTH)
            proj_hg = _inproj(xs, scale, shift, w_in[:, 3 * DIFF_WIDTH:], rot_tabs, 0, F32, 1024, HG_WIDTH)
            lf = diff_lambda[e].astype(F32)
            lam = jnp.exp(jnp.sum(lf[0] * lf[1])) - jnp.exp(jnp.sum(lf[2] * lf[3])) + lam_init
            oa = _diff_attention(qkv, lam.reshape(1), diff_subln_g[e][None], lam_init, 2048, 1024)
            of, ob = _hgrn2(proj_hg, lb_table[:, layer], 1024)
            xs = _outproj_ln(xs, oa, of, ob, proj_hg, hg_norm_g[e][None], gate, lng, lnb,
                             mix_w_out[e].astype(BF16), 256)
        else:
            o = layer // 2
            ops = _s5_operators(s5_lambda_re[o], s5_lambda_im[o], s5_log_dt[o], s5_b_re[o], s5_b_im[o],
                                s5_c_re[o], s5_c_im[o])
            y = _s5_mixer_core(xs, scale, shift, s5_d[o][None], ops, 512)
            xs = _glu_ln(xs, y, gate, lng, lnb, s5_w_glu[o].astype(BF16), 512, 1024)
        shift, scale, gate = mods(2 * layer + 1)
        xs = _mlp_ln(xs, scale, shift, gate, ln_g[layer, 1][None], ln_b[layer, 1][None],
                     mlp_w1[layer].astype(BF16), mlp_w2[layer].astype(BF16), 512, 1024)
    return xs.reshape(B, L, D)
```

```python
import functools
import math

import numpy as np
import jax
import jax.numpy as jnp
from jax import lax
from jax.experimental import pallas as pl
from jax.experimental.pallas import tpu as pltpu

F32 = jnp.float32
BF16 = jnp.bfloat16

D_MODEL = 2048
DEPTH = 2
LANES = 128
SUBLANES = 8

DIFF_HEADS = 8
DIFF_QK_DIM = 64
DIFF_WIDTH = 1024
ROT_DIM = DIFF_QK_DIM // 4
ROPE_THETA = 500000.0
ATTN_ROWS = 128

HG_HEADS = 8
HG_WIDTH = 1024
HG_TILE = 128
HG_BAND = SUBLANES
HG_LEVELS = (64, 32, 16, 8)

S5_GROUP = 16
S5_GROUPS = D_MODEL // S5_GROUP
S5_STATE = 64
S5_T = 16
S5_GPB = LANES // S5_GROUP
S5_BLOCKS = D_MODEL // LANES
S5_ROW = S5_T * LANES
S5_SW = S5_GPB * S5_STATE * 2

D_FF = 4 * D_MODEL
ALPHA = (2 * DEPTH) ** 0.25
LN_EPS = 1e-5
RMS_EPS = 1e-6
HI = lax.Precision.HIGHEST

VMEM_LIMIT = 56 * 1024 * 1024
CAST_BLOCK_ELEMS = 2 * 1024 * 1024

ADALN_TN = 768
INPROJ_TM, INPROJ_TN = 1024, 1024
ATTN_TQ, ATTN_TK = 2048, 1024
HG_ROWS = 1024
OUTPROJ_TM = 256
MLP_TM, MLP_TF = 512, 1024
S5_TC = 512
GLU_TM, GLU_TN = 512, 1024


def _cp(sem):
    return pltpu.CompilerParams(dimension_semantics=sem, vmem_limit_bytes=VMEM_LIMIT)


def _bf16_kernel(w_ref, o_ref):
    o_ref[...] = w_ref[0].astype(BF16)


def _to_bf16(w, layer):
    _, r, c = w.shape
    tr = max(SUBLANES, CAST_BLOCK_ELEMS // c)
    assert r % tr == 0
    return pl.pallas_call(
        _bf16_kernel, grid=(r // tr,),
        in_specs=[pl.BlockSpec((1, tr, c), lambda i: (layer, i, 0))],
        out_specs=pl.BlockSpec((tr, c), lambda i: (i, 0)),
        out_shape=jax.ShapeDtypeStruct((r, c), BF16), compiler_params=_cp(("parallel",)), name="to_bf16",
    )(w)


def _layer_norm_rows(z, g, b):
    mu = jnp.mean(z, axis=-1, keepdims=True)
    zc = z - mu
    var = jnp.mean(zc * zc, axis=-1, keepdims=True)
    return zc * lax.rsqrt(var + LN_EPS) * g + b


def _adaln_kernel(c_ref, w_ref, b_ref, o_ref):
    c = c_ref[...]
    ca = c * jax.nn.sigmoid(c)
    w = w_ref[0]
    reps = w.shape[1] // LANES
    cb = jnp.concatenate([ca] * reps, axis=1)
    o_ref[0] = jnp.sum(w * cb, axis=0, keepdims=True) + b_ref[0]


def _adaln(c, ada_w, ada_b):
    n = 3 * D_MODEL
    tn = ADALN_TN
    w = ada_w.reshape(2 * DEPTH, D_MODEL, n)
    b = ada_b.reshape(2 * DEPTH, 1, n)
    c_rep = jnp.broadcast_to(c.reshape(D_MODEL, 1), (D_MODEL, LANES))
    return pl.pallas_call(
        _adaln_kernel,
        grid=(2 * DEPTH, n // tn),
        in_specs=[pl.BlockSpec((D_MODEL, LANES), lambda a, j: (0, 0)),
                  pl.BlockSpec((1, D_MODEL, tn), lambda a, j: (a, 0, j)),
                  pl.BlockSpec((1, 1, tn), lambda a, j: (a, 0, j))],
        out_specs=pl.BlockSpec((1, 1, tn), lambda a, j: (a, 0, j)),
        out_shape=jax.ShapeDtypeStruct((2 * DEPTH, 1, n), F32),
        compiler_params=_cp(("parallel", "parallel")),
        name="adaln",
    )(c_rep, w, b)


def _inproj_kernel(x_ref, sc_ref, sh_ref, w_ref, cos_ref, sin_ref, o_ref, h_scr, *, n_rot):
    j = pl.program_id(1)

    @pl.when(j == 0)
    def _():
        h_scr[...] = (x_ref[...] * (1.0 + sc_ref[...]) + sh_ref[...]).astype(BF16)

    y = jnp.dot(h_scr[...], w_ref[...], preferred_element_type=F32)

    @pl.when(j < n_rot)
    def _():
        half = ROT_DIM // 2
        cs, sn = cos_ref[...], sin_ref[...]
        pos = lax.broadcasted_iota(jnp.int32, sn.shape, 1) % DIFF_QK_DIM
        sa = jnp.where(pos >= half, sn, 0.0)
        sb = jnp.where(pos < half, -sn, 0.0)
        for g in range(y.shape[1] // LANES):
            yg = y[:, g * LANES:(g + 1) * LANES]
            rot = (yg * cs + pltpu.roll(yg, ROT_DIM // 2, 1) * sa
                   + pltpu.roll(yg, LANES - ROT_DIM // 2, 1) * sb)
            o_ref[:, g * LANES:(g + 1) * LANES] = rot.astype(o_ref.dtype)

    @pl.when(j >= n_rot)
    def _():
        o_ref[...] = y.astype(o_ref.dtype)


def _inproj(x, scale, shift, w, col0, n, rot_tabs, n_rot, out_dtype, tm, tn):
    L = x.shape[0]
    tm = min(tm, L)
    j0 = col0 // tn
    return pl.pallas_call(
        functools.partial(_inproj_kernel, n_rot=n_rot),
        grid=(L // tm, n // tn),
        in_specs=[pl.BlockSpec((tm, D_MODEL), lambda i, j: (i, 0)),
                  pl.BlockSpec((1, D_MODEL), lambda i, j: (0, 0)),
                  pl.BlockSpec((1, D_MODEL), lambda i, j: (0, 0)),
                  pl.BlockSpec((D_MODEL, tn), lambda i, j: (0, j0 + j)),
                  pl.BlockSpec((tm, LANES), lambda i, j: (i, 0)),
                  pl.BlockSpec((tm, LANES), lambda i, j: (i, 0))],
        out_specs=pl.BlockSpec((tm, tn), lambda i, j: (i, j)),
        out_shape=jax.ShapeDtypeStruct((L, n), out_dtype),
        scratch_shapes=[pltpu.VMEM((tm, D_MODEL), BF16)],
        compiler_params=_cp(("parallel", "arbitrary")),
        name="inproj",
    )(x, scale, shift, w, *rot_tabs)


def _rotary_tables(L):
    half = ROT_DIM // 2
    pos = jnp.arange(L, dtype=F32)
    inv_freq = 1.0 / (ROPE_THETA ** (jnp.arange(0, ROT_DIM, 2, dtype=F32) / ROT_DIM))
    ang = pos[:, None] * inv_freq[None, :]
    cos, sin = jnp.cos(ang), jnp.sin(ang)
    ones = jnp.ones((L, DIFF_QK_DIM - ROT_DIM), F32)
    zrest = jnp.zeros((L, DIFF_QK_DIM - ROT_DIM), F32)
    c_comp = jnp.concatenate([cos, cos, ones], axis=1)
    s_comp = jnp.concatenate([sin, sin, zrest], axis=1)
    two = lambda t: jnp.concatenate([t, t], axis=1)
    return two(c_comp), two(s_comp)


def _attn_kernel(lam_ref, q_ref, k_ref, v_ref, g_ref, o_ref, q2_scr, m_scr, acc_scr, s_scr,
                 *, tq, tk, out_scale):
    n_rows = 2 * tq
    n_chunks = n_rows // ATTN_ROWS
    n_kv = k_ref.shape[0] // tk
    q = q_ref[...]
    lane = lax.broadcasted_iota(jnp.int32, q.shape, 1)
    qs = q * (DIFF_QK_DIM ** -0.5)
    q2_scr[0:tq, :] = jnp.where(lane < DIFF_QK_DIM, qs, 0).astype(BF16)
    q2_scr[tq:n_rows, :] = jnp.where(lane >= DIFF_QK_DIM, qs, 0).astype(BF16)
    m_scr[...] = jnp.full(m_scr.shape, -jnp.inf, F32)
    acc_scr[...] = jnp.zeros(acc_scr.shape, F32)
    ones = jnp.ones((tk, LANES), BF16)
    n_col = tk // LANES

    def scores(r, kc):
        rs = slice(r * ATTN_ROWS, (r + 1) * ATTN_ROWS)
        return lax.dot_general(q2_scr[rs, :], kc, (((1,), (1,)), ((), ())), preferred_element_type=F32)

    kc0 = k_ref[0:tk, :]
    for r in range(n_chunks):
        s_scr[r * ATTN_ROWS:(r + 1) * ATTN_ROWS, :] = scores(r, kc0)

    def step(j, prefetch):
        c0 = pl.multiple_of(j * tk, tk)
        if prefetch:
            k_next = k_ref[pl.ds(pl.multiple_of((j + 1) * tk, tk), tk), :]
        v1 = jnp.concatenate([v_ref[pl.ds(c0, tk), :], ones], axis=1)
        for r in range(n_chunks):
            rs = slice(r * ATTN_ROWS, (r + 1) * ATTN_ROWS)
            s = s_scr[rs, :]
            if prefetch:
                s_scr[rs, :] = scores(r, k_next)
            mx = s[:, 0:LANES]
            for c in range(1, n_col):
                mx = jnp.maximum(mx, s[:, c * LANES:(c + 1) * LANES])
            m_prev = m_scr[rs, :]
            m_new = jnp.maximum(m_prev, jnp.max(mx, axis=1, keepdims=True))
            alpha = jnp.exp(m_prev - m_new)
            p = jnp.exp(s - jnp.concatenate([m_new] * n_col, axis=1)).astype(BF16)
            pv = jnp.dot(p, v1, preferred_element_type=F32)
            acc_scr[rs, :] = jnp.concatenate([alpha, alpha], axis=1) * acc_scr[rs, :] + pv
            m_scr[rs, :] = m_new

    def body(j, carry):
        step(j, True)
        return carry

    lax.fori_loop(0, n_kv - 1, body, 0)
    step(n_kv - 1, False)

    acc = acc_scr[...]
    o2 = acc[:, 0:LANES] / acc[:, LANES:2 * LANES]
    o = o2[0:tq, :] - lam_ref[0] * o2[tq:n_rows, :]
    ms = jnp.mean(o * o, axis=-1, keepdims=True)
    o_ref[...] = (o * lax.rsqrt(ms + RMS_EPS) * g_ref[...] * out_scale).astype(o_ref.dtype)


def _diff_attention(qkv, lam, subln_g, lam_init, tq, tk):
    L = qkv.shape[0]
    tq, tk = min(tq, L), min(tk, L)
    H = DIFF_HEADS
    return pl.pallas_call(
        functools.partial(_attn_kernel, tq=tq, tk=tk, out_scale=1.0 - lam_init),
        grid=(H, L // tq),
        in_specs=[pl.BlockSpec(memory_space=pltpu.SMEM),
                  pl.BlockSpec((tq, LANES), lambda h, i: (i, h)),
                  pl.BlockSpec((L, LANES), lambda h, i: (0, H + h)),
                  pl.BlockSpec((L, LANES), lambda h, i: (0, 2 * H + h)),
                  pl.BlockSpec((1, LANES), lambda h, i: (0, 0))],
        out_specs=pl.BlockSpec((tq, LANES), lambda h, i: (i, h)),
        out_shape=jax.ShapeDtypeStruct((L, DIFF_WIDTH), BF16),
        scratch_shapes=[pltpu.VMEM((2 * tq, LANES), BF16),
                        pltpu.VMEM((2 * tq, LANES), F32),
                        pltpu.VMEM((2 * tq, 2 * LANES), F32),
                        pltpu.VMEM((2 * tq, tk), F32)],
        compiler_params=_cp(("parallel", "parallel")),
        name="diff_attn",
    )(lam, qkv, qkv, qkv, subln_g)


def _hg_masks():
    C = HG_TILE
    t = np.arange(C)[:, None]
    s = np.arange(C)[None, :]
    out = []
    for reverse in (False, True):
        for h in HG_LEVELS:
            same = (t // (2 * h)) == (s // (2 * h))
            t_late = (t % (2 * h)) >= h
            s_late = (s % (2 * h)) >= h
            m = same & (~t_late & s_late if reverse else t_late & ~s_late)
            out.append(m)
        for d in range(HG_BAND):
            same = (t // HG_BAND) == (s // HG_BAND)
            out.append(same & ((s == t + d) if reverse else (s == t - d)))
    return jnp.asarray(np.stack(out).astype(np.float32))


def _hg_tri():
    C = HG_TILE
    t = np.arange(C)[:, None]
    r = np.arange(C)[None, :]
    return jnp.asarray(np.stack([(r <= t), (r >= t)]).astype(np.float32))


def _hg_tile(qr, fr, v, lb, st, tri, mask_ref, reverse):
    C = HG_TILE
    n_lvl = len(HG_LEVELS)
    m0 = (n_lvl + HG_BAND) if reverse else 0
    q = qr * jax.nn.sigmoid(qr)
    f = lb + (1.0 - lb) * jax.nn.sigmoid(fr)
    k = 1.0 - f
    g = jnp.log(f)
    b = jnp.dot(tri, g, precision=HI, preferred_element_type=F32)
    b_edge = b[0:1, :] if reverse else b[C - 1:C, :]
    vb = v.astype(BF16)

    qc = (q * jnp.exp(b)).astype(BF16)
    kc = (k * jnp.exp(b_edge - b)).astype(BF16)
    o = lax.dot_general(qc, st.astype(BF16), (((1,), (1,)), ((), ())), preferred_element_type=F32)
    st_new = st * jnp.exp(b_edge) + jnp.dot(v.T.astype(BF16), kc, preferred_element_type=F32)

    a = jnp.zeros((C, C), F32)
    for li, h in enumerate(HG_LEVELS):
        rows = []
        for blk in range(C // (2 * h)):
            r = blk * 2 * h + (h if reverse else h - 1)
            rows.append(jnp.broadcast_to(b[r:r + 1, :], (2 * h, LANES)))
        bref = rows[0] if len(rows) == 1 else jnp.concatenate(rows, axis=0)
        e = jnp.exp(-jnp.abs(b - bref))
        ah = lax.dot_general((q * e).astype(BF16), (k * e).astype(BF16), (((1,), (1,)), ((), ())),
                             preferred_element_type=F32)
        a = a + ah * mask_ref[m0 + li]
    def roll_rows(x, shift):
        x3 = pltpu.roll(x.reshape(C // HG_BAND, HG_BAND, LANES), shift % HG_BAND, 1)
        return x3.reshape(C, LANES)
    ed = None
    for d in range(HG_BAND):
        if d == 0:
            p = q * k
        else:
            sh = (C - d) if reverse else d
            shf = (C - (d - 1)) % C if reverse else d - 1
            fsh = f if shf == 0 else roll_rows(f, shf)
            ed = fsh if ed is None else ed * fsh
            p = q * roll_rows(k, sh) * ed
        a = a + jnp.sum(p, axis=1, keepdims=True) * mask_ref[m0 + n_lvl + d]
    o = o + jnp.dot(a.astype(BF16), vb, preferred_element_type=F32)
    return o, st_new


def _hgrn_kernel(qf_ref, qb_ref, ff_ref, fb_ref, vf_ref, vb_ref, lb_ref, tri_ref, mask_ref,
                 of_ref, ob_ref, stf_scr, stb_scr, *, n_tiles):
    @pl.when(pl.program_id(1) == 0)
    def _():
        stf_scr[...] = jnp.zeros(stf_scr.shape, F32)
        stb_scr[...] = jnp.zeros(stb_scr.shape, F32)

    C = HG_TILE
    lbf = lb_ref[0:1, :]
    lbb = lb_ref[1:2, :]

    def body(t, carry):
        rf = pl.multiple_of(t * C, C)
        rb = pl.multiple_of((n_tiles - 1 - t) * C, C)
        o_f, st_f = _hg_tile(qf_ref[pl.ds(rf, C), :], ff_ref[pl.ds(rf, C), :], vf_ref[pl.ds(rf, C), :],
                             lbf, stf_scr[...], tri_ref[0], mask_ref, False)
        of_ref[pl.ds(rf, C), :] = o_f
        stf_scr[...] = st_f
        o_b, st_b = _hg_tile(qb_ref[pl.ds(rb, C), :], fb_ref[pl.ds(rb, C), :], vb_ref[pl.ds(rb, C), :],
                             lbb, stb_scr[...], tri_ref[1], mask_ref, True)
        ob_ref[pl.ds(rb, C), :] = o_b
        stb_scr[...] = st_b
        return carry

    lax.fori_loop(0, n_tiles, body, 0, unroll=True)


def _hgrn2(proj, lb, rows):
    L = proj.shape[0]
    rows = min(rows, L)
    nb = L // rows
    H = HG_HEADS
    nm = len(HG_LEVELS) + HG_BAND
    blk = lambda col0, rev: pl.BlockSpec(
        (rows, LANES), (lambda h, i: (nb - 1 - i, col0 + h)) if rev else (lambda h, i: (i, col0 + h)))
    return pl.pallas_call(
        functools.partial(_hgrn_kernel, n_tiles=rows // HG_TILE),
        grid=(H, nb),
        in_specs=[blk(0, False), blk(0, True), blk(H, False), blk(2 * H, True),
                  blk(3 * H, False), blk(3 * H, True),
                  pl.BlockSpec((2, LANES), lambda h, i: (0, h)),
                  pl.BlockSpec((2, HG_TILE, HG_TILE), lambda h, i: (0, 0, 0)),
                  pl.BlockSpec((2 * nm, HG_TILE, HG_TILE), lambda h, i: (0, 0, 0))],
        out_specs=[blk(0, False), blk(0, True)],
        out_shape=[jax.ShapeDtypeStruct((L, HG_WIDTH), F32)] * 2,
        scratch_shapes=[pltpu.VMEM((LANES, LANES), F32)] * 2,
        compiler_params=_cp(("parallel", "arbitrary")),
        name="hgrn2",
    )(proj, proj, proj, proj, proj, proj, lb, _hg_tri(), _hg_masks())


def _outproj_kernel(x_ref, oa_ref, of_ref, ob_ref, gr_ref, ng_ref, gate_ref, lng_ref, lnb_ref, w_ref,
                    o_ref, lhs_scr):
    lhs_scr[:, 0:DIFF_WIDTH] = oa_ref[...]
    ng = ng_ref[...]
    for h in range(HG_HEADS):
        sl = slice(h * LANES, (h + 1) * LANES)
        o = of_ref[:, sl] + ob_ref[:, sl]
        ms = jnp.mean(o * o, axis=-1, keepdims=True)
        gr = gr_ref[:, sl]
        hg = o * lax.rsqrt(ms + RMS_EPS) * ng * (gr * jax.nn.sigmoid(gr))
        lhs_scr[:, DIFF_WIDTH + h * LANES:DIFF_WIDTH + (h + 1) * LANES] = hg.astype(BF16)
    y = jnp.dot(lhs_scr[...], w_ref[...], preferred_element_type=F32)
    z = ALPHA * x_ref[...] + (1.0 + gate_ref[...]) * y
    o_ref[...] = _layer_norm_rows(z, lng_ref[...], lnb_ref[...])


def _outproj_ln(x, oa, of, ob, proj_hg, norm_g, gate, ln_g, ln_b, w, tm):
    L = x.shape[0]
    tm = min(tm, L)
    row = lambda i: (i, 0)
    vec = pl.BlockSpec((1, D_MODEL), lambda i: (0, 0))
    return pl.pallas_call(
        _outproj_kernel,
        grid=(L // tm,),
        in_specs=[pl.BlockSpec((tm, D_MODEL), row),
                  pl.BlockSpec((tm, DIFF_WIDTH), row),
                  pl.BlockSpec((tm, HG_WIDTH), row),
                  pl.BlockSpec((tm, HG_WIDTH), row),
                  pl.BlockSpec((tm, HG_WIDTH), lambda i: (i, 4)),
                  pl.BlockSpec((1, LANES), lambda i: (0, 0)),
                  vec, vec, vec,
                  pl.BlockSpec((D_MODEL, D_MODEL), lambda i: (0, 0))],
        out_specs=pl.BlockSpec((tm, D_MODEL), row),
        out_shape=jax.ShapeDtypeStruct((L, D_MODEL), F32),
        scratch_shapes=[pltpu.VMEM((tm, D_MODEL), BF16)],
        compiler_params=_cp(("parallel",)),
        name="outproj_ln",
    )(x, oa, of, ob, proj_hg, norm_g, gate, ln_g, ln_b, w)


def _mlp_kernel(x_ref, sc_ref, sh_ref, gate_ref, lng_ref, lnb_ref, w1_ref, w2_ref, o_ref, h_scr, acc_scr):
    j = pl.program_id(1)

    @pl.when(j == 0)
    def _():
        h_scr[...] = (x_ref[...] * (1.0 + sc_ref[...]) + sh_ref[...]).astype(BF16)
        acc_scr[...] = jnp.zeros(acc_scr.shape, F32)

    a = jnp.maximum(jnp.dot(h_scr[...], w1_ref[...], preferred_element_type=F32), 0.0)
    acc_scr[...] += jnp.dot((a * a).astype(BF16), w2_ref[...], preferred_element_type=F32)

    @pl.when(j == pl.num_programs(1) - 1)
    def _():
        z = ALPHA * x_ref[...] + (1.0 + gate_ref[...]) * acc_scr[...]
        o_ref[...] = _layer_norm_rows(z, lng_ref[...], lnb_ref[...])


def _mlp_ln(x, scale, shift, gate, ln_g, ln_b, w1, w2, tm, tf):
    L = x.shape[0]
    tm = min(tm, L)
    vec = pl.BlockSpec((1, D_MODEL), lambda i, j: (0, 0))
    return pl.pallas_call(
        _mlp_kernel,
        grid=(L // tm, D_FF // tf),
        in_specs=[pl.BlockSpec((tm, D_MODEL), lambda i, j: (i, 0)),
                  vec, vec, vec, vec, vec,
                  pl.BlockSpec((D_MODEL, tf), lambda i, j: (0, j)),
                  pl.BlockSpec((tf, D_MODEL), lambda i, j: (j, 0))],
        out_specs=pl.BlockSpec((tm, D_MODEL), lambda i, j: (i, 0)),
        out_shape=jax.ShapeDtypeStruct((L, D_MODEL), F32),
        scratch_shapes=[pltpu.VMEM((tm, D_MODEL), BF16), pltpu.VMEM((tm, D_MODEL), F32)],
        compiler_params=_cp(("parallel", "arbitrary")),
        name="mlp_ln",
    )(x, scale, shift, gate, ln_g, ln_b, w1, w2)


def _s5_operators(lam_re, lam_im, log_dt, b_re, b_im, c_re, c_im):
    T, G, P, Hc, NB, GPB = S5_T, S5_GROUPS, S5_STATE, S5_GROUP, S5_BLOCKS, S5_GPB
    dt = jnp.exp(log_dt)[..., None]
    zr, zi = lam_re * dt, lam_im * dt
    mag = jnp.exp(zr)
    ab_re, ab_im = mag * jnp.cos(zi), mag * jnp.sin(zi)
    den = lam_re * lam_re + lam_im * lam_im
    coef_re = ((ab_re - 1.0) * lam_re + ab_im * lam_im) / den
    coef_im = (ab_im * lam_re - (ab_re - 1.0) * lam_im) / den
    bb_re = coef_re[..., None] * b_re - coef_im[..., None] * b_im
    bb_im = coef_re[..., None] * b_im + coef_im[..., None] * b_re

    def apow(k):
        k = jnp.asarray(k, F32)[:, None, None, None]
        m = jnp.exp(zr[None] * k)
        return m * jnp.cos(zi[None] * k), m * jnp.sin(zi[None] * k)

    pr, pi = apow(np.arange(T))
    ca_re = c_re[None] * pr[:, :, :, None, :] - c_im[None] * pi[:, :, :, None, :]
    ca_im = c_re[None] * pi[:, :, :, None, :] + c_im[None] * pr[:, :, :, None, :]
    bt_re, bt_im = jnp.swapaxes(bb_re, 2, 3), jnp.swapaxes(bb_im, 2, 3)
    lagk = jnp.sum(ca_re[:, :, :, :, None, :] * bt_re[None, :, :, None, :, :]
                   - ca_im[:, :, :, :, None, :] * bt_im[None, :, :, None, :, :], axis=-1)
    kf, kb = lagk[:, 0], lagk[:, 1]
    kall = jnp.concatenate([kb[:0:-1], (kf[0] + kb[0])[None], kf[1:]], axis=0)
    kc = jnp.transpose(kall.reshape(2 * T - 1, NB, GPB, Hc, Hc), (1, 0, 4, 2, 3))
    kc = kc.reshape(NB, 2 * T - 1, Hc, LANES)

    def gp_lanes(v):
        return jnp.transpose(v.reshape(2, NB, GPB, Hc, P), (1, 0, 3, 2, 4)).reshape(NB, 2, Hc, GPB * P)
    bt = jnp.stack([gp_lanes(bt_re), gp_lanes(bt_im)], axis=1)
    def e_tab(v):
        v = jnp.stack([v[::-1, 0], v[:, 1]], axis=1).reshape(T, 2, NB, GPB * P)
        return jnp.transpose(v, (2, 0, 1, 3))
    et = jnp.stack([e_tab(pr), e_tab(pi)], axis=2)[:, :, :, :, None, :]

    ct = jnp.stack([gp_lanes(c_re), gp_lanes(c_im)], axis=1)
    cr, ci = apow(np.arange(1, T + 1))
    def ec_tab(v):
        v = jnp.stack([v[:, 0], v[::-1, 1]], axis=1).reshape(T, 2, NB, GPB * P)
        return jnp.transpose(v, (2, 0, 1, 3))
    ec = jnp.stack([ec_tab(cr), ec_tab(ci)], axis=2)[:, :, :, :, None, :]
    return tuple(_s5_expand(kc, bt, et, ct, ec)) + ((zr, zi),)


def _s5_expand_kernel(kc_ref, bt_ref, et_ref, ct_ref, ec_ref, wt_ref, ws_ref, wc_ref):
    s = pl.program_id(1)
    n_q = S5_GPB * S5_STATE // LANES
    gpq = S5_GPB // n_q
    row = lax.broadcasted_iota(jnp.int32, (LANES, LANES), 0)
    col = lax.broadcasted_iota(jnp.int32, (LANES, LANES), 1)
    over_groups = lambda a: jnp.concatenate([a] * S5_GPB, axis=0)

    same_group = (row // S5_GROUP) == (col // S5_GROUP)
    for t in range(S5_T):
        k = over_groups(kc_ref[0, t - s + S5_T - 1])
        wt_ref[0, :, t * LANES:(t + 1) * LANES] = jnp.where(same_group, k, 0.0).astype(BF16)

    for d in range(2):
        for q in range(n_q):
            sl = slice(q * LANES, (q + 1) * LANES)
            own = (row // S5_GROUP) == (gpq * q + col // S5_STATE)
            for m_ref, p_ref, o_ref, im_sign in ((bt_ref, et_ref, ws_ref, 1.0), (ct_ref, ec_ref, wc_ref, -1.0)):
                m_re, m_im = m_ref[0, 0, d, :, sl], m_ref[0, 1, d, :, sl]
                p_re, p_im = p_ref[0, 0, 0, d, :, sl], p_ref[0, 0, 1, d, :, sl]
                vals = (p_re * m_re - p_im * m_im, im_sign * (p_re * m_im + p_im * m_re))
                for ri in range(2):
                    c0 = ((d * n_q + q) * 2 + ri) * LANES
                    o_ref[0, :, c0:c0 + LANES] = jnp.where(own, over_groups(vals[ri]), 0.0).astype(BF16)


def _s5_expand(kc, bt, et, ct, ec):
    NB, T, Hc, P = S5_BLOCKS, S5_T, S5_GROUP, S5_STATE
    slab = pl.BlockSpec((1, LANES, S5_ROW), lambda b, s: (b, s, 0))
    mat = pl.BlockSpec((1, 2, 2, Hc, S5_GPB * P), lambda b, s: (b, 0, 0, 0, 0))
    powers = pl.BlockSpec((1, 1, 2, 2, 1, S5_GPB * P), lambda b, s: (b, s, 0, 0, 0, 0))
    return pl.pallas_call(
        _s5_expand_kernel,
        grid=(NB, T),
        in_specs=[pl.BlockSpec((1, 2 * T - 1, Hc, LANES), lambda b, s: (b, 0, 0, 0)), mat, powers, mat, powers],
        out_specs=[slab, slab, slab],
        out_shape=[jax.ShapeDtypeStruct((NB, S5_ROW, S5_ROW), BF16)] * 3,
        compiler_params=_cp(("parallel", "arbitrary")),
        name="s5_expand",
    )(kc, bt, et, ct, ec)


def _s5_state_lanes(re, im):
    n = re.shape[0]
    x = jnp.stack([re, im], axis=0)
    x = x.reshape(2, n, 2, S5_BLOCKS, S5_GPB * S5_STATE // LANES, LANES)
    x = jnp.transpose(x, (2, 1, 3, 4, 0, 5))
    return x.reshape(2, n, S5_BLOCKS * S5_SW)


def _s5_scan_powers(zr, zi, n_steps):
    def power(k):
        k = jnp.asarray(k, F32)[:, :, None, None]
        m = jnp.exp(zr[None] * k)
        return m * jnp.cos(zi[None] * k), m * jnp.sin(zi[None] * k)
    k_dbl = S5_T * (2.0 ** np.arange(n_steps))
    j = np.arange(SUBLANES)
    return (_s5_state_lanes(*power(np.stack([k_dbl, k_dbl], axis=1))),
            _s5_state_lanes(*power(S5_T * np.stack([j, SUBLANES - 1 - j], axis=1))))


def _s5_fold(x_ref, sc, sh, tc):
    parts = [x_ref[pl.ds(t, tc, stride=S5_T), :] * (1.0 + sc) + sh for t in range(S5_T)]
    return parts


def _s5_state_kernel(x_ref, sc_ref, sh_ref, w_ref, zf_ref, zb_ref, *, tc):
    parts = _s5_fold(x_ref, sc_ref[...], sh_ref[...], tc)
    u = jnp.concatenate([p.astype(BF16) for p in parts], axis=1)
    z = jnp.dot(u, w_ref[0], preferred_element_type=F32)
    zf_ref[...] = z[:, :S5_SW]
    zb_ref[...] = z[:, S5_SW:]


def _s5_scan_kernel(zf_ref, zb_ref, af_ref, ab_ref, pf_ref, pb_ref, sf_ref, sb_ref, loc_scr,
                    *, n_rows, n_steps):
    n_blk = n_rows // SUBLANES
    n_local = int(math.log2(SUBLANES))
    sub = lax.broadcasted_iota(jnp.int32, (n_blk, SUBLANES, LANES), 1)
    blk = lax.broadcasted_iota(jnp.int32, (n_blk, LANES), 0)

    def cmul_add(re, im, ar, ai, rs, is_):
        return re + ar * rs - ai * is_, im + ar * is_ + ai * rs

    def scan(z_ref, a_ref, p_ref, reverse):
        re = z_ref[:, 0:LANES].reshape(n_blk, SUBLANES, LANES)
        im = z_ref[:, LANES:2 * LANES].reshape(n_blk, SUBLANES, LANES)

        def in_block(v, sh):
            if reverse:
                return jnp.where(sub < SUBLANES - sh, pltpu.roll(v, SUBLANES - sh, 1), 0.0)
            return jnp.where(sub >= sh, pltpu.roll(v, sh, 1), 0.0)

        def across(v, sh):
            if reverse:
                return jnp.where(blk < n_blk - sh, pltpu.roll(v, n_blk - sh, 0), 0.0)
            return jnp.where(blk >= sh, pltpu.roll(v, sh, 0), 0.0)

        for k in range(n_local):
            ar, ai = a_ref[0, k:k + 1, 0:LANES], a_ref[0, k:k + 1, LANES:2 * LANES]
            re, im = cmul_add(re, im, ar, ai, in_block(re, 2 ** k), in_block(im, 2 ** k))
        edge = 0 if reverse else SUBLANES - 1
        loc_scr[0] = re.reshape(n_rows, LANES)
        loc_scr[1] = im.reshape(n_rows, LANES)
        cr = loc_scr[0, pl.ds(edge, n_blk, stride=SUBLANES), :]
        ci = loc_scr[1, pl.ds(edge, n_blk, stride=SUBLANES), :]
        for k in range(n_local, n_steps):
            ar, ai = a_ref[0, k:k + 1, 0:LANES], a_ref[0, k:k + 1, LANES:2 * LANES]
            sh = 2 ** (k - n_local)
            cr, ci = cmul_add(cr, ci, ar, ai, across(cr, sh), across(ci, sh))
        cr, ci = across(cr, 1), across(ci, 1)
        pr = p_ref[0, :, 0:LANES][None]
        pi = p_ref[0, :, LANES:2 * LANES][None]
        cr3 = jnp.broadcast_to(cr[:, None, :], (n_blk, SUBLANES, LANES))
        ci3 = jnp.broadcast_to(ci[:, None, :], (n_blk, SUBLANES, LANES))
        ore, oim = cmul_add(in_block(re, 1), in_block(im, 1), pr, pi, cr3, ci3)
        return ore.reshape(n_rows, LANES), oim.reshape(n_rows, LANES)

    re, im = scan(zf_ref, af_ref, pf_ref, False)
    sf_ref[:, 0:LANES] = re.astype(BF16)
    sf_ref[:, LANES:2 * LANES] = im.astype(BF16)
    re, im = scan(zb_ref, ab_ref, pb_ref, True)
    sb_ref[:, 0:LANES] = re.astype(BF16)
    sb_ref[:, LANES:2 * LANES] = im.astype(BF16)


def _s5_out_kernel(x_ref, sc_ref, sh_ref, d_ref, sf_ref, sb_ref, wt_ref, wc_ref, o_ref, *, tc, n_half):
    j = pl.program_id(2)
    sc, sh, d = sc_ref[...], sh_ref[...], d_ref[...]
    parts = _s5_fold(x_ref, sc, sh, tc)
    u = jnp.concatenate([p.astype(BF16) for p in parts], axis=1)
    s = jnp.concatenate([sf_ref[...], sb_ref[...]], axis=1)
    y = (jnp.dot(u, wt_ref[0], preferred_element_type=F32)
         + lax.dot_general(s, wc_ref[0], (((1,), (1,)), ((), ())),
                           preferred_element_type=F32))
    tph = S5_T // n_half
    for jj in range(n_half):
        @pl.when(j == jj)
        def _():
            for tl in range(tph):
                t = jj * tph + tl
                o_ref[pl.ds(t, tc, stride=S5_T), :] = y[:, tl * LANES:(tl + 1) * LANES] + d * parts[t]


def _s5_mixer_core(x, scale, shift, d_skip, ops, tc):
    w_toep, w_state, w_carry, (zr, zi) = ops
    L = x.shape[0]
    n_rows = L // S5_T
    NB = S5_BLOCKS
    n_steps = int(math.log2(n_rows))
    assert 2 ** n_steps == n_rows
    tc = min(tc, n_rows)
    vecb = pl.BlockSpec((1, LANES), lambda b, i: (0, b))

    zf, zb = pl.pallas_call(
        functools.partial(_s5_state_kernel, tc=tc),
        grid=(NB, n_rows // tc),
        in_specs=[pl.BlockSpec((tc * S5_T, LANES), lambda b, i: (i, b)), vecb, vecb,
                  pl.BlockSpec((1, S5_ROW, 2 * S5_SW), lambda b, i: (b, 0, 0))],
        out_specs=[pl.BlockSpec((tc, S5_SW), lambda b, i: (i, b))] * 2,
        out_shape=[jax.ShapeDtypeStruct((n_rows, NB * S5_SW), F32)] * 2,
        compiler_params=_cp(("parallel", "parallel")),
        name="s5_state",
    )(x, scale, shift, w_state)

    pw, pj = _s5_scan_powers(zr, zi, n_steps)
    cw = 2 * LANES
    col = lambda n: (0, n)
    sf, sb = pl.pallas_call(
        functools.partial(_s5_scan_kernel, n_rows=n_rows, n_steps=n_steps),
        grid=(NB * S5_SW // cw,),
        in_specs=[pl.BlockSpec((n_rows, cw), col),
                  pl.BlockSpec((n_rows, cw), col),
                  pl.BlockSpec((1, n_steps, cw), lambda n: (0, 0, n)),
                  pl.BlockSpec((1, n_steps, cw), lambda n: (1, 0, n)),
                  pl.BlockSpec((1, SUBLANES, cw), lambda n: (0, 0, n)),
                  pl.BlockSpec((1, SUBLANES, cw), lambda n: (1, 0, n))],
        out_specs=[pl.BlockSpec((n_rows, cw), col)] * 2,
        out_shape=[jax.ShapeDtypeStruct((n_rows, NB * S5_SW), BF16)] * 2,
        scratch_shapes=[pltpu.VMEM((2, n_rows, LANES), F32)],
        compiler_params=_cp(("parallel",)),
        name="s5_scan",
    )(zf, zb, pw, pw, pj, pj)

    n_half = 2
    nh = S5_ROW // n_half
    vec3 = pl.BlockSpec((1, LANES), lambda b, i, j: (0, b))
    return pl.pallas_call(
        functools.partial(_s5_out_kernel, tc=tc, n_half=n_half),
        grid=(NB, n_rows // tc, n_half),
        in_specs=[pl.BlockSpec((tc * S5_T, LANES), lambda b, i, j: (i, b)), vec3, vec3, vec3,
                  pl.BlockSpec((tc, S5_SW), lambda b, i, j: (i, b)),
                  pl.BlockSpec((tc, S5_SW), lambda b, i, j: (i, b)),
                  pl.BlockSpec((1, S5_ROW, nh), lambda b, i, j: (b, 0, j)),
                  pl.BlockSpec((1, nh, 2 * S5_SW), lambda b, i, j: (b, j, 0))],
        out_specs=pl.BlockSpec((tc * S5_T, LANES), lambda b, i, j: (i, b)),
        out_shape=jax.ShapeDtypeStruct((L, D_MODEL), F32),
        compiler_params=_cp(("parallel", "parallel", "arbitrary")),
        name="s5_out",
    )(x, scale, shift, d_skip, sf, sb, w_toep, w_carry)


def _glu_kernel(x_ref, y_ref, gate_ref, lng_ref, lnb_ref, wa_ref, wg_ref, o_ref, lhs_scr, mix_scr, *, tn):
    j = pl.program_id(1)

    @pl.when(j == 0)
    def _():
        lhs_scr[...] = jax.nn.gelu(y_ref[...]).astype(BF16)

    a = jnp.dot(lhs_scr[...], wa_ref[...], preferred_element_type=F32)
    g = jnp.dot(lhs_scr[...], wg_ref[...], preferred_element_type=F32)
    mix = a * jax.nn.sigmoid(g)
    for jj in range(D_MODEL // tn):
        @pl.when(j == jj)
        def _():
            mix_scr[:, jj * tn:(jj + 1) * tn] = mix

    @pl.when(j == pl.num_programs(1) - 1)
    def _():
        z = ALPHA * x_ref[...] + (1.0 + gate_ref[...]) * mix_scr[...]
        o_ref[...] = _layer_norm_rows(z, lng_ref[...], lnb_ref[...])


def _glu_ln(x, y, gate, ln_g, ln_b, w_glu, tm, tn):
    L = x.shape[0]
    tm = min(tm, L)
    nj = D_MODEL // tn
    vec = pl.BlockSpec((1, D_MODEL), lambda i, j: (0, 0))
    return pl.pallas_call(
        functools.partial(_glu_kernel, tn=tn),
        grid=(L // tm, nj),
        in_specs=[pl.BlockSpec((tm, D_MODEL), lambda i, j: (i, 0)),
                  pl.BlockSpec((tm, D_MODEL), lambda i, j: (i, 0)),
                  vec, vec, vec,
                  pl.BlockSpec((D_MODEL, tn), lambda i, j: (0, j)),
                  pl.BlockSpec((D_MODEL, tn), lambda i, j: (0, nj + j))],
        out_specs=pl.BlockSpec((tm, D_MODEL), lambda i, j: (i, 0)),
        out_shape=jax.ShapeDtypeStruct((L, D_MODEL), F32),
        scratch_shapes=[pltpu.VMEM((tm, D_MODEL), BF16), pltpu.VMEM((tm, D_MODEL), F32)],
        compiler_params=_cp(("parallel", "arbitrary")),
        name="glu_ln",
    )(x, y, gate, ln_g, ln_b, w_glu, w_glu)


def kernel(x, c, ada_w, ada_b, ln_g, ln_b, mix_w_in, mix_w_out, diff_lambda, diff_subln_g, hg_lower_bound,
           hg_norm_g, s5_lambda_re, s5_lambda_im, s5_log_dt, s5_b_re, s5_b_im, s5_c_re, s5_c_im, s5_d,
           s5_w_glu, mlp_w1, mlp_w2):
    B, L, D = x.shape
    assert B == 1 and D == D_MODEL
    xs = x.reshape(L, D)
    mod = _adaln(c, ada_w, ada_b)
    mods = lambda a: (mod[a, :, 0:D], mod[a, :, D:2 * D], mod[a, :, 2 * D:3 * D])
    lb_table = jnp.cumsum(jax.nn.softmax(hg_lower_bound.astype(F32), axis=1), axis=1)
    rot_tabs = _rotary_tables(L)

    for layer in range(DEPTH):
        shift, scale, gate = mods(2 * layer)
        lng, lnb = ln_g[layer, 0][None], ln_b[layer, 0][None]
        if layer % 2 == 0:
            e = layer // 2
            lam_init = 0.8 - 0.6 * math.exp(-0.3 * layer)
            w_in = _to_bf16(mix_w_in, e)
            qkv = _inproj(xs, scale, shift, w_in, 0, 3 * DIFF_WIDTH, rot_tabs, 2, BF16, INPROJ_TM, INPROJ_TN)
            proj_hg = _inproj(xs, scale, shift, w_in, 3 * DIFF_WIDTH, 5 * HG_WIDTH, rot_tabs, 0, F32,
                              INPROJ_TM, INPROJ_TN)
            lf = diff_lambda[e].astype(F32)
            lam = jnp.exp(jnp.sum(lf[0] * lf[1])) - jnp.exp(jnp.sum(lf[2] * lf[3])) + lam_init
            oa = _diff_attention(qkv, lam.reshape(1), diff_subln_g[e][None], lam_init, ATTN_TQ, ATTN_TK)
            of, ob = _hgrn2(proj_hg, lb_table[:, layer], HG_ROWS)
            xs = _outproj_ln(xs, oa, of, ob, proj_hg, hg_norm_g[e][None], gate, lng, lnb,
                             _to_bf16(mix_w_out, e), OUTPROJ_TM)
        else:
            o = layer // 2
            ops = _s5_operators(s5_lambda_re[o], s5_lambda_im[o], s5_log_dt[o], s5_b_re[o], s5_b_im[o],
                                s5_c_re[o], s5_c_im[o])
            y = _s5_mixer_core(xs, scale, shift, s5_d[o][None], ops, S5_TC)
            xs = _glu_ln(xs, y, gate, lng, lnb, _to_bf16(s5_w_glu, o), GLU_TM, GLU_TN)
        shift, scale, gate = mods(2 * layer + 1)
        xs = _mlp_ln(xs, scale, shift, gate, ln_g[layer, 1][None], ln_b[layer, 1][None],
                     _to_bf16(mlp_w1, layer), _to_bf16(mlp_w2, layer), MLP_TM, MLP_TF)
    return xs.reshape(B, L, D)
```

```python
import functools
import math

import numpy as np
import jax
import jax.numpy as jnp
from jax import lax
from jax.experimental import pallas as pl
from jax.experimental.pallas import tpu as pltpu

F32 = jnp.float32
BF16 = jnp.bfloat16

D_MODEL = 2048
DEPTH = 2
LANES = 128
SUBLANES = 8

DIFF_HEADS = 8
DIFF_QK_DIM = 64
DIFF_WIDTH = 1024
ROT_DIM = DIFF_QK_DIM // 4
ROPE_THETA = 500000.0
ATTN_ROWS = 128

HG_HEADS = 8
HG_WIDTH = 1024
HG_TILE = 128
HG_BAND = SUBLANES
HG_LEVELS = (64, 32, 16, 8)

S5_GROUP = 16
S5_GROUPS = D_MODEL // S5_GROUP
S5_STATE = 64
S5_T = 16
S5_GPB = LANES // S5_GROUP
S5_BLOCKS = D_MODEL // LANES
S5_ROW = S5_T * LANES
S5_SW = S5_GPB * S5_STATE * 2

D_FF = 4 * D_MODEL
ALPHA = (2 * DEPTH) ** 0.25
LN_EPS = 1e-5
RMS_EPS = 1e-6
HI = lax.Precision.HIGHEST

VMEM_LIMIT = 56 * 1024 * 1024
CAST_BLOCK_ELEMS = 2 * 1024 * 1024

ADALN_TN = 768
INPROJ_TM, INPROJ_TN = 1024, 1024
ATTN_TQ, ATTN_TK = 2048, 1024
HG_ROWS = 1024
OUTPROJ_TM = 256
MLP_TM, MLP_TF = 512, 1024
S5_TC = 512
GLU_TM, GLU_TN = 512, 1024


def _cp(sem):
    return pltpu.CompilerParams(dimension_semantics=sem, vmem_limit_bytes=VMEM_LIMIT)


def _bf16_kernel(w_ref, o_ref):
    o_ref[...] = w_ref[0].astype(BF16)


def _to_bf16(w, layer):
    _, r, c = w.shape
    tr = max(SUBLANES, CAST_BLOCK_ELEMS // c)
    assert r % tr == 0
    return pl.pallas_call(
        _bf16_kernel, grid=(r // tr,),
        in_specs=[pl.BlockSpec((1, tr, c), lambda i: (layer, i, 0))],
        out_specs=pl.BlockSpec((tr, c), lambda i: (i, 0)),
        out_shape=jax.ShapeDtypeStruct((r, c), BF16), compiler_params=_cp(("parallel",)), name="to_bf16",
    )(w)


def _layer_norm_rows(z, g, b):
    mu = jnp.mean(z, axis=-1, keepdims=True)
    zc = z - mu
    var = jnp.mean(zc * zc, axis=-1, keepdims=True)
    return zc * lax.rsqrt(var + LN_EPS) * g + b


def _adaln_kernel(c_ref, w_ref, b_ref, o_ref):
    c = c_ref[...]
    ca = c * jax.nn.sigmoid(c)
    w = w_ref[0]
    reps = w.shape[1] // LANES
    cb = jnp.concatenate([ca] * reps, axis=1)
    o_ref[0] = jnp.sum(w * cb, axis=0, keepdims=True) + b_ref[0]


def _adaln(c, ada_w, ada_b):
    n = 3 * D_MODEL
    tn = ADALN_TN
    w = ada_w.reshape(2 * DEPTH, D_MODEL, n)
    b = ada_b.reshape(2 * DEPTH, 1, n)
    c_rep = jnp.broadcast_to(c.reshape(D_MODEL, 1), (D_MODEL, LANES))
    return pl.pallas_call(
        _adaln_kernel,
        grid=(2 * DEPTH, n // tn),
        in_specs=[pl.BlockSpec((D_MODEL, LANES), lambda a, j: (0, 0)),
                  pl.BlockSpec((1, D_MODEL, tn), lambda a, j: (a, 0, j)),
                  pl.BlockSpec((1, 1, tn), lambda a, j: (a, 0, j))],
        out_specs=pl.BlockSpec((1, 1, tn), lambda a, j: (a, 0, j)),
        out_shape=jax.ShapeDtypeStruct((2 * DEPTH, 1, n), F32),
        compiler_params=_cp(("parallel", "parallel")),
        name="adaln",
    )(c_rep, w, b)


def _inproj_kernel(x_ref, sc_ref, sh_ref, w_ref, cos_ref, sin_ref, o_ref, h_scr, *, n_rot):
    j = pl.program_id(1)

    @pl.when(j == 0)
    def _():
        h_scr[...] = (x_ref[...] * (1.0 + sc_ref[...]) + sh_ref[...]).astype(BF16)

    y = jnp.dot(h_scr[...], w_ref[...], preferred_element_type=F32)

    @pl.when(j < n_rot)
    def _():
        half = ROT_DIM // 2
        cs, sn = cos_ref[...], sin_ref[...]
        pos = lax.broadcasted_iota(jnp.int32, sn.shape, 1) % DIFF_QK_DIM
        sa = jnp.where(pos >= half, sn, 0.0)
        sb = jnp.where(pos < half, -sn, 0.0)
        for g in range(y.shape[1] // LANES):
            yg = y[:, g * LANES:(g + 1) * LANES]
            rot = (yg * cs + pltpu.roll(yg, ROT_DIM // 2, 1) * sa
                   + pltpu.roll(yg, LANES - ROT_DIM // 2, 1) * sb)
            o_ref[:, g * LANES:(g + 1) * LANES] = rot.astype(o_ref.dtype)

    @pl.when(j >= n_rot)
    def _():
        o_ref[...] = y.astype(o_ref.dtype)


def _inproj(x, scale, shift, w, col0, n, rot_tabs, n_rot, out_dtype, tm, tn):
    L = x.shape[0]
    tm = min(tm, L)
    j0 = col0 // tn
    return pl.pallas_call(
        functools.partial(_inproj_kernel, n_rot=n_rot),
        grid=(L // tm, n // tn),
        in_specs=[pl.BlockSpec((tm, D_MODEL), lambda i, j: (i, 0)),
                  pl.BlockSpec((1, D_MODEL), lambda i, j: (0, 0)),
                  pl.BlockSpec((1, D_MODEL), lambda i, j: (0, 0)),
                  pl.BlockSpec((D_MODEL, tn), lambda i, j: (0, j0 + j)),
                  pl.BlockSpec((tm, LANES), lambda i, j: (i, 0)),
                  pl.BlockSpec((tm, LANES), lambda i, j: (i, 0))],
        out_specs=pl.BlockSpec((tm, tn), lambda i, j: (i, j)),
        out_shape=jax.ShapeDtypeStruct((L, n), out_dtype),
        scratch_shapes=[pltpu.VMEM((tm, D_MODEL), BF16)],
        compiler_params=_cp(("parallel", "arbitrary")),
        name="inproj",
    )(x, scale, shift, w, *rot_tabs)


def _rotary_tables(L):
    half = ROT_DIM // 2
    pos = jnp.arange(L, dtype=F32)
    inv_freq = 1.0 / (ROPE_THETA ** (jnp.arange(0, ROT_DIM, 2, dtype=F32) / ROT_DIM))
    lane = np.arange(LANES) % DIFF_QK_DIM
    rotated = jnp.asarray(lane < ROT_DIM)[None, :]
    ang = pos[:, None] * inv_freq[lane % half][None, :]
    return jnp.where(rotated, jnp.cos(ang), 1.0), jnp.where(rotated, jnp.sin(ang), 0.0)


def _attn_kernel(lam_ref, q_ref, k_ref, v_ref, g_ref, o_ref, q2_scr, m_scr, acc_scr, s_scr,
                 *, tq, tk, out_scale):
    n_rows = 2 * tq
    n_chunks = n_rows // ATTN_ROWS
    n_kv = k_ref.shape[0] // tk
    q = q_ref[...]
    lane = lax.broadcasted_iota(jnp.int32, q.shape, 1)
    qs = q * (DIFF_QK_DIM ** -0.5)
    q2_scr[0:tq, :] = jnp.where(lane < DIFF_QK_DIM, qs, 0).astype(BF16)
    q2_scr[tq:n_rows, :] = jnp.where(lane >= DIFF_QK_DIM, qs, 0).astype(BF16)
    m_scr[...] = jnp.full(m_scr.shape, -jnp.inf, F32)
    acc_scr[...] = jnp.zeros(acc_scr.shape, F32)
    ones = jnp.ones((tk, LANES), BF16)
    n_col = tk // LANES

    def scores(r, kc):
        rs = slice(r * ATTN_ROWS, (r + 1) * ATTN_ROWS)
        return lax.dot_general(q2_scr[rs, :], kc, (((1,), (1,)), ((), ())), preferred_element_type=F32)

    kc0 = k_ref[0:tk, :]
    for r in range(n_chunks):
        s_scr[r * ATTN_ROWS:(r + 1) * ATTN_ROWS, :] = scores(r, kc0)

    def step(j, prefetch):
        c0 = pl.multiple_of(j * tk, tk)
        if prefetch:
            k_next = k_ref[pl.ds(pl.multiple_of((j + 1) * tk, tk), tk), :]
        v1 = jnp.concatenate([v_ref[pl.ds(c0, tk), :], ones], axis=1)
        for r in range(n_chunks):
            rs = slice(r * ATTN_ROWS, (r + 1) * ATTN_ROWS)
            s = s_scr[rs, :]
            if prefetch:
                s_scr[rs, :] = scores(r, k_next)
            mx = s[:, 0:LANES]
            for c in range(1, n_col):
                mx = jnp.maximum(mx, s[:, c * LANES:(c + 1) * LANES])
            m_prev = m_scr[rs, :]
            m_new = jnp.maximum(m_prev, jnp.max(mx, axis=1, keepdims=True))
            alpha = jnp.exp(m_prev - m_new)
            p = jnp.exp(s - jnp.concatenate([m_new] * n_col, axis=1)).astype(BF16)
            pv = jnp.dot(p, v1, preferred_element_type=F32)
            acc_scr[rs, :] = jnp.concatenate([alpha, alpha], axis=1) * acc_scr[rs, :] + pv
            m_scr[rs, :] = m_new

    def body(j, carry):
        step(j, True)
        return carry

    lax.fori_loop(0, n_kv - 1, body, 0)
    step(n_kv - 1, False)

    acc = acc_scr[...]
    o2 = acc[:, 0:LANES] / acc[:, LANES:2 * LANES]
    o = o2[0:tq, :] - lam_ref[0] * o2[tq:n_rows, :]
    ms = jnp.mean(o * o, axis=-1, keepdims=True)
    o_ref[...] = (o * lax.rsqrt(ms + RMS_EPS) * g_ref[...] * out_scale).astype(o_ref.dtype)


def _diff_attention(qkv, lam, subln_g, lam_init, tq, tk):
    L = qkv.shape[0]
    tq, tk = min(tq, L), min(tk, L)
    H = DIFF_HEADS
    return pl.pallas_call(
        functools.partial(_attn_kernel, tq=tq, tk=tk, out_scale=1.0 - lam_init),
        grid=(H, L // tq),
        in_specs=[pl.BlockSpec(memory_space=pltpu.SMEM),
                  pl.BlockSpec((tq, LANES), lambda h, i: (i, h)),
                  pl.BlockSpec((L, LANES), lambda h, i: (0, H + h)),
                  pl.BlockSpec((L, LANES), lambda h, i: (0, 2 * H + h)),
                  pl.BlockSpec((1, LANES), lambda h, i: (0, 0))],
        out_specs=pl.BlockSpec((tq, LANES), lambda h, i: (i, h)),
        out_shape=jax.ShapeDtypeStruct((L, DIFF_WIDTH), BF16),
        scratch_shapes=[pltpu.VMEM((2 * tq, LANES), BF16),
                        pltpu.VMEM((2 * tq, LANES), F32),
                        pltpu.VMEM((2 * tq, 2 * LANES), F32),
                        pltpu.VMEM((2 * tq, tk), F32)],
        compiler_params=_cp(("parallel", "parallel")),
        name="diff_attn",
    )(lam, qkv, qkv, qkv, subln_g)


def _hg_masks():
    C = HG_TILE
    t = np.arange(C)[:, None]
    s = np.arange(C)[None, :]
    out = []
    for reverse in (False, True):
        for h in HG_LEVELS:
            same = (t // (2 * h)) == (s // (2 * h))
            t_late = (t % (2 * h)) >= h
            s_late = (s % (2 * h)) >= h
            m = same & (~t_late & s_late if reverse else t_late & ~s_late)
            out.append(m)
        for d in range(HG_BAND):
            same = (t // HG_BAND) == (s // HG_BAND)
            out.append(same & ((s == t + d) if reverse else (s == t - d)))
    return jnp.asarray(np.stack(out).astype(np.float32))


def _hg_tri():
    C = HG_TILE
    t = np.arange(C)[:, None]
    r = np.arange(C)[None, :]
    return jnp.asarray(np.stack([(r <= t), (r >= t)]).astype(np.float32))


def _hg_tile(qr, fr, v, lb, st, tri, mask_ref, reverse):
    C = HG_TILE
    n_lvl = len(HG_LEVELS)
    m0 = (n_lvl + HG_BAND) if reverse else 0
    q = qr * jax.nn.sigmoid(qr)
    f = lb + (1.0 - lb) * jax.nn.sigmoid(fr)
    k = 1.0 - f
    g = jnp.log(f)
    b = jnp.dot(tri, g, precision=HI, preferred_element_type=F32)
    b_edge = b[0:1, :] if reverse else b[C - 1:C, :]
    vb = v.astype(BF16)

    qc = (q * jnp.exp(b)).astype(BF16)
    kc = (k * jnp.exp(b_edge - b)).astype(BF16)
    o = lax.dot_general(qc, st.astype(BF16), (((1,), (1,)), ((), ())), preferred_element_type=F32)
    st_new = st * jnp.exp(b_edge) + jnp.dot(v.T.astype(BF16), kc, preferred_element_type=F32)

    a = jnp.zeros((C, C), F32)
    for li, h in enumerate(HG_LEVELS):
        rows = []
        for blk in range(C // (2 * h)):
            r = blk * 2 * h + (h if reverse else h - 1)
            rows.append(jnp.broadcast_to(b[r:r + 1, :], (2 * h, LANES)))
        bref = rows[0] if len(rows) == 1 else jnp.concatenate(rows, axis=0)
        e = jnp.exp(-jnp.abs(b - bref))
        ah = lax.dot_general((q * e).astype(BF16), (k * e).astype(BF16), (((1,), (1,)), ((), ())),
                             preferred_element_type=F32)
        a = a + ah * mask_ref[m0 + li]
    def roll_rows(x, shift):
        x3 = pltpu.roll(x.reshape(C // HG_BAND, HG_BAND, LANES), shift % HG_BAND, 1)
        return x3.reshape(C, LANES)
    ed = None
    for d in range(HG_BAND):
        if d == 0:
            p = q * k
        else:
            sh = (C - d) if reverse else d
            shf = (C - (d - 1)) % C if reverse else d - 1
            fsh = f if shf == 0 else roll_rows(f, shf)
            ed = fsh if ed is None else ed * fsh
            p = q * roll_rows(k, sh) * ed
        a = a + jnp.sum(p, axis=1, keepdims=True) * mask_ref[m0 + n_lvl + d]
    o = o + jnp.dot(a.astype(BF16), vb, preferred_element_type=F32)
    return o, st_new


def _hgrn_kernel(qf_ref, qb_ref, ff_ref, fb_ref, vf_ref, vb_ref, lb_ref, tri_ref, mask_ref,
                 of_ref, ob_ref, stf_scr, stb_scr, *, n_tiles):
    @pl.when(pl.program_id(1) == 0)
    def _():
        stf_scr[...] = jnp.zeros(stf_scr.shape, F32)
        stb_scr[...] = jnp.zeros(stb_scr.shape, F32)

    C = HG_TILE
    lbf = lb_ref[0:1, :]
    lbb = lb_ref[1:2, :]

    def body(t, carry):
        rf = pl.multiple_of(t * C, C)
        rb = pl.multiple_of((n_tiles - 1 - t) * C, C)
        o_f, st_f = _hg_tile(qf_ref[pl.ds(rf, C), :], ff_ref[pl.ds(rf, C), :], vf_ref[pl.ds(rf, C), :],
                             lbf, stf_scr[...], tri_ref[0], mask_ref, False)
        of_ref[pl.ds(rf, C), :] = o_f
        stf_scr[...] = st_f
        o_b, st_b = _hg_tile(qb_ref[pl.ds(rb, C), :], fb_ref[pl.ds(rb, C), :], vb_ref[pl.ds(rb, C), :],
                             lbb, stb_scr[...], tri_ref[1], mask_ref, True)
        ob_ref[pl.ds(rb, C), :] = o_b
        stb_scr[...] = st_b
        return carry

    lax.fori_loop(0, n_tiles, body, 0, unroll=True)


def _hgrn2(proj, lb, rows):
    L = proj.shape[0]
    rows = min(rows, L)
    nb = L // rows
    H = HG_HEADS
    nm = len(HG_LEVELS) + HG_BAND
    blk = lambda col0, rev: pl.BlockSpec(
        (rows, LANES), (lambda h, i: (nb - 1 - i, col0 + h)) if rev else (lambda h, i: (i, col0 + h)))
    return pl.pallas_call(
        functools.partial(_hgrn_kernel, n_tiles=rows // HG_TILE),
        grid=(H, nb),
        in_specs=[blk(0, False), blk(0, True), blk(H, False), blk(2 * H, True),
                  blk(3 * H, False), blk(3 * H, True),
                  pl.BlockSpec((2, LANES), lambda h, i: (0, h)),
                  pl.BlockSpec((2, HG_TILE, HG_TILE), lambda h, i: (0, 0, 0)),
                  pl.BlockSpec((2 * nm, HG_TILE, HG_TILE), lambda h, i: (0, 0, 0))],
        out_specs=[blk(0, False), blk(0, True)],
        out_shape=[jax.ShapeDtypeStruct((L, HG_WIDTH), F32)] * 2,
        scratch_shapes=[pltpu.VMEM((LANES, LANES), F32)] * 2,
        compiler_params=_cp(("parallel", "arbitrary")),
        name="hgrn2",
    )(proj, proj, proj, proj, proj, proj, lb, _hg_tri(), _hg_masks())


def _outproj_kernel(x_ref, oa_ref, of_ref, ob_ref, gr_ref, ng_ref, gate_ref, lng_ref, lnb_ref, w_ref,
                    o_ref, lhs_scr):
    lhs_scr[:, 0:DIFF_WIDTH] = oa_ref[...]
    ng = ng_ref[...]
    for h in range(HG_HEADS):
        sl = slice(h * LANES, (h + 1) * LANES)
        o = of_ref[:, sl] + ob_ref[:, sl]
        ms = jnp.mean(o * o, axis=-1, keepdims=True)
        gr = gr_ref[:, sl]
        hg = o * lax.rsqrt(ms + RMS_EPS) * ng * (gr * jax.nn.sigmoid(gr))
        lhs_scr[:, DIFF_WIDTH + h * LANES:DIFF_WIDTH + (h + 1) * LANES] = hg.astype(BF16)
    y = jnp.dot(lhs_scr[...], w_ref[...], preferred_element_type=F32)
    z = ALPHA * x_ref[...] + (1.0 + gate_ref[...]) * y
    o_ref[...] = _layer_norm_rows(z, lng_ref[...], lnb_ref[...])


def _outproj_ln(x, oa, of, ob, proj_hg, norm_g, gate, ln_g, ln_b, w, tm):
    L = x.shape[0]
    tm = min(tm, L)
    row = lambda i: (i, 0)
    vec = pl.BlockSpec((1, D_MODEL), lambda i: (0, 0))
    return pl.pallas_call(
        _outproj_kernel,
        grid=(L // tm,),
        in_specs=[pl.BlockSpec((tm, D_MODEL), row),
                  pl.BlockSpec((tm, DIFF_WIDTH), row),
                  pl.BlockSpec((tm, HG_WIDTH), row),
                  pl.BlockSpec((tm, HG_WIDTH), row),
                  pl.BlockSpec((tm, HG_WIDTH), lambda i: (i, 4)),
                  pl.BlockSpec((1, LANES), lambda i: (0, 0)),
                  vec, vec, vec,
                  pl.BlockSpec((D_MODEL, D_MODEL), lambda i: (0, 0))],
        out_specs=pl.BlockSpec((tm, D_MODEL), row),
        out_shape=jax.ShapeDtypeStruct((L, D_MODEL), F32),
        scratch_shapes=[pltpu.VMEM((tm, D_MODEL), BF16)],
        compiler_params=_cp(("parallel",)),
        name="outproj_ln",
    )(x, oa, of, ob, proj_hg, norm_g, gate, ln_g, ln_b, w)


def _mlp_kernel(x_ref, sc_ref, sh_ref, gate_ref, lng_ref, lnb_ref, w1_ref, w2_ref, o_ref, h_scr, acc_scr):
    j = pl.program_id(1)

    @pl.when(j == 0)
    def _():
        h_scr[...] = (x_ref[...] * (1.0 + sc_ref[...]) + sh_ref[...]).astype(BF16)
        acc_scr[...] = jnp.zeros(acc_scr.shape, F32)

    a = jnp.maximum(jnp.dot(h_scr[...], w1_ref[...], preferred_element_type=F32), 0.0)
    acc_scr[...] += jnp.dot((a * a).astype(BF16), w2_ref[...], preferred_element_type=F32)

    @pl.when(j == pl.num_programs(1) - 1)
    def _():
        z = ALPHA * x_ref[...] + (1.0 + gate_ref[...]) * acc_scr[...]
        o_ref[...] = _layer_norm_rows(z, lng_ref[...], lnb_ref[...])


def _mlp_ln(x, scale, shift, gate, ln_g, ln_b, w1, w2, tm, tf):
    L = x.shape[0]
    tm = min(tm, L)
    vec = pl.BlockSpec((1, D_MODEL), lambda i, j: (0, 0))
    return pl.pallas_call(
        _mlp_kernel,
        grid=(L // tm, D_FF // tf),
        in_specs=[pl.BlockSpec((tm, D_MODEL), lambda i, j: (i, 0)),
                  vec, vec, vec, vec, vec,
                  pl.BlockSpec((D_MODEL, tf), lambda i, j: (0, j)),
                  pl.BlockSpec((tf, D_MODEL), lambda i, j: (j, 0))],
        out_specs=pl.BlockSpec((tm, D_MODEL), lambda i, j: (i, 0)),
        out_shape=jax.ShapeDtypeStruct((L, D_MODEL), F32),
        scratch_shapes=[pltpu.VMEM((tm, D_MODEL), BF16), pltpu.VMEM((tm, D_MODEL), F32)],
        compiler_params=_cp(("parallel", "arbitrary")),
        name="mlp_ln",
    )(x, scale, shift, gate, ln_g, ln_b, w1, w2)


def _s5_operators(lam_re, lam_im, log_dt, b_re, b_im, c_re, c_im):
    T, G, P, Hc, NB, GPB = S5_T, S5_GROUPS, S5_STATE, S5_GROUP, S5_BLOCKS, S5_GPB
    dt = jnp.exp(log_dt)[..., None]
    zr, zi = lam_re * dt, lam_im * dt
    mag = jnp.exp(zr)
    ab_re, ab_im = mag * jnp.cos(zi), mag * jnp.sin(zi)
    den = lam_re * lam_re + lam_im * lam_im
    coef_re = ((ab_re - 1.0) * lam_re + ab_im * lam_im) / den
    coef_im = (ab_im * lam_re - (ab_re - 1.0) * lam_im) / den
    bb_re = coef_re[..., None] * b_re - coef_im[..., None] * b_im
    bb_im = coef_re[..., None] * b_im + coef_im[..., None] * b_re

    def apow(k):
        k = jnp.asarray(k, F32)[:, None, None, None]
        m = jnp.exp(zr[None] * k)
        return m * jnp.cos(zi[None] * k), m * jnp.sin(zi[None] * k)

    pr, pi = apow(np.arange(T))
    ca_re = c_re[None] * pr[:, :, :, None, :] - c_im[None] * pi[:, :, :, None, :]
    ca_im = c_re[None] * pi[:, :, :, None, :] + c_im[None] * pr[:, :, :, None, :]
    bt_re, bt_im = jnp.swapaxes(bb_re, 2, 3), jnp.swapaxes(bb_im, 2, 3)
    lagk = jnp.sum(ca_re[:, :, :, :, None, :] * bt_re[None, :, :, None, :, :]
                   - ca_im[:, :, :, :, None, :] * bt_im[None, :, :, None, :, :], axis=-1)
    kc = jnp.transpose(lagk.reshape(T, 2, NB, GPB, Hc, Hc), (2, 1, 0, 5, 3, 4)).reshape(NB, 2 * T, Hc, LANES)

    def gp_lanes(v):
        return jnp.transpose(v.reshape(2, NB, GPB, Hc, P), (1, 0, 3, 2, 4)).reshape(NB, 2, Hc, GPB * P)
    bt = jnp.stack([gp_lanes(bt_re), gp_lanes(bt_im)], axis=1)
    def e_tab(v):
        v = jnp.stack([v[::-1, 0], v[:, 1]], axis=1).reshape(T, 2, NB, GPB * P)
        return jnp.transpose(v, (2, 0, 1, 3))
    et = jnp.stack([e_tab(pr), e_tab(pi)], axis=2)[:, :, :, :, None, :]

    ct = jnp.stack([gp_lanes(c_re), gp_lanes(c_im)], axis=1)
    cr, ci = apow(np.arange(1, T + 1))
    def ec_tab(v):
        v = jnp.stack([v[:, 0], v[::-1, 1]], axis=1).reshape(T, 2, NB, GPB * P)
        return jnp.transpose(v, (2, 0, 1, 3))
    ec = jnp.stack([ec_tab(cr), ec_tab(ci)], axis=2)[:, :, :, :, None, :]
    return tuple(_s5_expand(kc, bt, et, ct, ec)) + ((zr, zi),)


def _s5_expand_kernel(kc_ref, bt_ref, et_ref, ct_ref, ec_ref, wt_ref, ws_ref, wc_ref):
    s = pl.program_id(1)
    n_q = S5_GPB * S5_STATE // LANES
    gpq = S5_GPB // n_q
    row = lax.broadcasted_iota(jnp.int32, (LANES, LANES), 0)
    col = lax.broadcasted_iota(jnp.int32, (LANES, LANES), 1)
    over_groups = lambda a: jnp.concatenate([a] * S5_GPB, axis=0)

    same_group = (row // S5_GROUP) == (col // S5_GROUP)
    for t in range(S5_T):
        lag = t - s
        k = kc_ref[0, jnp.where(lag >= 0, lag, S5_T - lag)] + jnp.where(lag == 0, kc_ref[0, S5_T], 0.0)
        k = over_groups(k)
        wt_ref[0, :, t * LANES:(t + 1) * LANES] = jnp.where(same_group, k, 0.0).astype(BF16)

    for d in range(2):
        for q in range(n_q):
            sl = slice(q * LANES, (q + 1) * LANES)
            own = (row // S5_GROUP) == (gpq * q + col // S5_STATE)
            for m_ref, p_ref, o_ref, im_sign in ((bt_ref, et_ref, ws_ref, 1.0), (ct_ref, ec_ref, wc_ref, -1.0)):
                m_re, m_im = m_ref[0, 0, d, :, sl], m_ref[0, 1, d, :, sl]
                p_re, p_im = p_ref[0, 0, 0, d, :, sl], p_ref[0, 0, 1, d, :, sl]
                vals = (p_re * m_re - p_im * m_im, im_sign * (p_re * m_im + p_im * m_re))
                for ri in range(2):
                    c0 = ((d * n_q + q) * 2 + ri) * LANES
                    o_ref[0, :, c0:c0 + LANES] = jnp.where(own, over_groups(vals[ri]), 0.0).astype(BF16)


def _s5_expand(kc, bt, et, ct, ec):
    NB, T, Hc, P = S5_BLOCKS, S5_T, S5_GROUP, S5_STATE
    slab = pl.BlockSpec((1, LANES, S5_ROW), lambda b, s: (b, s, 0))
    mat = pl.BlockSpec((1, 2, 2, Hc, S5_GPB * P), lambda b, s: (b, 0, 0, 0, 0))
    powers = pl.BlockSpec((1, 1, 2, 2, 1, S5_GPB * P), lambda b, s: (b, s, 0, 0, 0, 0))
    return pl.pallas_call(
        _s5_expand_kernel,
        grid=(NB, T),
        in_specs=[pl.BlockSpec((1, 2 * T, Hc, LANES), lambda b, s: (b, 0, 0, 0)), mat, powers, mat, powers],
        out_specs=[slab, slab, slab],
        out_shape=[jax.ShapeDtypeStruct((NB, S5_ROW, S5_ROW), BF16)] * 3,
        compiler_params=_cp(("parallel", "arbitrary")),
        name="s5_expand",
    )(kc, bt, et, ct, ec)


def _s5_state_lanes(re, im):
    n = re.shape[0]
    x = jnp.stack([re, im], axis=0)
    x = x.reshape(2, n, 2, S5_BLOCKS, S5_GPB * S5_STATE // LANES, LANES)
    x = jnp.transpose(x, (2, 1, 3, 4, 0, 5))
    return x.reshape(2, n, S5_BLOCKS * S5_SW)


def _s5_scan_powers(zr, zi, n_steps):
    def power(k):
        k = jnp.asarray(k, F32)[:, :, None, None]
        m = jnp.exp(zr[None] * k)
        return m * jnp.cos(zi[None] * k), m * jnp.sin(zi[None] * k)
    k_dbl = S5_T * (2.0 ** np.arange(n_steps))
    j = np.arange(SUBLANES)
    return (_s5_state_lanes(*power(np.stack([k_dbl, k_dbl], axis=1))),
            _s5_state_lanes(*power(S5_T * np.stack([j, SUBLANES - 1 - j], axis=1))))


def _s5_fold(x_ref, sc, sh, tc):
    parts = [x_ref[pl.ds(t, tc, stride=S5_T), :] * (1.0 + sc) + sh for t in range(S5_T)]
    return parts


def _s5_state_kernel(x_ref, sc_ref, sh_ref, w_ref, zf_ref, zb_ref, *, tc):
    parts = _s5_fold(x_ref, sc_ref[...], sh_ref[...], tc)
    u = jnp.concatenate([p.astype(BF16) for p in parts], axis=1)
    z = jnp.dot(u, w_ref[0], preferred_element_type=F32)
    zf_ref[...] = z[:, :S5_SW]
    zb_ref[...] = z[:, S5_SW:]


def _s5_scan_kernel(zf_ref, zb_ref, af_ref, ab_ref, pf_ref, pb_ref, sf_ref, sb_ref, loc_scr,
                    *, n_rows, n_steps):
    n_blk = n_rows // SUBLANES
    n_local = int(math.log2(SUBLANES))
    sub = lax.broadcasted_iota(jnp.int32, (n_blk, SUBLANES, LANES), 1)
    blk = lax.broadcasted_iota(jnp.int32, (n_blk, LANES), 0)

    def cmul_add(re, im, ar, ai, rs, is_):
        return re + ar * rs - ai * is_, im + ar * is_ + ai * rs

    def scan(z_ref, a_ref, p_ref, reverse):
        re = z_ref[:, 0:LANES].reshape(n_blk, SUBLANES, LANES)
        im = z_ref[:, LANES:2 * LANES].reshape(n_blk, SUBLANES, LANES)

        def in_block(v, sh):
            if reverse:
                return jnp.where(sub < SUBLANES - sh, pltpu.roll(v, SUBLANES - sh, 1), 0.0)
            return jnp.where(sub >= sh, pltpu.roll(v, sh, 1), 0.0)

        def across(v, sh):
            if reverse:
                return jnp.where(blk < n_blk - sh, pltpu.roll(v, n_blk - sh, 0), 0.0)
            return jnp.where(blk >= sh, pltpu.roll(v, sh, 0), 0.0)

        for k in range(n_local):
            ar, ai = a_ref[0, k:k + 1, 0:LANES], a_ref[0, k:k + 1, LANES:2 * LANES]
            re, im = cmul_add(re, im, ar, ai, in_block(re, 2 ** k), in_block(im, 2 ** k))
        edge = 0 if reverse else SUBLANES - 1
        loc_scr[0] = re.reshape(n_rows, LANES)
        loc_scr[1] = im.reshape(n_rows, LANES)
        cr = loc_scr[0, pl.ds(edge, n_blk, stride=SUBLANES), :]
        ci = loc_scr[1, pl.ds(edge, n_blk, stride=SUBLANES), :]
        for k in range(n_local, n_steps):
            ar, ai = a_ref[0, k:k + 1, 0:LANES], a_ref[0, k:k + 1, LANES:2 * LANES]
            sh = 2 ** (k - n_local)
            cr, ci = cmul_add(cr, ci, ar, ai, across(cr, sh), across(ci, sh))
        cr, ci = across(cr, 1), across(ci, 1)
        pr = p_ref[0, :, 0:LANES][None]
        pi = p_ref[0, :, LANES:2 * LANES][None]
        cr3 = jnp.broadcast_to(cr[:, None, :], (n_blk, SUBLANES, LANES))
        ci3 = jnp.broadcast_to(ci[:, None, :], (n_blk, SUBLANES, LANES))
        ore, oim = cmul_add(in_block(re, 1), in_block(im, 1), pr, pi, cr3, ci3)
        return ore.reshape(n_rows, LANES), oim.reshape(n_rows, LANES)

    re, im = scan(zf_ref, af_ref, pf_ref, False)
    sf_ref[:, 0:LANES] = re.astype(BF16)
    sf_ref[:, LANES:2 * LANES] = im.astype(BF16)
    re, im = scan(zb_ref, ab_ref, pb_ref, True)
    sb_ref[:, 0:LANES] = re.astype(BF16)
    sb_ref[:, LANES:2 * LANES] = im.astype(BF16)


def _s5_out_kernel(x_ref, sc_ref, sh_ref, d_ref, sf_ref, sb_ref, wt_ref, wc_ref, o_ref, *, tc, n_half):
    j = pl.program_id(2)
    sc, sh, d = sc_ref[...], sh_ref[...], d_ref[...]
    parts = _s5_fold(x_ref, sc, sh, tc)
    u = jnp.concatenate([p.astype(BF16) for p in parts], axis=1)
    s = jnp.concatenate([sf_ref[...], sb_ref[...]], axis=1)
    y = (jnp.dot(u, wt_ref[0], preferred_element_type=F32)
         + lax.dot_general(s, wc_ref[0], (((1,), (1,)), ((), ())),
                           preferred_element_type=F32))
    tph = S5_T // n_half
    for jj in range(n_half):
        @pl.when(j == jj)
        def _():
            for tl in range(tph):
                t = jj * tph + tl
                o_ref[pl.ds(t, tc, stride=S5_T), :] = y[:, tl * LANES:(tl + 1) * LANES] + d * parts[t]


def _s5_mixer_core(x, scale, shift, d_skip, ops, tc):
    w_toep, w_state, w_carry, (zr, zi) = ops
    L = x.shape[0]
    n_rows = L // S5_T
    NB = S5_BLOCKS
    n_steps = int(math.log2(n_rows))
    assert 2 ** n_steps == n_rows
    tc = min(tc, n_rows)
    vecb = pl.BlockSpec((1, LANES), lambda b, i: (0, b))

    zf, zb = pl.pallas_call(
        functools.partial(_s5_state_kernel, tc=tc),
        grid=(NB, n_rows // tc),
        in_specs=[pl.BlockSpec((tc * S5_T, LANES), lambda b, i: (i, b)), vecb, vecb,
                  pl.BlockSpec((1, S5_ROW, 2 * S5_SW), lambda b, i: (b, 0, 0))],
        out_specs=[pl.BlockSpec((tc, S5_SW), lambda b, i: (i, b))] * 2,
        out_shape=[jax.ShapeDtypeStruct((n_rows, NB * S5_SW), F32)] * 2,
        compiler_params=_cp(("parallel", "parallel")),
        name="s5_state",
    )(x, scale, shift, w_state)

    pw, pj = _s5_scan_powers(zr, zi, n_steps)
    cw = 2 * LANES
    col = lambda n: (0, n)
    sf, sb = pl.pallas_call(
        functools.partial(_s5_scan_kernel, n_rows=n_rows, n_steps=n_steps),
        grid=(NB * S5_SW // cw,),
        in_specs=[pl.BlockSpec((n_rows, cw), col),
                  pl.BlockSpec((n_rows, cw), col),
                  pl.BlockSpec((1, n_steps, cw), lambda n: (0, 0, n)),
                  pl.BlockSpec((1, n_steps, cw), lambda n: (1, 0, n)),
                  pl.BlockSpec((1, SUBLANES, cw), lambda n: (0, 0, n)),
                  pl.BlockSpec((1, SUBLANES, cw), lambda n: (1, 0, n))],
        out_specs=[pl.BlockSpec((n_rows, cw), col)] * 2,
        out_shape=[jax.ShapeDtypeStruct((n_rows, NB * S5_SW), BF16)] * 2,
        scratch_shapes=[pltpu.VMEM((2, n_rows, LANES), F32)],
        compiler_params=_cp(("parallel",)),
        name="s5_scan",
    )(zf, zb, pw, pw, pj, pj)

    n_half = 2
    nh = S5_ROW // n_half
    vec3 = pl.BlockSpec((1, LANES), lambda b, i, j: (0, b))
    return pl.pallas_call(
        functools.partial(_s5_out_kernel, tc=tc, n_half=n_half),
        grid=(NB, n_rows // tc, n_half),
        in_specs=[pl.BlockSpec((tc * S5_T, LANES), lambda b, i, j: (i, b)), vec3, vec3, vec3,
                  pl.BlockSpec((tc, S5_SW), lambda b, i, j: (i, b)),
                  pl.BlockSpec((tc, S5_SW), lambda b, i, j: (i, b)),
                  pl.BlockSpec((1, S5_ROW, nh), lambda b, i, j: (b, 0, j)),
                  pl.BlockSpec((1, nh, 2 * S5_SW), lambda b, i, j: (b, j, 0))],
        out_specs=pl.BlockSpec((tc * S5_T, LANES), lambda b, i, j: (i, b)),
        out_shape=jax.ShapeDtypeStruct((L, D_MODEL), F32),
        compiler_params=_cp(("parallel", "parallel", "arbitrary")),
        name="s5_out",
    )(x, scale, shift, d_skip, sf, sb, w_toep, w_carry)


def _glu_kernel(x_ref, y_ref, gate_ref, lng_ref, lnb_ref, wa_ref, wg_ref, o_ref, lhs_scr, mix_scr, *, tn):
    j = pl.program_id(1)

    @pl.when(j == 0)
    def _():
        lhs_scr[...] = jax.nn.gelu(y_ref[...]).astype(BF16)

    a = jnp.dot(lhs_scr[...], wa_ref[...], preferred_element_type=F32)
    g = jnp.dot(lhs_scr[...], wg_ref[...], preferred_element_type=F32)
    mix = a * jax.nn.sigmoid(g)
    for jj in range(D_MODEL // tn):
        @pl.when(j == jj)
        def _():
            mix_scr[:, jj * tn:(jj + 1) * tn] = mix

    @pl.when(j == pl.num_programs(1) - 1)
    def _():
        z = ALPHA * x_ref[...] + (1.0 + gate_ref[...]) * mix_scr[...]
        o_ref[...] = _layer_norm_rows(z, lng_ref[...], lnb_ref[...])


def _glu_ln(x, y, gate, ln_g, ln_b, w_glu, tm, tn):
    L = x.shape[0]
    tm = min(tm, L)
    nj = D_MODEL // tn
    vec = pl.BlockSpec((1, D_MODEL), lambda i, j: (0, 0))
    return pl.pallas_call(
        functools.partial(_glu_kernel, tn=tn),
        grid=(L // tm, nj),
        in_specs=[pl.BlockSpec((tm, D_MODEL), lambda i, j: (i, 0)),
                  pl.BlockSpec((tm, D_MODEL), lambda i, j: (i, 0)),
                  vec, vec, vec,
                  pl.BlockSpec((D_MODEL, tn), lambda i, j: (0, j)),
                  pl.BlockSpec((D_MODEL, tn), lambda i, j: (0, nj + j))],
        out_specs=pl.BlockSpec((tm, D_MODEL), lambda i, j: (i, 0)),
        out_shape=jax.ShapeDtypeStruct((L, D_MODEL), F32),
        scratch_shapes=[pltpu.VMEM((tm, D_MODEL), BF16), pltpu.VMEM((tm, D_MODEL), F32)],
        compiler_params=_cp(("parallel", "arbitrary")),
        name="glu_ln",
    )(x, y, gate, ln_g, ln_b, w_glu, w_glu)


def kernel(x, c, ada_w, ada_b, ln_g, ln_b, mix_w_in, mix_w_out, diff_lambda, diff_subln_g, hg_lower_bound,
           hg_norm_g, s5_lambda_re, s5_lambda_im, s5_log_dt, s5_b_re, s5_b_im, s5_c_re, s5_c_im, s5_d,
           s5_w_glu, mlp_w1, mlp_w2):
    B, L, D = x.shape
    assert B == 1 and D == D_MODEL
    xs = x.reshape(L, D)
    mod = _adaln(c, ada_w, ada_b)
    mods = lambda a: (mod[a, :, 0:D], mod[a, :, D:2 * D], mod[a, :, 2 * D:3 * D])
    lb_table = jnp.cumsum(jax.nn.softmax(hg_lower_bound.astype(F32), axis=1), axis=1)
    rot_tabs = _rotary_tables(L)

    for layer in range(DEPTH):
        shift, scale, gate = mods(2 * layer)
        lng, lnb = ln_g[layer, 0][None], ln_b[layer, 0][None]
        if layer % 2 == 0:
            e = layer // 2
            lam_init = 0.8 - 0.6 * math.exp(-0.3 * layer)
            w_in = _to_bf16(mix_w_in, e)
            qkv = _inproj(xs, scale, shift, w_in, 0, 3 * DIFF_WIDTH, rot_tabs, 2, BF16, INPROJ_TM, INPROJ_TN)
            proj_hg = _inproj(xs, scale, shift, w_in, 3 * DIFF_WIDTH, 5 * HG_WIDTH, rot_tabs, 0, F32,
                              INPROJ_TM, INPROJ_TN)
            lf = diff_lambda[e].astype(F32)
            lam = jnp.exp(jnp.sum(lf[0] * lf[1])) - jnp.exp(jnp.sum(lf[2] * lf[3])) + lam_init
            oa = _diff_attention(qkv, lam.reshape(1), diff_subln_g[e][None], lam_init, ATTN_TQ, ATTN_TK)
            of, ob = _hgrn2(proj_hg, lb_table[:, layer], HG_ROWS)
            xs = _outproj_ln(xs, oa, of, ob, proj_hg, hg_norm_g[e][None], gate, lng, lnb,
                             _to_bf16(mix_w_out, e), OUTPROJ_TM)
        else:
            o = layer // 2
            ops = _s5_operators(s5_lambda_re[o], s5_lambda_im[o], s5_log_dt[o], s5_b_re[o], s5_b_im[o],
                                s5_c_re[o], s5_c_im[o])
            y = _s5_mixer_core(xs, scale, shift, s5_d[o][None], ops, S5_TC)
            xs = _glu_ln(xs, y, gate, lng, lnb, _to_bf16(s5_w_glu, o), GLU_TM, GLU_TN)
        shift, scale, gate = mods(2 * layer + 1)
        xs = _mlp_ln(xs, scale, shift, gate, ln_g[layer, 1][None], ln_b[layer, 1][None],
                     _to_bf16(mlp_w1, layer), _to_bf16(mlp_w2, layer), MLP_TM, MLP_TF)
    return xs.reshape(B, L, D)
```

```python
import functools
import math

import numpy as np
import jax
import jax.numpy as jnp
from jax import lax
from jax.experimental import pallas as pl
from jax.experimental.pallas import tpu as pltpu

F32 = jnp.float32
BF16 = jnp.bfloat16

D_MODEL = 2048
DEPTH = 2
LANES = 128
SUBLANES = 8

DIFF_HEADS = 8
DIFF_QK_DIM = 64
DIFF_WIDTH = 1024
ROT_DIM = DIFF_QK_DIM // 4
ROPE_THETA = 500000.0
ATTN_ROWS = 128

HG_HEADS = 8
HG_WIDTH = 1024
HG_TILE = 128
HG_BAND = SUBLANES
HG_LEVELS = (64, 32, 16, 8)

S5_GROUP = 16
S5_GROUPS = D_MODEL // S5_GROUP
S5_STATE = 64
S5_T = 16
S5_GPB = LANES // S5_GROUP
S5_BLOCKS = D_MODEL // LANES
S5_ROW = S5_T * LANES
S5_SW = S5_GPB * S5_STATE * 2

D_FF = 4 * D_MODEL
ALPHA = (2 * DEPTH) ** 0.25
LN_EPS = 1e-5
RMS_EPS = 1e-6

VMEM_LIMIT = 56 * 1024 * 1024
CAST_BLOCK_ELEMS = 2 * 1024 * 1024

ADALN_TN = 768
INPROJ_TM, INPROJ_TN = 1024, 1024
ATTN_TQ, ATTN_TK = 2048, 1024
HG_ROWS = 1024
OUTPROJ_TM = 256
MLP_TM, MLP_TF = 512, 1024
S5_TC = 512
GLU_TM, GLU_TN = 512, 1024


def _cp(sem):
    return pltpu.CompilerParams(dimension_semantics=sem, vmem_limit_bytes=VMEM_LIMIT)


def _bf16_kernel(w_ref, o_ref):
    o_ref[...] = w_ref[0].astype(BF16)


def _to_bf16(w, layer):
    _, r, c = w.shape
    tr = max(SUBLANES, CAST_BLOCK_ELEMS // c)
    assert r % tr == 0
    return pl.pallas_call(
        _bf16_kernel, grid=(r // tr,),
        in_specs=[pl.BlockSpec((1, tr, c), lambda i: (layer, i, 0))],
        out_specs=pl.BlockSpec((tr, c), lambda i: (i, 0)),
        out_shape=jax.ShapeDtypeStruct((r, c), BF16), compiler_params=_cp(("parallel",)), name="to_bf16",
    )(w)


def _layer_norm_rows(z, g, b):
    mu = jnp.mean(z, axis=-1, keepdims=True)
    zc = z - mu
    var = jnp.mean(zc * zc, axis=-1, keepdims=True)
    return zc * lax.rsqrt(var + LN_EPS) * g + b


def _adaln_kernel(c_ref, w_ref, b_ref, o_ref):
    c = c_ref[...]
    ca = c * jax.nn.sigmoid(c)
    w = w_ref[0]
    reps = w.shape[1] // LANES
    cb = jnp.concatenate([ca] * reps, axis=1)
    o_ref[0] = jnp.sum(w * cb, axis=0, keepdims=True) + b_ref[0]


def _adaln(c, ada_w, ada_b):
    n = 3 * D_MODEL
    tn = ADALN_TN
    w = ada_w.reshape(2 * DEPTH, D_MODEL, n)
    b = ada_b.reshape(2 * DEPTH, 1, n)
    c_rep = jnp.broadcast_to(c.reshape(D_MODEL, 1), (D_MODEL, LANES))
    return pl.pallas_call(
        _adaln_kernel,
        grid=(2 * DEPTH, n // tn),
        in_specs=[pl.BlockSpec((D_MODEL, LANES), lambda a, j: (0, 0)),
                  pl.BlockSpec((1, D_MODEL, tn), lambda a, j: (a, 0, j)),
                  pl.BlockSpec((1, 1, tn), lambda a, j: (a, 0, j))],
        out_specs=pl.BlockSpec((1, 1, tn), lambda a, j: (a, 0, j)),
        out_shape=jax.ShapeDtypeStruct((2 * DEPTH, 1, n), F32),
        compiler_params=_cp(("parallel", "parallel")),
        name="adaln",
    )(c_rep, w, b)


def _inproj_kernel(x_ref, sc_ref, sh_ref, w_ref, cos_ref, sin_ref, o_ref, h_scr, *, n_rot):
    j = pl.program_id(1)

    @pl.when(j == 0)
    def _():
        h_scr[...] = (x_ref[...] * (1.0 + sc_ref[...]) + sh_ref[...]).astype(BF16)

    y = jnp.dot(h_scr[...], w_ref[...], preferred_element_type=F32)

    @pl.when(j < n_rot)
    def _():
        half = ROT_DIM // 2
        cs, sn = cos_ref[...], sin_ref[...]
        pos = lax.broadcasted_iota(jnp.int32, sn.shape, 1) % DIFF_QK_DIM
        sa = jnp.where(pos >= half, sn, 0.0)
        sb = jnp.where(pos < half, -sn, 0.0)
        for g in range(y.shape[1] // LANES):
            yg = y[:, g * LANES:(g + 1) * LANES]
            rot = (yg * cs + pltpu.roll(yg, ROT_DIM // 2, 1) * sa
                   + pltpu.roll(yg, LANES - ROT_DIM // 2, 1) * sb)
            o_ref[:, g * LANES:(g + 1) * LANES] = rot.astype(o_ref.dtype)

    @pl.when(j >= n_rot)
    def _():
        o_ref[...] = y.astype(o_ref.dtype)


def _inproj(x, scale, shift, w, col0, n, rot_tabs, n_rot, out_dtype, tm, tn):
    L = x.shape[0]
    tm = min(tm, L)
    j0 = col0 // tn
    return pl.pallas_call(
        functools.partial(_inproj_kernel, n_rot=n_rot),
        grid=(L // tm, n // tn),
        in_specs=[pl.BlockSpec((tm, D_MODEL), lambda i, j: (i, 0)),
                  pl.BlockSpec((1, D_MODEL), lambda i, j: (0, 0)),
                  pl.BlockSpec((1, D_MODEL), lambda i, j: (0, 0)),
                  pl.BlockSpec((D_MODEL, tn), lambda i, j: (0, j0 + j)),
                  pl.BlockSpec((tm, LANES), lambda i, j: (i, 0)),
                  pl.BlockSpec((tm, LANES), lambda i, j: (i, 0))],
        out_specs=pl.BlockSpec((tm, tn), lambda i, j: (i, j)),
        out_shape=jax.ShapeDtypeStruct((L, n), out_dtype),
        scratch_shapes=[pltpu.VMEM((tm, D_MODEL), BF16)],
        compiler_params=_cp(("parallel", "arbitrary")),
        name="inproj",
    )(x, scale, shift, w, *rot_tabs)


def _rotary_tables(L):
    half = ROT_DIM // 2
    pos = jnp.arange(L, dtype=F32)
    inv_freq = 1.0 / (ROPE_THETA ** (jnp.arange(0, ROT_DIM, 2, dtype=F32) / ROT_DIM))
    lane = np.arange(LANES) % DIFF_QK_DIM
    rotated = jnp.asarray(lane < ROT_DIM)[None, :]
    ang = pos[:, None] * inv_freq[lane % half][None, :]
    return jnp.where(rotated, jnp.cos(ang), 1.0), jnp.where(rotated, jnp.sin(ang), 0.0)


def _attn_kernel(lam_ref, q_ref, k_ref, v_ref, g_ref, o_ref, q2_scr, m_scr, acc_scr, s_scr,
                 *, tq, tk, out_scale):
    n_rows = 2 * tq
    n_chunks = n_rows // ATTN_ROWS
    n_kv = k_ref.shape[0] // tk
    q = q_ref[...]
    lane = lax.broadcasted_iota(jnp.int32, q.shape, 1)
    qs = q * (DIFF_QK_DIM ** -0.5)
    q2_scr[0:tq, :] = jnp.where(lane < DIFF_QK_DIM, qs, 0).astype(BF16)
    q2_scr[tq:n_rows, :] = jnp.where(lane >= DIFF_QK_DIM, qs, 0).astype(BF16)
    m_scr[...] = jnp.full(m_scr.shape, -jnp.inf, F32)
    acc_scr[...] = jnp.zeros(acc_scr.shape, F32)
    ones = jnp.ones((tk, LANES), BF16)
    n_col = tk // LANES

    def scores(r, kc):
        rs = slice(r * ATTN_ROWS, (r + 1) * ATTN_ROWS)
        return lax.dot_general(q2_scr[rs, :], kc, (((1,), (1,)), ((), ())), preferred_element_type=F32)

    kc0 = k_ref[0:tk, :]
    for r in range(n_chunks):
        s_scr[r * ATTN_ROWS:(r + 1) * ATTN_ROWS, :] = scores(r, kc0)

    def step(j, prefetch):
        c0 = pl.multiple_of(j * tk, tk)
        if prefetch:
            k_next = k_ref[pl.ds(pl.multiple_of((j + 1) * tk, tk), tk), :]
        v1 = jnp.concatenate([v_ref[pl.ds(c0, tk), :], ones], axis=1)
        for r in range(n_chunks):
            rs = slice(r * ATTN_ROWS, (r + 1) * ATTN_ROWS)
            s = s_scr[rs, :]
            if prefetch:
                s_scr[rs, :] = scores(r, k_next)
            mx = s[:, 0:LANES]
            for c in range(1, n_col):
                mx = jnp.maximum(mx, s[:, c * LANES:(c + 1) * LANES])
            m_prev = m_scr[rs, :]
            m_new = jnp.maximum(m_prev, jnp.max(mx, axis=1, keepdims=True))
            alpha = jnp.exp(m_prev - m_new)
            p = jnp.exp(s - jnp.concatenate([m_new] * n_col, axis=1)).astype(BF16)
            pv = jnp.dot(p, v1, preferred_element_type=F32)
            acc_scr[rs, :] = jnp.concatenate([alpha, alpha], axis=1) * acc_scr[rs, :] + pv
            m_scr[rs, :] = m_new

    def body(j, carry):
        step(j, True)
        return carry

    lax.fori_loop(0, n_kv - 1, body, 0)
    step(n_kv - 1, False)

    acc = acc_scr[...]
    o2 = acc[:, 0:LANES] / acc[:, LANES:2 * LANES]
    o = o2[0:tq, :] - lam_ref[0] * o2[tq:n_rows, :]
    ms = jnp.mean(o * o, axis=-1, keepdims=True)
    o_ref[...] = (o * lax.rsqrt(ms + RMS_EPS) * g_ref[...] * out_scale).astype(o_ref.dtype)


def _diff_attention(qkv, lam, subln_g, lam_init, tq, tk):
    L = qkv.shape[0]
    tq, tk = min(tq, L), min(tk, L)
    H = DIFF_HEADS
    return pl.pallas_call(
        functools.partial(_attn_kernel, tq=tq, tk=tk, out_scale=1.0 - lam_init),
        grid=(H, L // tq),
        in_specs=[pl.BlockSpec(memory_space=pltpu.SMEM),
                  pl.BlockSpec((tq, LANES), lambda h, i: (i, h)),
                  pl.BlockSpec((L, LANES), lambda h, i: (0, H + h)),
                  pl.BlockSpec((L, LANES), lambda h, i: (0, 2 * H + h)),
                  pl.BlockSpec((1, LANES), lambda h, i: (0, 0))],
        out_specs=pl.BlockSpec((tq, LANES), lambda h, i: (i, h)),
        out_shape=jax.ShapeDtypeStruct((L, DIFF_WIDTH), BF16),
        scratch_shapes=[pltpu.VMEM((2 * tq, LANES), BF16),
                        pltpu.VMEM((2 * tq, LANES), F32),
                        pltpu.VMEM((2 * tq, 2 * LANES), F32),
                        pltpu.VMEM((2 * tq, tk), F32)],
        compiler_params=_cp(("parallel", "parallel")),
        name="diff_attn",
    )(lam, qkv, qkv, qkv, subln_g)


def _hg_masks():
    C = HG_TILE
    t = np.arange(C)[:, None]
    s = np.arange(C)[None, :]
    out = []
    for reverse in (False, True):
        for h in HG_LEVELS:
            same = (t // (2 * h)) == (s // (2 * h))
            t_late = (t % (2 * h)) >= h
            s_late = (s % (2 * h)) >= h
            m = same & (~t_late & s_late if reverse else t_late & ~s_late)
            out.append(m)
        for d in range(HG_BAND):
            same = (t // HG_BAND) == (s // HG_BAND)
            out.append(same & ((s == t + d) if reverse else (s == t - d)))
    return jnp.asarray(np.stack(out).astype(np.float32))


def _hg_tri():
    C = HG_TILE
    t = np.arange(C)[:, None]
    r = np.arange(C)[None, :]
    return jnp.asarray(np.stack([(r <= t), (r >= t)]).astype(np.float32), dtype=BF16)


def _hg_tile(qr, fr, v, lb, st, tri, mask_ref, reverse):
    C = HG_TILE
    n_lvl = len(HG_LEVELS)
    m0 = (n_lvl + HG_BAND) if reverse else 0
    q = qr * jax.nn.sigmoid(qr)
    f = lb + (1.0 - lb) * jax.nn.sigmoid(fr)
    k = 1.0 - f
    g = jnp.log(f)
    g1 = g.astype(BF16)
    r1 = g - g1.astype(F32)
    g2 = r1.astype(BF16)
    g3 = (r1 - g2.astype(F32)).astype(BF16)
    b3 = jnp.dot(tri, jnp.concatenate([g1, g2, g3], axis=1), preferred_element_type=F32)
    b = b3[:, 0:LANES] + b3[:, LANES:2 * LANES] + b3[:, 2 * LANES:3 * LANES]
    b_edge = b[0:1, :] if reverse else b[C - 1:C, :]
    vb = v.astype(BF16)

    qc = (q * jnp.exp(b)).astype(BF16)
    kc = (k * jnp.exp(b_edge - b)).astype(BF16)
    o = lax.dot_general(qc, st.astype(BF16), (((1,), (1,)), ((), ())), preferred_element_type=F32)
    st_new = st * jnp.exp(b_edge) + jnp.dot(v.T.astype(BF16), kc, preferred_element_type=F32)

    a = jnp.zeros((C, C), F32)
    for li, h in enumerate(HG_LEVELS):
        rows = []
        for blk in range(C // (2 * h)):
            r = blk * 2 * h + (h if reverse else h - 1)
            rows.append(jnp.broadcast_to(b[r:r + 1, :], (2 * h, LANES)))
        bref = rows[0] if len(rows) == 1 else jnp.concatenate(rows, axis=0)
        e = jnp.exp(-jnp.abs(b - bref))
        ah = lax.dot_general((q * e).astype(BF16), (k * e).astype(BF16), (((1,), (1,)), ((), ())),
                             preferred_element_type=F32)
        a = a + ah * mask_ref[m0 + li]
    def roll_rows(x, shift):
        x3 = pltpu.roll(x.reshape(C // HG_BAND, HG_BAND, LANES), shift % HG_BAND, 1)
        return x3.reshape(C, LANES)
    ed = None
    for d in range(HG_BAND):
        if d == 0:
            p = q * k
        else:
            sh = (C - d) if reverse else d
            shf = (C - (d - 1)) % C if reverse else d - 1
            fsh = f if shf == 0 else roll_rows(f, shf)
            ed = fsh if ed is None else ed * fsh
            p = q * roll_rows(k, sh) * ed
        a = a + jnp.sum(p, axis=1, keepdims=True) * mask_ref[m0 + n_lvl + d]
    o = o + jnp.dot(a.astype(BF16), vb, preferred_element_type=F32)
    return o, st_new


def _hgrn_kernel(qf_ref, qb_ref, ff_ref, fb_ref, vf_ref, vb_ref, lb_ref, tri_ref, mask_ref,
                 of_ref, ob_ref, stf_scr, stb_scr, *, n_tiles):
    @pl.when(pl.program_id(1) == 0)
    def _():
        stf_scr[...] = jnp.zeros(stf_scr.shape, F32)
        stb_scr[...] = jnp.zeros(stb_scr.shape, F32)

    C = HG_TILE
    lbf = lb_ref[0:1, :]
    lbb = lb_ref[1:2, :]

    def body(t, carry):
        rf = pl.multiple_of(t * C, C)
        rb = pl.multiple_of((n_tiles - 1 - t) * C, C)
        o_f, st_f = _hg_tile(qf_ref[pl.ds(rf, C), :], ff_ref[pl.ds(rf, C), :], vf_ref[pl.ds(rf, C), :],
                             lbf, stf_scr[...], tri_ref[0], mask_ref, False)
        of_ref[pl.ds(rf, C), :] = o_f
        stf_scr[...] = st_f
        o_b, st_b = _hg_tile(qb_ref[pl.ds(rb, C), :], fb_ref[pl.ds(rb, C), :], vb_ref[pl.ds(rb, C), :],
                             lbb, stb_scr[...], tri_ref[1], mask_ref, True)
        ob_ref[pl.ds(rb, C), :] = o_b
        stb_scr[...] = st_b
        return carry

    lax.fori_loop(0, n_tiles, body, 0, unroll=True)


def _hgrn2(proj, lb, rows):
    L = proj.shape[0]
    rows = min(rows, L)
    nb = L // rows
    H = HG_HEADS
    nm = len(HG_LEVELS) + HG_BAND
    blk = lambda col0, rev: pl.BlockSpec(
        (rows, LANES), (lambda h, i: (nb - 1 - i, col0 + h)) if rev else (lambda h, i: (i, col0 + h)))
    return pl.pallas_call(
        functools.partial(_hgrn_kernel, n_tiles=rows // HG_TILE),
        grid=(H, nb),
        in_specs=[blk(0, False), blk(0, True), blk(H, False), blk(2 * H, True),
                  blk(3 * H, False), blk(3 * H, True),
                  pl.BlockSpec((2, LANES), lambda h, i: (0, h)),
                  pl.BlockSpec((2, HG_TILE, HG_TILE), lambda h, i: (0, 0, 0)),
                  pl.BlockSpec((2 * nm, HG_TILE, HG_TILE), lambda h, i: (0, 0, 0))],
        out_specs=[blk(0, False), blk(0, True)],
        out_shape=[jax.ShapeDtypeStruct((L, HG_WIDTH), F32)] * 2,
        scratch_shapes=[pltpu.VMEM((LANES, LANES), F32)] * 2,
        compiler_params=_cp(("parallel", "arbitrary")),
        name="hgrn2",
    )(proj, proj, proj, proj, proj, proj, lb, _hg_tri(), _hg_masks())


def _outproj_kernel(x_ref, oa_ref, of_ref, ob_ref, gr_ref, ng_ref, gate_ref, lng_ref, lnb_ref, w_ref,
                    o_ref, lhs_scr):
    lhs_scr[:, 0:DIFF_WIDTH] = oa_ref[...]
    ng = ng_ref[...]
    for h in range(HG_HEADS):
        sl = slice(h * LANES, (h + 1) * LANES)
        o = of_ref[:, sl] + ob_ref[:, sl]
        ms = jnp.mean(o * o, axis=-1, keepdims=True)
        gr = gr_ref[:, sl]
        hg = o * lax.rsqrt(ms + RMS_EPS) * ng * (gr * jax.nn.sigmoid(gr))
        lhs_scr[:, DIFF_WIDTH + h * LANES:DIFF_WIDTH + (h + 1) * LANES] = hg.astype(BF16)
    y = jnp.dot(lhs_scr[...], w_ref[...], preferred_element_type=F32)
    z = ALPHA * x_ref[...] + (1.0 + gate_ref[...]) * y
    o_ref[...] = _layer_norm_rows(z, lng_ref[...], lnb_ref[...])


def _outproj_ln(x, oa, of, ob, proj_hg, norm_g, gate, ln_g, ln_b, w, tm):
    L = x.shape[0]
    tm = min(tm, L)
    row = lambda i: (i, 0)
    vec = pl.BlockSpec((1, D_MODEL), lambda i: (0, 0))
    return pl.pallas_call(
        _outproj_kernel,
        grid=(L // tm,),
        in_specs=[pl.BlockSpec((tm, D_MODEL), row),
                  pl.BlockSpec((tm, DIFF_WIDTH), row),
                  pl.BlockSpec((tm, HG_WIDTH), row),
                  pl.BlockSpec((tm, HG_WIDTH), row),
                  pl.BlockSpec((tm, HG_WIDTH), lambda i: (i, 4)),
                  pl.BlockSpec((1, LANES), lambda i: (0, 0)),
                  vec, vec, vec,
                  pl.BlockSpec((D_MODEL, D_MODEL), lambda i: (0, 0))],
        out_specs=pl.BlockSpec((tm, D_MODEL), row),
        out_shape=jax.ShapeDtypeStruct((L, D_MODEL), F32),
        scratch_shapes=[pltpu.VMEM((tm, D_MODEL), BF16)],
        compiler_params=_cp(("parallel",)),
        name="outproj_ln",
    )(x, oa, of, ob, proj_hg, norm_g, gate, ln_g, ln_b, w)


def _mlp_kernel(x_ref, sc_ref, sh_ref, gate_ref, lng_ref, lnb_ref, w1_ref, w2_ref, o_ref, h_scr, acc_scr):
    j = pl.program_id(1)

    @pl.when(j == 0)
    def _():
        h_scr[...] = (x_ref[...] * (1.0 + sc_ref[...]) + sh_ref[...]).astype(BF16)
        acc_scr[...] = jnp.zeros(acc_scr.shape, F32)

    a = jnp.maximum(jnp.dot(h_scr[...], w1_ref[...], preferred_element_type=F32), 0.0)
    acc_scr[...] += jnp.dot((a * a).astype(BF16), w2_ref[...], preferred_element_type=F32)

    @pl.when(j == pl.num_programs(1) - 1)
    def _():
        z = ALPHA * x_ref[...] + (1.0 + gate_ref[...]) * acc_scr[...]
        o_ref[...] = _layer_norm_rows(z, lng_ref[...], lnb_ref[...])


def _mlp_ln(x, scale, shift, gate, ln_g, ln_b, w1, w2, tm, tf):
    L = x.shape[0]
    tm = min(tm, L)
    vec = pl.BlockSpec((1, D_MODEL), lambda i, j: (0, 0))
    return pl.pallas_call(
        _mlp_kernel,
        grid=(L // tm, D_FF // tf),
        in_specs=[pl.BlockSpec((tm, D_MODEL), lambda i, j: (i, 0)),
                  vec, vec, vec, vec, vec,
                  pl.BlockSpec((D_MODEL, tf), lambda i, j: (0, j)),
                  pl.BlockSpec((tf, D_MODEL), lambda i, j: (j, 0))],
        out_specs=pl.BlockSpec((tm, D_MODEL), lambda i, j: (i, 0)),
        out_shape=jax.ShapeDtypeStruct((L, D_MODEL), F32),
        scratch_shapes=[pltpu.VMEM((tm, D_MODEL), BF16), pltpu.VMEM((tm, D_MODEL), F32)],
        compiler_params=_cp(("parallel", "arbitrary")),
        name="mlp_ln",
    )(x, scale, shift, gate, ln_g, ln_b, w1, w2)


def _s5_operators(lam_re, lam_im, log_dt, b_re, b_im, c_re, c_im):
    T, G, P, Hc, NB, GPB = S5_T, S5_GROUPS, S5_STATE, S5_GROUP, S5_BLOCKS, S5_GPB
    dt = jnp.exp(log_dt)[..., None]
    zr, zi = lam_re * dt, lam_im * dt
    mag = jnp.exp(zr)
    ab_re, ab_im = mag * jnp.cos(zi), mag * jnp.sin(zi)
    den = lam_re * lam_re + lam_im * lam_im
    coef_re = ((ab_re - 1.0) * lam_re + ab_im * lam_im) / den
    coef_im = (ab_im * lam_re - (ab_re - 1.0) * lam_im) / den
    bb_re = coef_re[..., None] * b_re - coef_im[..., None] * b_im
    bb_im = coef_re[..., None] * b_im + coef_im[..., None] * b_re

    def apow(k):
        k = jnp.asarray(k, F32)[:, None, None, None]
        m = jnp.exp(zr[None] * k)
        return m * jnp.cos(zi[None] * k), m * jnp.sin(zi[None] * k)

    pr, pi = apow(np.arange(T))
    ca_re = c_re[None] * pr[:, :, :, None, :] - c_im[None] * pi[:, :, :, None, :]
    ca_im = c_re[None] * pi[:, :, :, None, :] + c_im[None] * pr[:, :, :, None, :]
    bt_re, bt_im = jnp.swapaxes(bb_re, 2, 3), jnp.swapaxes(bb_im, 2, 3)
    lagk = jnp.sum(ca_re[:, :, :, :, None, :] * bt_re[None, :, :, None, :, :]
                   - ca_im[:, :, :, :, None, :] * bt_im[None, :, :, None, :, :], axis=-1)
    kc = jnp.transpose(lagk.reshape(T, 2, NB, GPB, Hc, Hc), (2, 1, 0, 5, 3, 4)).reshape(NB, 2 * T, Hc, LANES)

    def gp_lanes(v):
        return jnp.transpose(v.reshape(2, NB, GPB, Hc, P), (1, 0, 3, 2, 4)).reshape(NB, 2, Hc, GPB * P)
    bt = jnp.stack([gp_lanes(bt_re), gp_lanes(bt_im)], axis=1)
    def e_tab(v):
        v = jnp.stack([v[::-1, 0], v[:, 1]], axis=1).reshape(T, 2, NB, GPB * P)
        return jnp.transpose(v, (2, 0, 1, 3))
    et = jnp.stack([e_tab(pr), e_tab(pi)], axis=2)[:, :, :, :, None, :]

    ct = jnp.stack([gp_lanes(c_re), gp_lanes(c_im)], axis=1)
    cr, ci = apow(np.arange(1, T + 1))
    def ec_tab(v):
        v = jnp.stack([v[:, 0], v[::-1, 1]], axis=1).reshape(T, 2, NB, GPB * P)
        return jnp.transpose(v, (2, 0, 1, 3))
    ec = jnp.stack([ec_tab(cr), ec_tab(ci)], axis=2)[:, :, :, :, None, :]
    return tuple(_s5_expand(kc, bt, et, ct, ec)) + ((zr, zi),)


def _s5_expand_kernel(kc_ref, bt_ref, et_ref, ct_ref, ec_ref, wt_ref, ws_ref, wc_ref):
    s = pl.program_id(1)
    n_q = S5_GPB * S5_STATE // LANES
    gpq = S5_GPB // n_q
    row = lax.broadcasted_iota(jnp.int32, (LANES, LANES), 0)
    col = lax.broadcasted_iota(jnp.int32, (LANES, LANES), 1)
    over_groups = lambda a: jnp.concatenate([a] * S5_GPB, axis=0)

    same_group = (row // S5_GROUP) == (col // S5_GROUP)
    for t in range(S5_T):
        lag = t - s
        k = kc_ref[0, jnp.where(lag >= 0, lag, S5_T - lag)] + jnp.where(lag == 0, kc_ref[0, S5_T], 0.0)
        k = over_groups(k)
        wt_ref[0, :, t * LANES:(t + 1) * LANES] = jnp.where(same_group, k, 0.0).astype(BF16)

    for d in range(2):
        for q in range(n_q):
            sl = slice(q * LANES, (q + 1) * LANES)
            own = (row // S5_GROUP) == (gpq * q + col // S5_STATE)
            for m_ref, p_ref, o_ref, im_sign in ((bt_ref, et_ref, ws_ref, 1.0), (ct_ref, ec_ref, wc_ref, -1.0)):
                m_re, m_im = m_ref[0, 0, d, :, sl], m_ref[0, 1, d, :, sl]
                p_re, p_im = p_ref[0, 0, 0, d, :, sl], p_ref[0, 0, 1, d, :, sl]
                vals = (p_re * m_re - p_im * m_im, im_sign * (p_re * m_im + p_im * m_re))
                for ri in range(2):
                    c0 = ((d * n_q + q) * 2 + ri) * LANES
                    o_ref[0, :, c0:c0 + LANES] = jnp.where(own, over_groups(vals[ri]), 0.0).astype(BF16)


def _s5_expand(kc, bt, et, ct, ec):
    NB, T, Hc, P = S5_BLOCKS, S5_T, S5_GROUP, S5_STATE
    slab = pl.BlockSpec((1, LANES, S5_ROW), lambda b, s: (b, s, 0))
    mat = pl.BlockSpec((1, 2, 2, Hc, S5_GPB * P), lambda b, s: (b, 0, 0, 0, 0))
    powers = pl.BlockSpec((1, 1, 2, 2, 1, S5_GPB * P), lambda b, s: (b, s, 0, 0, 0, 0))
    return pl.pallas_call(
        _s5_expand_kernel,
        grid=(NB, T),
        in_specs=[pl.BlockSpec((1, 2 * T, Hc, LANES), lambda b, s: (b, 0, 0, 0)), mat, powers, mat, powers],
        out_specs=[slab, slab, slab],
        out_shape=[jax.ShapeDtypeStruct((NB, S5_ROW, S5_ROW), BF16)] * 3,
        compiler_params=_cp(("parallel", "arbitrary")),
        name="s5_expand",
    )(kc, bt, et, ct, ec)


def _s5_state_lanes(re, im):
    n = re.shape[0]
    x = jnp.stack([re, im], axis=0)
    x = x.reshape(2, n, 2, S5_BLOCKS, S5_GPB * S5_STATE // LANES, LANES)
    x = jnp.transpose(x, (2, 1, 3, 4, 0, 5))
    return x.reshape(2, n, S5_BLOCKS * S5_SW)


def _s5_scan_powers(zr, zi, n_steps):
    def power(k):
        k = jnp.asarray(k, F32)[:, :, None, None]
        m = jnp.exp(zr[None] * k)
        return m * jnp.cos(zi[None] * k), m * jnp.sin(zi[None] * k)
    k_dbl = S5_T * (2.0 ** np.arange(n_steps))
    j = np.arange(SUBLANES)
    return (_s5_state_lanes(*power(np.stack([k_dbl, k_dbl], axis=1))),
            _s5_state_lanes(*power(S5_T * np.stack([j, SUBLANES - 1 - j], axis=1))))


def _s5_fold(x_ref, sc, sh, tc):
    parts = [x_ref[pl.ds(t, tc, stride=S5_T), :] * (1.0 + sc) + sh for t in range(S5_T)]
    return parts


def _s5_state_kernel(x_ref, sc_ref, sh_ref, w_ref, zf_ref, zb_ref, *, tc):
    parts = _s5_fold(x_ref, sc_ref[...], sh_ref[...], tc)
    u = jnp.concatenate([p.astype(BF16) for p in parts], axis=1)
    z = jnp.dot(u, w_ref[0], preferred_element_type=F32)
    zf_ref[...] = z[:, :S5_SW]
    zb_ref[...] = z[:, S5_SW:]


def _s5_scan_kernel(zf_ref, zb_ref, af_ref, ab_ref, pf_ref, pb_ref, sf_ref, sb_ref, loc_scr,
                    *, n_rows, n_steps):
    n_blk = n_rows // SUBLANES
    n_local = int(math.log2(SUBLANES))
    sub = lax.broadcasted_iota(jnp.int32, (n_blk, SUBLANES, LANES), 1)
    blk = lax.broadcasted_iota(jnp.int32, (n_blk, LANES), 0)

    def cmul_add(re, im, ar, ai, rs, is_):
        return re + ar * rs - ai * is_, im + ar * is_ + ai * rs

    def scan(z_ref, a_ref, p_ref, reverse):
        re = z_ref[:, 0:LANES].reshape(n_blk, SUBLANES, LANES)
        im = z_ref[:, LANES:2 * LANES].reshape(n_blk, SUBLANES, LANES)

        def in_block(v, sh):
            if reverse:
                return jnp.where(sub < SUBLANES - sh, pltpu.roll(v, SUBLANES - sh, 1), 0.0)
            return jnp.where(sub >= sh, pltpu.roll(v, sh, 1), 0.0)

        def across(v, sh):
            if reverse:
                return jnp.where(blk < n_blk - sh, pltpu.roll(v, n_blk - sh, 0), 0.0)
            return jnp.where(blk >= sh, pltpu.roll(v, sh, 0), 0.0)

        for k in range(n_local):
            ar, ai = a_ref[0, k:k + 1, 0:LANES], a_ref[0, k:k + 1, LANES:2 * LANES]
            re, im = cmul_add(re, im, ar, ai, in_block(re, 2 ** k), in_block(im, 2 ** k))
        edge = 0 if reverse else SUBLANES - 1
        loc_scr[0] = re.reshape(n_rows, LANES)
        loc_scr[1] = im.reshape(n_rows, LANES)
        cr = loc_scr[0, pl.ds(edge, n_blk, stride=SUBLANES), :]
        ci = loc_scr[1, pl.ds(edge, n_blk, stride=SUBLANES), :]
        for k in range(n_local, n_steps):
            ar, ai = a_ref[0, k:k + 1, 0:LANES], a_ref[0, k:k + 1, LANES:2 * LANES]
            sh = 2 ** (k - n_local)
            cr, ci = cmul_add(cr, ci, ar, ai, across(cr, sh), across(ci, sh))
        cr, ci = across(cr, 1), across(ci, 1)
        pr = p_ref[0, :, 0:LANES][None]
        pi = p_ref[0, :, LANES:2 * LANES][None]
        cr3 = jnp.broadcast_to(cr[:, None, :], (n_blk, SUBLANES, LANES))
        ci3 = jnp.broadcast_to(ci[:, None, :], (n_blk, SUBLANES, LANES))
        ore, oim = cmul_add(in_block(re, 1), in_block(im, 1), pr, pi, cr3, ci3)
        return ore.reshape(n_rows, LANES), oim.reshape(n_rows, LANES)

    re, im = scan(zf_ref, af_ref, pf_ref, False)
    sf_ref[:, 0:LANES] = re.astype(BF16)
    sf_ref[:, LANES:2 * LANES] = im.astype(BF16)
    re, im = scan(zb_ref, ab_ref, pb_ref, True)
    sb_ref[:, 0:LANES] = re.astype(BF16)
    sb_ref[:, LANES:2 * LANES] = im.astype(BF16)


def _s5_out_kernel(x_ref, sc_ref, sh_ref, d_ref, sf_ref, sb_ref, wt_ref, wc_ref, o_ref, *, tc, n_half):
    j = pl.program_id(2)
    sc, sh, d = sc_ref[...], sh_ref[...], d_ref[...]
    parts = _s5_fold(x_ref, sc, sh, tc)
    u = jnp.concatenate([p.astype(BF16) for p in parts], axis=1)
    s = jnp.concatenate([sf_ref[...], sb_ref[...]], axis=1)
    y = (jnp.dot(u, wt_ref[0], preferred_element_type=F32)
         + lax.dot_general(s, wc_ref[0], (((1,), (1,)), ((), ())),
                           preferred_element_type=F32))
    tph = S5_T // n_half
    for jj in range(n_half):
        @pl.when(j == jj)
        def _():
            for tl in range(tph):
                t = jj * tph + tl
                o_ref[pl.ds(t, tc, stride=S5_T), :] = y[:, tl * LANES:(tl + 1) * LANES] + d * parts[t]


def _s5_mixer_core(x, scale, shift, d_skip, ops, tc):
    w_toep, w_state, w_carry, (zr, zi) = ops
    L = x.shape[0]
    n_rows = L // S5_T
    NB = S5_BLOCKS
    n_steps = int(math.log2(n_rows))
    assert 2 ** n_steps == n_rows
    tc = min(tc, n_rows)
    vecb = pl.BlockSpec((1, LANES), lambda b, i: (0, b))

    zf, zb = pl.pallas_call(
        functools.partial(_s5_state_kernel, tc=tc),
        grid=(NB, n_rows // tc),
        in_specs=[pl.BlockSpec((tc * S5_T, LANES), lambda b, i: (i, b)), vecb, vecb,
                  pl.BlockSpec((1, S5_ROW, 2 * S5_SW), lambda b, i: (b, 0, 0))],
        out_specs=[pl.BlockSpec((tc, S5_SW), lambda b, i: (i, b))] * 2,
        out_shape=[jax.ShapeDtypeStruct((n_rows, NB * S5_SW), F32)] * 2,
        compiler_params=_cp(("parallel", "parallel")),
        name="s5_state",
    )(x, scale, shift, w_state)

    pw, pj = _s5_scan_powers(zr, zi, n_steps)
    cw = 2 * LANES
    col = lambda n: (0, n)
    sf, sb = pl.pallas_call(
        functools.partial(_s5_scan_kernel, n_rows=n_rows, n_steps=n_steps),
        grid=(NB * S5_SW // cw,),
        in_specs=[pl.BlockSpec((n_rows, cw), col),
                  pl.BlockSpec((n_rows, cw), col),
                  pl.BlockSpec((1, n_steps, cw), lambda n: (0, 0, n)),
                  pl.BlockSpec((1, n_steps, cw), lambda n: (1, 0, n)),
                  pl.BlockSpec((1, SUBLANES, cw), lambda n: (0, 0, n)),
                  pl.BlockSpec((1, SUBLANES, cw), lambda n: (1, 0, n))],
        out_specs=[pl.BlockSpec((n_rows, cw), col)] * 2,
        out_shape=[jax.ShapeDtypeStruct((n_rows, NB * S5_SW), BF16)] * 2,
        scratch_shapes=[pltpu.VMEM((2, n_rows, LANES), F32)],
        compiler_params=_cp(("parallel",)),
        name="s5_scan",
    )(zf, zb, pw, pw, pj, pj)

    n_half = 2
    nh = S5_ROW // n_half
    vec3 = pl.BlockSpec((1, LANES), lambda b, i, j: (0, b))
    return pl.pallas_call(
        functools.partial(_s5_out_kernel, tc=tc, n_half=n_half),
        grid=(NB, n_rows // tc, n_half),
        in_specs=[pl.BlockSpec((tc * S5_T, LANES), lambda b, i, j: (i, b)), vec3, vec3, vec3,
                  pl.BlockSpec((tc, S5_SW), lambda b, i, j: (i, b)),
                  pl.BlockSpec((tc, S5_SW), lambda b, i, j: (i, b)),
                  pl.BlockSpec((1, S5_ROW, nh), lambda b, i, j: (b, 0, j)),
                  pl.BlockSpec((1, nh, 2 * S5_SW), lambda b, i, j: (b, j, 0))],
        out_specs=pl.BlockSpec((tc * S5_T, LANES), lambda b, i, j: (i, b)),
        out_shape=jax.ShapeDtypeStruct((L, D_MODEL), F32),
        compiler_params=_cp(("parallel", "parallel", "arbitrary")),
        name="s5_out",
    )(x, scale, shift, d_skip, sf, sb, w_toep, w_carry)


def _glu_kernel(x_ref, y_ref, gate_ref, lng_ref, lnb_ref, wa_ref, wg_ref, o_ref, lhs_scr, mix_scr, *, tn):
    j = pl.program_id(1)

    @pl.when(j == 0)
    def _():
        lhs_scr[...] = jax.nn.gelu(y_ref[...]).astype(BF16)

    a = jnp.dot(lhs_scr[...], wa_ref[...], preferred_element_type=F32)
    g = jnp.dot(lhs_scr[...], wg_ref[...], preferred_element_type=F32)
    mix = a * jax.nn.sigmoid(g)
    for jj in range(D_MODEL // tn):
        @pl.when(j == jj)
        def _():
            mix_scr[:, jj * tn:(jj + 1) * tn] = mix

    @pl.when(j == pl.num_programs(1) - 1)
    def _():
        z = ALPHA * x_ref[...] + (1.0 + gate_ref[...]) * mix_scr[...]
        o_ref[...] = _layer_norm_rows(z, lng_ref[...], lnb_ref[...])


def _glu_ln(x, y, gate, ln_g, ln_b, w_glu, tm, tn):
    L = x.shape[0]
    tm = min(tm, L)
    nj = D_MODEL // tn
    vec = pl.BlockSpec((1, D_MODEL), lambda i, j: (0, 0))
    return pl.pallas_call(
        functools.partial(_glu_kernel, tn=tn),
        grid=(L // tm, nj),
        in_specs=[pl.BlockSpec((tm, D_MODEL), lambda i, j: (i, 0)),
                  pl.BlockSpec((tm, D_MODEL), lambda i, j: (i, 0)),
                  vec, vec, vec,
                  pl.BlockSpec((D_MODEL, tn), lambda i, j: (0, j)),
                  pl.BlockSpec((D_MODEL, tn), lambda i, j: (0, nj + j))],
        out_specs=pl.BlockSpec((tm, D_MODEL), lambda i, j: (i, 0)),
        out_shape=jax.ShapeDtypeStruct((L, D_MODEL), F32),
        scratch_shapes=[pltpu.VMEM((tm, D_MODEL), BF16), pltpu.VMEM((tm, D_MODEL), F32)],
        compiler_params=_cp(("parallel", "arbitrary")),
        name="glu_ln",
    )(x, y, gate, ln_g, ln_b, w_glu, w_glu)


def kernel(x, c, ada_w, ada_b, ln_g, ln_b, mix_w_in, mix_w_out, diff_lambda, diff_subln_g, hg_lower_bound,
           hg_norm_g, s5_lambda_re, s5_lambda_im, s5_log_dt, s5_b_re, s5_b_im, s5_c_re, s5_c_im, s5_d,
           s5_w_glu, mlp_w1, mlp_w2):
    B, L, D = x.shape
    assert B == 1 and D == D_MODEL
    xs = x.reshape(L, D)
    mod = _adaln(c, ada_w, ada_b)
    mods = lambda a: (mod[a, :, 0:D], mod[a, :, D:2 * D], mod[a, :, 2 * D:3 * D])
    lb_table = jnp.cumsum(jax.nn.softmax(hg_lower_bound.astype(F32), axis=1), axis=1)
    rot_tabs = _rotary_tables(L)

    for layer in range(DEPTH):
        shift, scale, gate = mods(2 * layer)
        lng, lnb = ln_g[layer, 0][None], ln_b[layer, 0][None]
        if layer % 2 == 0:
            e = layer // 2
            lam_init = 0.8 - 0.6 * math.exp(-0.3 * layer)
            w_in = _to_bf16(mix_w_in, e)
            qkv = _inproj(xs, scale, shift, w_in, 0, 3 * DIFF_WIDTH, rot_tabs, 2, BF16, INPROJ_TM, INPROJ_TN)
            proj_hg = _inproj(xs, scale, shift, w_in, 3 * DIFF_WIDTH, 5 * HG_WIDTH, rot_tabs, 0, F32,
                              INPROJ_TM, INPROJ_TN)
            lf = diff_lambda[e].astype(F32)
            lam = jnp.exp(jnp.sum(lf[0] * lf[1])) - jnp.exp(jnp.sum(lf[2] * lf[3])) + lam_init
            oa = _diff_attention(qkv, lam.reshape(1), diff_subln_g[e][None], lam_init, ATTN_TQ, ATTN_TK)
            of, ob = _hgrn2(proj_hg, lb_table[:, layer], HG_ROWS)
            xs = _outproj_ln(xs, oa, of, ob, proj_hg, hg_norm_g[e][None], gate, lng, lnb,
                             _to_bf16(mix_w_out, e), OUTPROJ_TM)
        else:
            o = layer // 2
            ops = _s5_operators(s5_lambda_re[o], s5_lambda_im[o], s5_log_dt[o], s5_b_re[o], s5_b_im[o],
                                s5_c_re[o], s5_c_im[o])
            y = _s5_mixer_core(xs, scale, shift, s5_d[o][None], ops, S5_TC)
            xs = _glu_ln(xs, y, gate, lng, lnb, _to_bf16(s5_w_glu, o), GLU_TM, GLU_TN)
        shift, scale, gate = mods(2 * layer + 1)
        xs = _mlp_ln(xs, scale, shift, gate, ln_g[layer, 1][None], ln_b[layer, 1][None],
                     _to_bf16(mlp_w1, layer), _to_bf16(mlp_w2, layer), MLP_TM, MLP_TF)
    return xs.reshape(B, L, D)
```

```python
import functools
import math

import numpy as np
import jax
import jax.numpy as jnp
from jax import lax
from jax.experimental import pallas as pl
from jax.experimental.pallas import tpu as pltpu

F32 = jnp.float32
BF16 = jnp.bfloat16

D_MODEL = 2048
DEPTH = 2
LANES = 128
SUBLANES = 8

DIFF_HEADS = 8
DIFF_QK_DIM = 64
DIFF_WIDTH = 1024
ROT_DIM = DIFF_QK_DIM // 4
ROPE_THETA = 500000.0
ATTN_ROWS = 128

HG_HEADS = 8
HG_WIDTH = 1024
HG_TILE = 128
HG_BAND = SUBLANES
HG_LEVELS = (64, 32, 16, 8)

S5_GROUP = 16
S5_GROUPS = D_MODEL // S5_GROUP
S5_STATE = 64
S5_T = 16
S5_GPB = LANES // S5_GROUP
S5_BLOCKS = D_MODEL // LANES
S5_ROW = S5_T * LANES
S5_SW = S5_GPB * S5_STATE * 2

D_FF = 4 * D_MODEL
ALPHA = (2 * DEPTH) ** 0.25
LN_EPS = 1e-5
RMS_EPS = 1e-6

VMEM_LIMIT = 56 * 1024 * 1024
CAST_BLOCK_ELEMS = 2 * 1024 * 1024

ADALN_TN = 768
INPROJ_TM, INPROJ_TN = 1024, 1024
ATTN_TQ, ATTN_TK = 2048, 1024
HG_ROWS = 1024
OUTPROJ_TM = 256
MLP_TM, MLP_TF = 512, 1024
S5_TC = 512
S5_EXPAND_SLABS = 4
GLU_TM, GLU_TN = 512, 1024


def _cp(sem):
    return pltpu.CompilerParams(dimension_semantics=sem, vmem_limit_bytes=VMEM_LIMIT)


def _bf16_kernel(w_ref, o_ref):
    o_ref[...] = w_ref[0].astype(BF16)


def _to_bf16(w, layer):
    _, r, c = w.shape
    tr = max(SUBLANES, CAST_BLOCK_ELEMS // c)
    assert r % tr == 0
    return pl.pallas_call(
        _bf16_kernel, grid=(r // tr,),
        in_specs=[pl.BlockSpec((1, tr, c), lambda i: (layer, i, 0))],
        out_specs=pl.BlockSpec((tr, c), lambda i: (i, 0)),
        out_shape=jax.ShapeDtypeStruct((r, c), BF16), compiler_params=_cp(("parallel",)), name="to_bf16",
    )(w)


def _layer_norm_rows(z, g, b):
    mu = jnp.mean(z, axis=-1, keepdims=True)
    zc = z - mu
    var = jnp.mean(zc * zc, axis=-1, keepdims=True)
    return zc * lax.rsqrt(var + LN_EPS) * g + b


def _adaln_kernel(c_ref, w_ref, b_ref, o_ref):
    c = c_ref[...]
    ca = c * jax.nn.sigmoid(c)
    w = w_ref[0]
    reps = w.shape[1] // LANES
    cb = jnp.concatenate([ca] * reps, axis=1)
    o_ref[0] = jnp.sum(w * cb, axis=0, keepdims=True) + b_ref[0]


def _adaln(c, ada_w, ada_b):
    n = 3 * D_MODEL
    tn = ADALN_TN
    w = ada_w.reshape(2 * DEPTH, D_MODEL, n)
    b = ada_b.reshape(2 * DEPTH, 1, n)
    c_rep = jnp.broadcast_to(c.reshape(D_MODEL, 1), (D_MODEL, LANES))
    return pl.pallas_call(
        _adaln_kernel,
        grid=(2 * DEPTH, n // tn),
        in_specs=[pl.BlockSpec((D_MODEL, LANES), lambda a, j: (0, 0)),
                  pl.BlockSpec((1, D_MODEL, tn), lambda a, j: (a, 0, j)),
                  pl.BlockSpec((1, 1, tn), lambda a, j: (a, 0, j))],
        out_specs=pl.BlockSpec((1, 1, tn), lambda a, j: (a, 0, j)),
        out_shape=jax.ShapeDtypeStruct((2 * DEPTH, 1, n), F32),
        compiler_params=_cp(("parallel", "parallel")),
        name="adaln",
    )(c_rep, w, b)


def _inproj_kernel(x_ref, sc_ref, sh_ref, w_ref, cos_ref, sin_ref, o_ref, h_scr, *, n_rot):
    j = pl.program_id(1)

    @pl.when(j == 0)
    def _():
        h_scr[...] = (x_ref[...] * (1.0 + sc_ref[...]) + sh_ref[...]).astype(BF16)

    y = jnp.dot(h_scr[...], w_ref[...], preferred_element_type=F32)

    @pl.when(j < n_rot)
    def _():
        half = ROT_DIM // 2
        cs, sn = cos_ref[...], sin_ref[...]
        pos = lax.broadcasted_iota(jnp.int32, sn.shape, 1) % DIFF_QK_DIM
        sa = jnp.where(pos >= half, sn, 0.0)
        sb = jnp.where(pos < half, -sn, 0.0)
        for g in range(y.shape[1] // LANES):
            yg = y[:, g * LANES:(g + 1) * LANES]
            rot = (yg * cs + pltpu.roll(yg, ROT_DIM // 2, 1) * sa
                   + pltpu.roll(yg, LANES - ROT_DIM // 2, 1) * sb)
            o_ref[:, g * LANES:(g + 1) * LANES] = rot.astype(o_ref.dtype)

    @pl.when(j >= n_rot)
    def _():
        o_ref[...] = y.astype(o_ref.dtype)


def _inproj(x, scale, shift, w, col0, n, rot_tabs, n_rot, out_dtype, tm, tn):
    L = x.shape[0]
    tm = min(tm, L)
    j0 = col0 // tn
    return pl.pallas_call(
        functools.partial(_inproj_kernel, n_rot=n_rot),
        grid=(L // tm, n // tn),
        in_specs=[pl.BlockSpec((tm, D_MODEL), lambda i, j: (i, 0)),
                  pl.BlockSpec((1, D_MODEL), lambda i, j: (0, 0)),
                  pl.BlockSpec((1, D_MODEL), lambda i, j: (0, 0)),
                  pl.BlockSpec((D_MODEL, tn), lambda i, j: (0, j0 + j)),
                  pl.BlockSpec((tm, LANES), lambda i, j: (i, 0)),
                  pl.BlockSpec((tm, LANES), lambda i, j: (i, 0))],
        out_specs=pl.BlockSpec((tm, tn), lambda i, j: (i, j)),
        out_shape=jax.ShapeDtypeStruct((L, n), out_dtype),
        scratch_shapes=[pltpu.VMEM((tm, D_MODEL), BF16)],
        compiler_params=_cp(("parallel", "arbitrary")),
        name="inproj",
    )(x, scale, shift, w, *rot_tabs)


def _rotary_tables(L):
    half = ROT_DIM // 2
    pos = jnp.arange(L, dtype=F32)
    inv_freq = 1.0 / (ROPE_THETA ** (jnp.arange(0, ROT_DIM, 2, dtype=F32) / ROT_DIM))
    lane = np.arange(LANES) % DIFF_QK_DIM
    rotated = jnp.asarray(lane < ROT_DIM)[None, :]
    ang = pos[:, None] * inv_freq[lane % half][None, :]
    return jnp.where(rotated, jnp.cos(ang), 1.0), jnp.where(rotated, jnp.sin(ang), 0.0)


def _attn_kernel(lam_ref, q_ref, k_ref, v_ref, g_ref, o_ref, q2_scr, m_scr, acc_scr, s_scr,
                 *, tq, tk, out_scale):
    n_rows = 2 * tq
    n_chunks = n_rows // ATTN_ROWS
    n_kv = k_ref.shape[0] // tk
    q = q_ref[...]
    lane = lax.broadcasted_iota(jnp.int32, q.shape, 1)
    qs = q * (DIFF_QK_DIM ** -0.5)
    q2_scr[0:tq, :] = jnp.where(lane < DIFF_QK_DIM, qs, 0).astype(BF16)
    q2_scr[tq:n_rows, :] = jnp.where(lane >= DIFF_QK_DIM, qs, 0).astype(BF16)
    m_scr[...] = jnp.full(m_scr.shape, -jnp.inf, F32)
    acc_scr[...] = jnp.zeros(acc_scr.shape, F32)
    ones = jnp.ones((tk, LANES), BF16)
    n_col = tk // LANES

    def scores(r, kc):
        rs = slice(r * ATTN_ROWS, (r + 1) * ATTN_ROWS)
        return lax.dot_general(q2_scr[rs, :], kc, (((1,), (1,)), ((), ())), preferred_element_type=F32)

    kc0 = k_ref[0:tk, :]
    for r in range(n_chunks):
        s_scr[r * ATTN_ROWS:(r + 1) * ATTN_ROWS, :] = scores(r, kc0)

    def step(j, prefetch):
        c0 = pl.multiple_of(j * tk, tk)
        if prefetch:
            k_next = k_ref[pl.ds(pl.multiple_of((j + 1) * tk, tk), tk), :]
        v1 = jnp.concatenate([v_ref[pl.ds(c0, tk), :], ones], axis=1)
        for r in range(n_chunks):
            rs = slice(r * ATTN_ROWS, (r + 1) * ATTN_ROWS)
            s = s_scr[rs, :]
            if prefetch:
                s_scr[rs, :] = scores(r, k_next)
            mx = s[:, 0:LANES]
            for c in range(1, n_col):
                mx = jnp.maximum(mx, s[:, c * LANES:(c + 1) * LANES])
            m_prev = m_scr[rs, :]
            m_new = jnp.maximum(m_prev, jnp.max(mx, axis=1, keepdims=True))
            alpha = jnp.exp(m_prev - m_new)
            p = jnp.exp(s - jnp.concatenate([m_new] * n_col, axis=1)).astype(BF16)
            pv = jnp.dot(p, v1, preferred_element_type=F32)
            acc_scr[rs, :] = jnp.concatenate([alpha, alpha], axis=1) * acc_scr[rs, :] + pv
            m_scr[rs, :] = m_new

    def body(j, carry):
        step(j, True)
        return carry

    lax.fori_loop(0, n_kv - 1, body, 0)
    step(n_kv - 1, False)

    acc = acc_scr[...]
    o2 = acc[:, 0:LANES] / acc[:, LANES:2 * LANES]
    o = o2[0:tq, :] - lam_ref[0] * o2[tq:n_rows, :]
    ms = jnp.mean(o * o, axis=-1, keepdims=True)
    o_ref[...] = (o * lax.rsqrt(ms + RMS_EPS) * g_ref[...] * out_scale).astype(o_ref.dtype)


def _diff_attention(qkv, lam, subln_g, lam_init, tq, tk):
    L = qkv.shape[0]
    tq, tk = min(tq, L), min(tk, L)
    H = DIFF_HEADS
    return pl.pallas_call(
        functools.partial(_attn_kernel, tq=tq, tk=tk, out_scale=1.0 - lam_init),
        grid=(H, L // tq),
        in_specs=[pl.BlockSpec(memory_space=pltpu.SMEM),
                  pl.BlockSpec((tq, LANES), lambda h, i: (i, h)),
                  pl.BlockSpec((L, LANES), lambda h, i: (0, H + h)),
                  pl.BlockSpec((L, LANES), lambda h, i: (0, 2 * H + h)),
                  pl.BlockSpec((1, LANES), lambda h, i: (0, 0))],
        out_specs=pl.BlockSpec((tq, LANES), lambda h, i: (i, h)),
        out_shape=jax.ShapeDtypeStruct((L, DIFF_WIDTH), BF16),
        scratch_shapes=[pltpu.VMEM((2 * tq, LANES), BF16),
                        pltpu.VMEM((2 * tq, LANES), F32),
                        pltpu.VMEM((2 * tq, 2 * LANES), F32),
                        pltpu.VMEM((2 * tq, tk), F32)],
        compiler_params=_cp(("parallel", "parallel")),
        name="diff_attn",
    )(lam, qkv, qkv, qkv, subln_g)


def _hg_masks():
    C = HG_TILE
    t = np.arange(C)[:, None]
    s = np.arange(C)[None, :]
    out = []
    for reverse in (False, True):
        for h in HG_LEVELS:
            same = (t // (2 * h)) == (s // (2 * h))
            t_late = (t % (2 * h)) >= h
            s_late = (s % (2 * h)) >= h
            m = same & (~t_late & s_late if reverse else t_late & ~s_late)
            out.append(m)
        for d in range(HG_BAND):
            same = (t // HG_BAND) == (s // HG_BAND)
            out.append(same & ((s == t + d) if reverse else (s == t - d)))
    return jnp.asarray(np.stack(out).astype(np.float32))


def _hg_tri():
    C = HG_TILE
    t = np.arange(C)[:, None]
    r = np.arange(C)[None, :]
    return jnp.asarray(np.stack([(r <= t), (r >= t)]).astype(np.float32), dtype=BF16)


def _hg_tile(qr, fr, v, lb, st, tri, mask_ref, reverse):
    C = HG_TILE
    n_lvl = len(HG_LEVELS)
    m0 = (n_lvl + HG_BAND) if reverse else 0
    q = qr * jax.nn.sigmoid(qr)
    f = lb + (1.0 - lb) * jax.nn.sigmoid(fr)
    k = 1.0 - f
    g = jnp.log(f)
    g1 = g.astype(BF16)
    r1 = g - g1.astype(F32)
    g2 = r1.astype(BF16)
    g3 = (r1 - g2.astype(F32)).astype(BF16)
    b3 = jnp.dot(tri, jnp.concatenate([g1, g2, g3], axis=1), preferred_element_type=F32)
    b = b3[:, 0:LANES] + b3[:, LANES:2 * LANES] + b3[:, 2 * LANES:3 * LANES]
    b_edge = b[0:1, :] if reverse else b[C - 1:C, :]
    vb = v.astype(BF16)

    qc = (q * jnp.exp(b)).astype(BF16)
    kc = (k * jnp.exp(b_edge - b)).astype(BF16)
    o = lax.dot_general(qc, st.astype(BF16), (((1,), (1,)), ((), ())), preferred_element_type=F32)
    st_new = st * jnp.exp(b_edge) + jnp.dot(v.T.astype(BF16), kc, preferred_element_type=F32)

    a = jnp.zeros((C, C), F32)
    for li, h in enumerate(HG_LEVELS):
        rows = []
        for blk in range(C // (2 * h)):
            r = blk * 2 * h + (h if reverse else h - 1)
            rows.append(jnp.broadcast_to(b[r:r + 1, :], (2 * h, LANES)))
        bref = rows[0] if len(rows) == 1 else jnp.concatenate(rows, axis=0)
        e = jnp.exp(-jnp.abs(b - bref))
        ah = lax.dot_general((q * e).astype(BF16), (k * e).astype(BF16), (((1,), (1,)), ((), ())),
                             preferred_element_type=F32)
        a = a + ah * mask_ref[m0 + li]
    def roll_rows(x, shift):
        x3 = pltpu.roll(x.reshape(C // HG_BAND, HG_BAND, LANES), shift % HG_BAND, 1)
        return x3.reshape(C, LANES)
    ed = None
    for d in range(HG_BAND):
        if d == 0:
            p = q * k
        else:
            sh = (C - d) if reverse else d
            shf = (C - (d - 1)) % C if reverse else d - 1
            fsh = f if shf == 0 else roll_rows(f, shf)
            ed = fsh if ed is None else ed * fsh
            p = q * roll_rows(k, sh) * ed
        a = a + jnp.sum(p, axis=1, keepdims=True) * mask_ref[m0 + n_lvl + d]
    o = o + jnp.dot(a.astype(BF16), vb, preferred_element_type=F32)
    return o, st_new


def _hgrn_kernel(qf_ref, qb_ref, ff_ref, fb_ref, vf_ref, vb_ref, lb_ref, tri_ref, mask_ref,
                 of_ref, ob_ref, stf_scr, stb_scr, *, n_tiles):
    @pl.when(pl.program_id(1) == 0)
    def _():
        stf_scr[...] = jnp.zeros(stf_scr.shape, F32)
        stb_scr[...] = jnp.zeros(stb_scr.shape, F32)

    C = HG_TILE
    lbf = lb_ref[0:1, :]
    lbb = lb_ref[1:2, :]

    def body(t, carry):
        rf = pl.multiple_of(t * C, C)
        rb = pl.multiple_of((n_tiles - 1 - t) * C, C)
        o_f, st_f = _hg_tile(qf_ref[pl.ds(rf, C), :], ff_ref[pl.ds(rf, C), :], vf_ref[pl.ds(rf, C), :],
                             lbf, stf_scr[...], tri_ref[0], mask_ref, False)
        of_ref[pl.ds(rf, C), :] = o_f
        stf_scr[...] = st_f
        o_b, st_b = _hg_tile(qb_ref[pl.ds(rb, C), :], fb_ref[pl.ds(rb, C), :], vb_ref[pl.ds(rb, C), :],
                             lbb, stb_scr[...], tri_ref[1], mask_ref, True)
        ob_ref[pl.ds(rb, C), :] = o_b
        stb_scr[...] = st_b
        return carry

    lax.fori_loop(0, n_tiles, body, 0, unroll=True)


def _hgrn2(proj, lb, rows):
    L = proj.shape[0]
    rows = min(rows, L)
    nb = L // rows
    H = HG_HEADS
    nm = len(HG_LEVELS) + HG_BAND
    blk = lambda col0, rev: pl.BlockSpec(
        (rows, LANES), (lambda h, i: (nb - 1 - i, col0 + h)) if rev else (lambda h, i: (i, col0 + h)))
    return pl.pallas_call(
        functools.partial(_hgrn_kernel, n_tiles=rows // HG_TILE),
        grid=(H, nb),
        in_specs=[blk(0, False), blk(0, True), blk(H, False), blk(2 * H, True),
                  blk(3 * H, False), blk(3 * H, True),
                  pl.BlockSpec((2, LANES), lambda h, i: (0, h)),
                  pl.BlockSpec((2, HG_TILE, HG_TILE), lambda h, i: (0, 0, 0)),
                  pl.BlockSpec((2 * nm, HG_TILE, HG_TILE), lambda h, i: (0, 0, 0))],
        out_specs=[blk(0, False), blk(0, True)],
        out_shape=[jax.ShapeDtypeStruct((L, HG_WIDTH), F32)] * 2,
        scratch_shapes=[pltpu.VMEM((LANES, LANES), F32)] * 2,
        compiler_params=_cp(("parallel", "arbitrary")),
        name="hgrn2",
    )(proj, proj, proj, proj, proj, proj, lb, _hg_tri(), _hg_masks())


def _outproj_kernel(x_ref, oa_ref, of_ref, ob_ref, gr_ref, ng_ref, gate_ref, lng_ref, lnb_ref, w_ref,
                    o_ref, lhs_scr):
    lhs_scr[:, 0:DIFF_WIDTH] = oa_ref[...]
    ng = ng_ref[...]
    for h in range(HG_HEADS):
        sl = slice(h * LANES, (h + 1) * LANES)
        o = of_ref[:, sl] + ob_ref[:, sl]
        ms = jnp.mean(o * o, axis=-1, keepdims=True)
        gr = gr_ref[:, sl]
        hg = o * lax.rsqrt(ms + RMS_EPS) * ng * (gr * jax.nn.sigmoid(gr))
        lhs_scr[:, DIFF_WIDTH + h * LANES:DIFF_WIDTH + (h + 1) * LANES] = hg.astype(BF16)
    y = jnp.dot(lhs_scr[...], w_ref[...], preferred_element_type=F32)
    z = ALPHA * x_ref[...] + (1.0 + gate_ref[...]) * y
    o_ref[...] = _layer_norm_rows(z, lng_ref[...], lnb_ref[...])


def _outproj_ln(x, oa, of, ob, proj_hg, norm_g, gate, ln_g, ln_b, w, tm):
    L = x.shape[0]
    tm = min(tm, L)
    row = lambda i: (i, 0)
    vec = pl.BlockSpec((1, D_MODEL), lambda i: (0, 0))
    return pl.pallas_call(
        _outproj_kernel,
        grid=(L // tm,),
        in_specs=[pl.BlockSpec((tm, D_MODEL), row),
                  pl.BlockSpec((tm, DIFF_WIDTH), row),
                  pl.BlockSpec((tm, HG_WIDTH), row),
                  pl.BlockSpec((tm, HG_WIDTH), row),
                  pl.BlockSpec((tm, HG_WIDTH), lambda i: (i, 4)),
                  pl.BlockSpec((1, LANES), lambda i: (0, 0)),
                  vec, vec, vec,
                  pl.BlockSpec((D_MODEL, D_MODEL), lambda i: (0, 0))],
        out_specs=pl.BlockSpec((tm, D_MODEL), row),
        out_shape=jax.ShapeDtypeStruct((L, D_MODEL), F32),
        scratch_shapes=[pltpu.VMEM((tm, D_MODEL), BF16)],
        compiler_params=_cp(("parallel",)),
        name="outproj_ln",
    )(x, oa, of, ob, proj_hg, norm_g, gate, ln_g, ln_b, w)


def _mlp_kernel(x_ref, sc_ref, sh_ref, gate_ref, lng_ref, lnb_ref, w1_ref, w2_ref, o_ref, h_scr, acc_scr):
    j = pl.program_id(1)

    @pl.when(j == 0)
    def _():
        h_scr[...] = (x_ref[...] * (1.0 + sc_ref[...]) + sh_ref[...]).astype(BF16)
        acc_scr[...] = jnp.zeros(acc_scr.shape, F32)

    a = jnp.maximum(jnp.dot(h_scr[...], w1_ref[...], preferred_element_type=F32), 0.0)
    acc_scr[...] += jnp.dot((a * a).astype(BF16), w2_ref[...], preferred_element_type=F32)

    @pl.when(j == pl.num_programs(1) - 1)
    def _():
        z = ALPHA * x_ref[...] + (1.0 + gate_ref[...]) * acc_scr[...]
        o_ref[...] = _layer_norm_rows(z, lng_ref[...], lnb_ref[...])


def _mlp_ln(x, scale, shift, gate, ln_g, ln_b, w1, w2, tm, tf):
    L = x.shape[0]
    tm = min(tm, L)
    vec = pl.BlockSpec((1, D_MODEL), lambda i, j: (0, 0))
    return pl.pallas_call(
        _mlp_kernel,
        grid=(L // tm, D_FF // tf),
        in_specs=[pl.BlockSpec((tm, D_MODEL), lambda i, j: (i, 0)),
                  vec, vec, vec, vec, vec,
                  pl.BlockSpec((D_MODEL, tf), lambda i, j: (0, j)),
                  pl.BlockSpec((tf, D_MODEL), lambda i, j: (j, 0))],
        out_specs=pl.BlockSpec((tm, D_MODEL), lambda i, j: (i, 0)),
        out_shape=jax.ShapeDtypeStruct((L, D_MODEL), F32),
        scratch_shapes=[pltpu.VMEM((tm, D_MODEL), BF16), pltpu.VMEM((tm, D_MODEL), F32)],
        compiler_params=_cp(("parallel", "arbitrary")),
        name="mlp_ln",
    )(x, scale, shift, gate, ln_g, ln_b, w1, w2)


def _s5_operators(lam_re, lam_im, log_dt, b_re, b_im, c_re, c_im):
    T, G, P, Hc, NB, GPB = S5_T, S5_GROUPS, S5_STATE, S5_GROUP, S5_BLOCKS, S5_GPB
    dt = jnp.exp(log_dt)[..., None]
    zr, zi = lam_re * dt, lam_im * dt
    mag = jnp.exp(zr)
    ab_re, ab_im = mag * jnp.cos(zi), mag * jnp.sin(zi)
    den = lam_re * lam_re + lam_im * lam_im
    coef_re = ((ab_re - 1.0) * lam_re + ab_im * lam_im) / den
    coef_im = (ab_im * lam_re - (ab_re - 1.0) * lam_im) / den
    bb_re = coef_re[..., None] * b_re - coef_im[..., None] * b_im
    bb_im = coef_re[..., None] * b_im + coef_im[..., None] * b_re

    def apow(k):
        k = jnp.asarray(k, F32)[:, None, None, None]
        m = jnp.exp(zr[None] * k)
        return m * jnp.cos(zi[None] * k), m * jnp.sin(zi[None] * k)

    pr, pi = apow(np.arange(T))
    ca_re = c_re[None] * pr[:, :, :, None, :] - c_im[None] * pi[:, :, :, None, :]
    ca_im = c_re[None] * pi[:, :, :, None, :] + c_im[None] * pr[:, :, :, None, :]
    bt_re, bt_im = jnp.swapaxes(bb_re, 2, 3), jnp.swapaxes(bb_im, 2, 3)
    lagk = jnp.sum(ca_re[:, :, :, :, None, :] * bt_re[None, :, :, None, :, :]
                   - ca_im[:, :, :, :, None, :] * bt_im[None, :, :, None, :, :], axis=-1)
    kc = jnp.transpose(lagk.reshape(T, 2, NB, GPB, Hc, Hc), (2, 1, 0, 5, 3, 4)).reshape(NB, 2 * T, Hc, LANES)

    def gp_lanes(v):
        return jnp.transpose(v.reshape(2, NB, GPB, Hc, P), (1, 0, 3, 2, 4)).reshape(NB, 2, Hc, GPB * P)
    bt = jnp.stack([gp_lanes(bt_re), gp_lanes(bt_im)], axis=1)
    def e_tab(v):
        v = jnp.stack([v[::-1, 0], v[:, 1]], axis=1).reshape(T, 2, NB, GPB * P)
        return jnp.transpose(v, (2, 0, 1, 3))
    et = jnp.stack([e_tab(pr), e_tab(pi)], axis=2)[:, :, :, :, None, :]

    ct = jnp.stack([gp_lanes(c_re), gp_lanes(c_im)], axis=1)
    cr, ci = apow(np.arange(1, T + 1))
    def ec_tab(v):
        v = jnp.stack([v[:, 0], v[::-1, 1]], axis=1).reshape(T, 2, NB, GPB * P)
        return jnp.transpose(v, (2, 0, 1, 3))
    ec = jnp.stack([ec_tab(cr), ec_tab(ci)], axis=2)[:, :, :, :, None, :]
    return tuple(_s5_expand(kc, bt, et, ct, ec)) + ((zr, zi),)


def _s5_expand_kernel(kc_ref, bt_ref, et_ref, ct_ref, ec_ref, wt_ref, ws_ref, wc_ref):
    n_q = S5_GPB * S5_STATE // LANES
    gpq = S5_GPB // n_q
    row = lax.broadcasted_iota(jnp.int32, (LANES, LANES), 0)
    col = lax.broadcasted_iota(jnp.int32, (LANES, LANES), 1)
    over_groups = lambda a: jnp.concatenate([a] * S5_GPB, axis=0)
    same_group = (row // S5_GROUP) == (col // S5_GROUP)

    for u in range(S5_EXPAND_SLABS):
        s = pl.program_id(1) * S5_EXPAND_SLABS + u
        rows = slice(u * LANES, (u + 1) * LANES)
        for t in range(S5_T):
            lag = t - s
            k = kc_ref[0, jnp.where(lag >= 0, lag, S5_T - lag)] + jnp.where(lag == 0, kc_ref[0, S5_T], 0.0)
            wt_ref[0, rows, t * LANES:(t + 1) * LANES] = jnp.where(same_group, over_groups(k), 0.0).astype(BF16)

        for d in range(2):
            for q in range(n_q):
                sl = slice(q * LANES, (q + 1) * LANES)
                own = (row // S5_GROUP) == (gpq * q + col // S5_STATE)
                for m_ref, p_ref, o_ref, im_sign in ((bt_ref, et_ref, ws_ref, 1.0), (ct_ref, ec_ref, wc_ref, -1.0)):
                    m_re, m_im = m_ref[0, 0, d, :, sl], m_ref[0, 1, d, :, sl]
                    p_re, p_im = p_ref[0, u, 0, d, :, sl], p_ref[0, u, 1, d, :, sl]
                    vals = (p_re * m_re - p_im * m_im, im_sign * (p_re * m_im + p_im * m_re))
                    for ri in range(2):
                        c0 = ((d * n_q + q) * 2 + ri) * LANES
                        o_ref[0, rows, c0:c0 + LANES] = jnp.where(own, over_groups(vals[ri]), 0.0).astype(BF16)


def _s5_expand(kc, bt, et, ct, ec):
    NB, T, Hc, P, ns = S5_BLOCKS, S5_T, S5_GROUP, S5_STATE, S5_EXPAND_SLABS
    slab = pl.BlockSpec((1, ns * LANES, S5_ROW), lambda b, s: (b, s, 0))
    mat = pl.BlockSpec((1, 2, 2, Hc, S5_GPB * P), lambda b, s: (b, 0, 0, 0, 0))
    powers = pl.BlockSpec((1, ns, 2, 2, 1, S5_GPB * P), lambda b, s: (b, s, 0, 0, 0, 0))
    return pl.pallas_call(
        _s5_expand_kernel,
        grid=(NB, T // ns),
        in_specs=[pl.BlockSpec((1, 2 * T, Hc, LANES), lambda b, s: (b, 0, 0, 0)), mat, powers, mat, powers],
        out_specs=[slab, slab, slab],
        out_shape=[jax.ShapeDtypeStruct((NB, S5_ROW, S5_ROW), BF16)] * 3,
        compiler_params=_cp(("parallel", "arbitrary")),
        name="s5_expand",
    )(kc, bt, et, ct, ec)


def _s5_state_lanes(re, im):
    n = re.shape[0]
    x = jnp.stack([re, im], axis=0)
    x = x.reshape(2, n, 2, S5_BLOCKS, S5_GPB * S5_STATE // LANES, LANES)
    x = jnp.transpose(x, (2, 1, 3, 4, 0, 5))
    return x.reshape(2, n, S5_BLOCKS * S5_SW)


def _s5_scan_powers(zr, zi, n_steps):
    def power(k):
        k = jnp.asarray(k, F32)[:, :, None, None]
        m = jnp.exp(zr[None] * k)
        return m * jnp.cos(zi[None] * k), m * jnp.sin(zi[None] * k)
    k_dbl = S5_T * (2.0 ** np.arange(n_steps))
    j = np.arange(SUBLANES)
    return (_s5_state_lanes(*power(np.stack([k_dbl, k_dbl], axis=1))),
            _s5_state_lanes(*power(S5_T * np.stack([j, SUBLANES - 1 - j], axis=1))))


def _s5_fold(x_ref, sc, sh, tc):
    parts = [x_ref[pl.ds(t, tc, stride=S5_T), :] * (1.0 + sc) + sh for t in range(S5_T)]
    return parts


def _s5_state_kernel(x_ref, sc_ref, sh_ref, w_ref, zf_ref, zb_ref, *, tc):
    parts = _s5_fold(x_ref, sc_ref[...], sh_ref[...], tc)
    u = jnp.concatenate([p.astype(BF16) for p in parts], axis=1)
    z = jnp.dot(u, w_ref[0], preferred_element_type=F32)
    zf_ref[...] = z[:, :S5_SW]
    zb_ref[...] = z[:, S5_SW:]


def _s5_scan_kernel(zf_ref, zb_ref, af_ref, ab_ref, pf_ref, pb_ref, sf_ref, sb_ref, loc_scr,
                    *, n_rows, n_steps):
    n_blk = n_rows // SUBLANES
    n_local = int(math.log2(SUBLANES))
    sub = lax.broadcasted_iota(jnp.int32, (n_blk, SUBLANES, LANES), 1)
    blk = lax.broadcasted_iota(jnp.int32, (n_blk, LANES), 0)

    def cmul_add(re, im, ar, ai, rs, is_):
        return re + ar * rs - ai * is_, im + ar * is_ + ai * rs

    def scan(z_ref, a_ref, p_ref, reverse):
        re = z_ref[:, 0:LANES].reshape(n_blk, SUBLANES, LANES)
        im = z_ref[:, LANES:2 * LANES].reshape(n_blk, SUBLANES, LANES)

        def in_block(v, sh):
            if reverse:
                return jnp.where(sub < SUBLANES - sh, pltpu.roll(v, SUBLANES - sh, 1), 0.0)
            return jnp.where(sub >= sh, pltpu.roll(v, sh, 1), 0.0)

        def across(v, sh):
            if reverse:
                return jnp.where(blk < n_blk - sh, pltpu.roll(v, n_blk - sh, 0), 0.0)
            return jnp.where(blk >= sh, pltpu.roll(v, sh, 0), 0.0)

        for k in range(n_local):
            ar, ai = a_ref[0, k:k + 1, 0:LANES], a_ref[0, k:k + 1, LANES:2 * LANES]
            re, im = cmul_add(re, im, ar, ai, in_block(re, 2 ** k), in_block(im, 2 ** k))
        edge = 0 if reverse else SUBLANES - 1
        loc_scr[0] = re.reshape(n_rows, LANES)
        loc_scr[1] = im.reshape(n_rows, LANES)
        cr = loc_scr[0, pl.ds(edge, n_blk, stride=SUBLANES), :]
        ci = loc_scr[1, pl.ds(edge, n_blk, stride=SUBLANES), :]
        for k in range(n_local, n_steps):
            ar, ai = a_ref[0, k:k + 1, 0:LANES], a_ref[0, k:k + 1, LANES:2 * LANES]
            sh = 2 ** (k - n_local)
            cr, ci = cmul_add(cr, ci, ar, ai, across(cr, sh), across(ci, sh))
        cr, ci = across(cr, 1), across(ci, 1)
        pr = p_ref[0, :, 0:LANES][None]
        pi = p_ref[0, :, LANES:2 * LANES][None]
        cr3 = jnp.broadcast_to(cr[:, None, :], (n_blk, SUBLANES, LANES))
        ci3 = jnp.broadcast_to(ci[:, None, :], (n_blk, SUBLANES, LANES))
        ore, oim = cmul_add(in_block(re, 1), in_block(im, 1), pr, pi, cr3, ci3)
        return ore.reshape(n_rows, LANES), oim.reshape(n_rows, LANES)

    re, im = scan(zf_ref, af_ref, pf_ref, False)
    sf_ref[:, 0:LANES] = re.astype(BF16)
    sf_ref[:, LANES:2 * LANES] = im.astype(BF16)
    re, im = scan(zb_ref, ab_ref, pb_ref, True)
    sb_ref[:, 0:LANES] = re.astype(BF16)
    sb_ref[:, LANES:2 * LANES] = im.astype(BF16)


def _s5_out_kernel(x_ref, sc_ref, sh_ref, d_ref, sf_ref, sb_ref, wt_ref, wc_ref, o_ref, *, tc, n_half):
    j = pl.program_id(2)
    sc, sh, d = sc_ref[...], sh_ref[...], d_ref[...]
    parts = _s5_fold(x_ref, sc, sh, tc)
    u = jnp.concatenate([p.astype(BF16) for p in parts], axis=1)
    s = jnp.concatenate([sf_ref[...], sb_ref[...]], axis=1)
    y = (jnp.dot(u, wt_ref[0], preferred_element_type=F32)
         + lax.dot_general(s, wc_ref[0], (((1,), (1,)), ((), ())),
                           preferred_element_type=F32))
    tph = S5_T // n_half
    for jj in range(n_half):
        @pl.when(j == jj)
        def _():
            for tl in range(tph):
                t = jj * tph + tl
                o_ref[pl.ds(t, tc, stride=S5_T), :] = y[:, tl * LANES:(tl + 1) * LANES] + d * parts[t]


def _s5_mixer_core(x, scale, shift, d_skip, ops, tc):
    w_toep, w_state, w_carry, (zr, zi) = ops
    L = x.shape[0]
    n_rows = L // S5_T
    NB = S5_BLOCKS
    n_steps = int(math.log2(n_rows))
    assert 2 ** n_steps == n_rows
    tc = min(tc, n_rows)
    vecb = pl.BlockSpec((1, LANES), lambda b, i: (0, b))

    zf, zb = pl.pallas_call(
        functools.partial(_s5_state_kernel, tc=tc),
        grid=(NB, n_rows // tc),
        in_specs=[pl.BlockSpec((tc * S5_T, LANES), lambda b, i: (i, b)), vecb, vecb,
                  pl.BlockSpec((1, S5_ROW, 2 * S5_SW), lambda b, i: (b, 0, 0))],
        out_specs=[pl.BlockSpec((tc, S5_SW), lambda b, i: (i, b))] * 2,
        out_shape=[jax.ShapeDtypeStruct((n_rows, NB * S5_SW), F32)] * 2,
        compiler_params=_cp(("parallel", "parallel")),
        name="s5_state",
    )(x, scale, shift, w_state)

    pw, pj = _s5_scan_powers(zr, zi, n_steps)
    cw = 2 * LANES
    col = lambda n: (0, n)
    sf, sb = pl.pallas_call(
        functools.partial(_s5_scan_kernel, n_rows=n_rows, n_steps=n_steps),
        grid=(NB * S5_SW // cw,),
        in_specs=[pl.BlockSpec((n_rows, cw), col),
                  pl.BlockSpec((n_rows, cw), col),
                  pl.BlockSpec((1, n_steps, cw), lambda n: (0, 0, n)),
                  pl.BlockSpec((1, n_steps, cw), lambda n: (1, 0, n)),
                  pl.BlockSpec((1, SUBLANES, cw), lambda n: (0, 0, n)),
                  pl.BlockSpec((1, SUBLANES, cw), lambda n: (1, 0, n))],
        out_specs=[pl.BlockSpec((n_rows, cw), col)] * 2,
        out_shape=[jax.ShapeDtypeStruct((n_rows, NB * S5_SW), BF16)] * 2,
        scratch_shapes=[pltpu.VMEM((2, n_rows, LANES), F32)],
        compiler_params=_cp(("parallel",)),
        name="s5_scan",
    )(zf, zb, pw, pw, pj, pj)

    n_half = 2
    nh = S5_ROW // n_half
    vec3 = pl.BlockSpec((1, LANES), lambda b, i, j: (0, b))
    return pl.pallas_call(
        functools.partial(_s5_out_kernel, tc=tc, n_half=n_half),
        grid=(NB, n_rows // tc, n_half),
        in_specs=[pl.BlockSpec((tc * S5_T, LANES), lambda b, i, j: (i, b)), vec3, vec3, vec3,
                  pl.BlockSpec((tc, S5_SW), lambda b, i, j: (i, b)),
                  pl.BlockSpec((tc, S5_SW), lambda b, i, j: (i, b)),
                  pl.BlockSpec((1, S5_ROW, nh), lambda b, i, j: (b, 0, j)),
                  pl.BlockSpec((1, nh, 2 * S5_SW), lambda b, i, j: (b, j, 0))],
        out_specs=pl.BlockSpec((tc * S5_T, LANES), lambda b, i, j: (i, b)),
        out_shape=jax.ShapeDtypeStruct((L, D_MODEL), F32),
        compiler_params=_cp(("parallel", "parallel", "arbitrary")),
        name="s5_out",
    )(x, scale, shift, d_skip, sf, sb, w_toep, w_carry)


def _glu_kernel(x_ref, y_ref, gate_ref, lng_ref, lnb_ref, wa_ref, wg_ref, o_ref, lhs_scr, mix_scr, *, tn):
    j = pl.program_id(1)

    @pl.when(j == 0)
    def _():
        lhs_scr[...] = jax.nn.gelu(y_ref[...]).astype(BF16)

    a = jnp.dot(lhs_scr[...], wa_ref[...], preferred_element_type=F32)
    g = jnp.dot(lhs_scr[...], wg_ref[...], preferred_element_type=F32)
    mix = a * jax.nn.sigmoid(g)
    for jj in range(D_MODEL // tn):
        @pl.when(j == jj)
        def _():
            mix_scr[:, jj * tn:(jj + 1) * tn] = mix

    @pl.when(j == pl.num_programs(1) - 1)
    def _():
        z = ALPHA * x_ref[...] + (1.0 + gate_ref[...]) * mix_scr[...]
        o_ref[...] = _layer_norm_rows(z, lng_ref[...], lnb_ref[...])


def _glu_ln(x, y, gate, ln_g, ln_b, w_glu, tm, tn):
    L = x.shape[0]
    tm = min(tm, L)
    nj = D_MODEL // tn
    vec = pl.BlockSpec((1, D_MODEL), lambda i, j: (0, 0))
    return pl.pallas_call(
        functools.partial(_glu_kernel, tn=tn),
        grid=(L // tm, nj),
        in_specs=[pl.BlockSpec((tm, D_MODEL), lambda i, j: (i, 0)),
                  pl.BlockSpec((tm, D_MODEL), lambda i, j: (i, 0)),
                  vec, vec, vec,
                  pl.BlockSpec((D_MODEL, tn), lambda i, j: (0, j)),
                  pl.BlockSpec((D_MODEL, tn), lambda i, j: (0, nj + j))],
        out_specs=pl.BlockSpec((tm, D_MODEL), lambda i, j: (i, 0)),
        out_shape=jax.ShapeDtypeStruct((L, D_MODEL), F32),
        scratch_shapes=[pltpu.VMEM((tm, D_MODEL), BF16), pltpu.VMEM((tm, D_MODEL), F32)],
        compiler_params=_cp(("parallel", "arbitrary")),
        name="glu_ln",
    )(x, y, gate, ln_g, ln_b, w_glu, w_glu)


def kernel(x, c, ada_w, ada_b, ln_g, ln_b, mix_w_in, mix_w_out, diff_lambda, diff_subln_g, hg_lower_bound,
           hg_norm_g, s5_lambda_re, s5_lambda_im, s5_log_dt, s5_b_re, s5_b_im, s5_c_re, s5_c_im, s5_d,
           s5_w_glu, mlp_w1, mlp_w2):
    B, L, D = x.shape
    assert B == 1 and D == D_MODEL
    xs = x.reshape(L, D)
    mod = _adaln(c, ada_w, ada_b)
    mods = lambda a: (mod[a, :, 0:D], mod[a, :, D:2 * D], mod[a, :, 2 * D:3 * D])
    lb_table = jnp.cumsum(jax.nn.softmax(hg_lower_bound.astype(F32), axis=1), axis=1)
    rot_tabs = _rotary_tables(L)

    for layer in range(DEPTH):
        shift, scale, gate = mods(2 * layer)
        lng, lnb = ln_g[layer, 0][None], ln_b[layer, 0][None]
        if layer % 2 == 0:
            e = layer // 2
            lam_init = 0.8 - 0.6 * math.exp(-0.3 * layer)
            w_in = _to_bf16(mix_w_in, e)
            qkv = _inproj(xs, scale, shift, w_in, 0, 3 * DIFF_WIDTH, rot_tabs, 2, BF16, INPROJ_TM, INPROJ_TN)
            proj_hg = _inproj(xs, scale, shift, w_in, 3 * DIFF_WIDTH, 5 * HG_WIDTH, rot_tabs, 0, F32,
                              INPROJ_TM, INPROJ_TN)
            lf = diff_lambda[e].astype(F32)
            lam = jnp.exp(jnp.sum(lf[0] * lf[1])) - jnp.exp(jnp.sum(lf[2] * lf[3])) + lam_init
            oa = _diff_attention(qkv, lam.reshape(1), diff_subln_g[e][None], lam_init, ATTN_TQ, ATTN_TK)
            of, ob = _hgrn2(proj_hg, lb_table[:, layer], HG_ROWS)
            xs = _outproj_ln(xs, oa, of, ob, proj_hg, hg_norm_g[e][None], gate, lng, lnb,
                             _to_bf16(mix_w_out, e), OUTPROJ_TM)
        else:
            o = layer // 2
            ops = _s5_operators(s5_lambda_re[o], s5_lambda_im[o], s5_log_dt[o], s5_b_re[o], s5_b_im[o],
                                s5_c_re[o], s5_c_im[o])
            y = _s5_mixer_core(xs, scale, shift, s5_d[o][None], ops, S5_TC)
            xs = _glu_ln(xs, y, gate, lng, lnb, _to_bf16(s5_w_glu, o), GLU_TM, GLU_TN)
        shift, scale, gate = mods(2 * layer + 1)
        xs = _mlp_ln(xs, scale, shift, gate, ln_g[layer, 1][None], ln_b[layer, 1][None],
                     _to_bf16(mlp_w1, layer), _to_bf16(mlp_w2, layer), MLP_TM, MLP_TF)
    return xs.reshape(B, L, D)
```

```python
import functools
import math

import numpy as np
import jax
import jax.numpy as jnp
from jax import lax
from jax.experimental import pallas as pl
from jax.experimental.pallas import tpu as pltpu

F32 = jnp.float32
BF16 = jnp.bfloat16

D_MODEL = 2048
DEPTH = 2
LANES = 128
SUBLANES = 8

DIFF_HEADS = 8
DIFF_QK_DIM = 64
DIFF_WIDTH = 1024
ROT_DIM = DIFF_QK_DIM // 4
ROPE_THETA = 500000.0
ATTN_ROWS = 128

HG_HEADS = 8
HG_WIDTH = 1024
HG_TILE = 128
HG_BAND = SUBLANES
HG_LEVELS = (64, 32, 16, 8)

S5_GROUP = 16
S5_GROUPS = D_MODEL // S5_GROUP
S5_STATE = 64
S5_T = 16
S5_GPB = LANES // S5_GROUP
S5_BLOCKS = D_MODEL // LANES
S5_ROW = S5_T * LANES
S5_SW = S5_GPB * S5_STATE * 2

D_FF = 4 * D_MODEL
ALPHA = (2 * DEPTH) ** 0.25
LN_EPS = 1e-5
RMS_EPS = 1e-6

VMEM_LIMIT = 56 * 1024 * 1024
CAST_BLOCK_ELEMS = 2 * 1024 * 1024

ADALN_TN = 768
INPROJ_TM, INPROJ_TN = 1024, 1024
ATTN_TQ, ATTN_TK = 2048, 1024
HG_ROWS = 1024
OUTPROJ_TM = 256
MLP_TM, MLP_TF = 512, 1024
S5_TC = 512
S5_EXPAND_SLABS = 4
GLU_TM, GLU_TN = 512, 1024


def _cp(sem):
    return pltpu.CompilerParams(dimension_semantics=sem, vmem_limit_bytes=VMEM_LIMIT)


def _bf16_kernel(w_ref, o_ref):
    o_ref[...] = w_ref[0].astype(BF16)


def _to_bf16(w, layer):
    _, r, c = w.shape
    tr = max(SUBLANES, CAST_BLOCK_ELEMS // c)
    assert r % tr == 0
    return pl.pallas_call(
        _bf16_kernel, grid=(r // tr,),
        in_specs=[pl.BlockSpec((1, tr, c), lambda i: (layer, i, 0))],
        out_specs=pl.BlockSpec((tr, c), lambda i: (i, 0)),
        out_shape=jax.ShapeDtypeStruct((r, c), BF16), compiler_params=_cp(("parallel",)), name="to_bf16",
    )(w)


def _layer_norm_rows(z, g, b):
    mu = jnp.mean(z, axis=-1, keepdims=True)
    zc = z - mu
    var = jnp.mean(zc * zc, axis=-1, keepdims=True)
    return zc * lax.rsqrt(var + LN_EPS) * g + b


def _adaln_kernel(c_ref, w_ref, b_ref, o_ref):
    c = c_ref[...]
    ca = c * jax.nn.sigmoid(c)
    w = w_ref[0]
    reps = w.shape[1] // LANES
    cb = jnp.concatenate([ca] * reps, axis=1)
    o_ref[0] = jnp.sum(w * cb, axis=0, keepdims=True) + b_ref[0]


def _adaln(c, ada_w, ada_b):
    n = 3 * D_MODEL
    tn = ADALN_TN
    w = ada_w.reshape(2 * DEPTH, D_MODEL, n)
    b = ada_b.reshape(2 * DEPTH, 1, n)
    c_rep = jnp.broadcast_to(c.reshape(D_MODEL, 1), (D_MODEL, LANES))
    return pl.pallas_call(
        _adaln_kernel,
        grid=(2 * DEPTH, n // tn),
        in_specs=[pl.BlockSpec((D_MODEL, LANES), lambda a, j: (0, 0)),
                  pl.BlockSpec((1, D_MODEL, tn), lambda a, j: (a, 0, j)),
                  pl.BlockSpec((1, 1, tn), lambda a, j: (a, 0, j))],
        out_specs=pl.BlockSpec((1, 1, tn), lambda a, j: (a, 0, j)),
        out_shape=jax.ShapeDtypeStruct((2 * DEPTH, 1, n), F32),
        compiler_params=_cp(("parallel", "parallel")),
        name="adaln",
    )(c_rep, w, b)


def _inproj_kernel(x_ref, sc_ref, sh_ref, w_ref, cos_ref, sin_ref, oa_ref, og_ref, h_scr, *, n_rot, n_a):
    j = pl.program_id(1)

    @pl.when(j == 0)
    def _():
        h_scr[...] = (x_ref[...] * (1.0 + sc_ref[...]) + sh_ref[...]).astype(BF16)

    y = jnp.dot(h_scr[...], w_ref[...], preferred_element_type=F32)

    @pl.when(j < n_rot)
    def _():
        half = ROT_DIM // 2
        cs, sn = cos_ref[...], sin_ref[...]
        pos = lax.broadcasted_iota(jnp.int32, sn.shape, 1) % DIFF_QK_DIM
        sa = jnp.where(pos >= half, sn, 0.0)
        sb = jnp.where(pos < half, -sn, 0.0)
        for g in range(y.shape[1] // LANES):
            yg = y[:, g * LANES:(g + 1) * LANES]
            rot = (yg * cs + pltpu.roll(yg, ROT_DIM // 2, 1) * sa
                   + pltpu.roll(yg, LANES - ROT_DIM // 2, 1) * sb)
            oa_ref[:, g * LANES:(g + 1) * LANES] = rot.astype(oa_ref.dtype)

    @pl.when((j >= n_rot) & (j < n_a))
    def _():
        oa_ref[...] = y.astype(oa_ref.dtype)

    @pl.when(j >= n_a)
    def _():
        og_ref[...] = y


def _inproj(x, scale, shift, w, n_a_cols, rot_tabs, n_rot, tm, tn):
    L = x.shape[0]
    n = w.shape[1]
    tm = min(tm, L)
    n_a = n_a_cols // tn
    return pl.pallas_call(
        functools.partial(_inproj_kernel, n_rot=n_rot, n_a=n_a),
        grid=(L // tm, n // tn),
        in_specs=[pl.BlockSpec((tm, D_MODEL), lambda i, j: (i, 0)),
                  pl.BlockSpec((1, D_MODEL), lambda i, j: (0, 0)),
                  pl.BlockSpec((1, D_MODEL), lambda i, j: (0, 0)),
                  pl.BlockSpec((D_MODEL, tn), lambda i, j: (0, j)),
                  pl.BlockSpec((tm, LANES), lambda i, j: (i, 0)),
                  pl.BlockSpec((tm, LANES), lambda i, j: (i, 0))],
        out_specs=[pl.BlockSpec((tm, tn), lambda i, j: (i, jnp.minimum(j, n_a - 1))),
                   pl.BlockSpec((tm, tn), lambda i, j: (i, jnp.maximum(j - n_a, 0)))],
        out_shape=[jax.ShapeDtypeStruct((L, n_a_cols), BF16),
                   jax.ShapeDtypeStruct((L, n - n_a_cols), F32)],
        scratch_shapes=[pltpu.VMEM((tm, D_MODEL), BF16)],
        compiler_params=_cp(("parallel", "arbitrary")),
        name="inproj",
    )(x, scale, shift, w, *rot_tabs)


def _rotary_tables(L):
    half = ROT_DIM // 2
    pos = jnp.arange(L, dtype=F32)
    inv_freq = 1.0 / (ROPE_THETA ** (jnp.arange(0, ROT_DIM, 2, dtype=F32) / ROT_DIM))
    lane = np.arange(LANES) % DIFF_QK_DIM
    rotated = jnp.asarray(lane < ROT_DIM)[None, :]
    ang = pos[:, None] * inv_freq[lane % half][None, :]
    return jnp.where(rotated, jnp.cos(ang), 1.0), jnp.where(rotated, jnp.sin(ang), 0.0)


def _attn_kernel(lam_ref, q_ref, k_ref, v_ref, g_ref, o_ref, q2_scr, m_scr, acc_scr, s_scr,
                 *, tq, tk, out_scale):
    n_rows = 2 * tq
    n_chunks = n_rows // ATTN_ROWS
    n_kv = k_ref.shape[0] // tk
    q = q_ref[...]
    lane = lax.broadcasted_iota(jnp.int32, q.shape, 1)
    qs = q * (DIFF_QK_DIM ** -0.5)
    q2_scr[0:tq, :] = jnp.where(lane < DIFF_QK_DIM, qs, 0).astype(BF16)
    q2_scr[tq:n_rows, :] = jnp.where(lane >= DIFF_QK_DIM, qs, 0).astype(BF16)
    m_scr[...] = jnp.full(m_scr.shape, -jnp.inf, F32)
    acc_scr[...] = jnp.zeros(acc_scr.shape, F32)
    ones = jnp.ones((tk, LANES), BF16)
    n_col = tk // LANES

    def scores(r, kc):
        rs = slice(r * ATTN_ROWS, (r + 1) * ATTN_ROWS)
        return lax.dot_general(q2_scr[rs, :], kc, (((1,), (1,)), ((), ())), preferred_element_type=F32)

    kc0 = k_ref[0:tk, :]
    for r in range(n_chunks):
        s_scr[r * ATTN_ROWS:(r + 1) * ATTN_ROWS, :] = scores(r, kc0)

    def step(j, prefetch):
        c0 = pl.multiple_of(j * tk, tk)
        if prefetch:
            k_next = k_ref[pl.ds(pl.multiple_of((j + 1) * tk, tk), tk), :]
        v1 = jnp.concatenate([v_ref[pl.ds(c0, tk), :], ones], axis=1)
        for r in range(n_chunks):
            rs = slice(r * ATTN_ROWS, (r + 1) * ATTN_ROWS)
            s = s_scr[rs, :]
            if prefetch:
                s_scr[rs, :] = scores(r, k_next)
            mx = s[:, 0:LANES]
            for c in range(1, n_col):
                mx = jnp.maximum(mx, s[:, c * LANES:(c + 1) * LANES])
            m_prev = m_scr[rs, :]
            m_new = jnp.maximum(m_prev, jnp.max(mx, axis=1, keepdims=True))
            alpha = jnp.exp(m_prev - m_new)
            p = jnp.exp(s - jnp.concatenate([m_new] * n_col, axis=1)).astype(BF16)
            pv = jnp.dot(p, v1, preferred_element_type=F32)
            acc_scr[rs, :] = jnp.concatenate([alpha, alpha], axis=1) * acc_scr[rs, :] + pv
            m_scr[rs, :] = m_new

    def body(j, carry):
        step(j, True)
        return carry

    lax.fori_loop(0, n_kv - 1, body, 0)
    step(n_kv - 1, False)

    acc = acc_scr[...]
    o2 = acc[:, 0:LANES] / acc[:, LANES:2 * LANES]
    o = o2[0:tq, :] - lam_ref[0] * o2[tq:n_rows, :]
    ms = jnp.mean(o * o, axis=-1, keepdims=True)
    o_ref[...] = (o * lax.rsqrt(ms + RMS_EPS) * g_ref[...] * out_scale).astype(o_ref.dtype)


def _diff_attention(qkv, lam, subln_g, lam_init, tq, tk):
    L = qkv.shape[0]
    tq, tk = min(tq, L), min(tk, L)
    H = DIFF_HEADS
    return pl.pallas_call(
        functools.partial(_attn_kernel, tq=tq, tk=tk, out_scale=1.0 - lam_init),
        grid=(H, L // tq),
        in_specs=[pl.BlockSpec(memory_space=pltpu.SMEM),
                  pl.BlockSpec((tq, LANES), lambda h, i: (i, h)),
                  pl.BlockSpec((L, LANES), lambda h, i: (0, H + h)),
                  pl.BlockSpec((L, LANES), lambda h, i: (0, 2 * H + h)),
                  pl.BlockSpec((1, LANES), lambda h, i: (0, 0))],
        out_specs=pl.BlockSpec((tq, LANES), lambda h, i: (i, h)),
        out_shape=jax.ShapeDtypeStruct((L, DIFF_WIDTH), BF16),
        scratch_shapes=[pltpu.VMEM((2 * tq, LANES), BF16),
                        pltpu.VMEM((2 * tq, LANES), F32),
                        pltpu.VMEM((2 * tq, 2 * LANES), F32),
                        pltpu.VMEM((2 * tq, tk), F32)],
        compiler_params=_cp(("parallel", "parallel")),
        name="diff_attn",
    )(lam, qkv, qkv, qkv, subln_g)


def _hg_masks():
    C = HG_TILE
    t = np.arange(C)[:, None]
    s = np.arange(C)[None, :]
    out = []
    for reverse in (False, True):
        for h in HG_LEVELS:
            same = (t // (2 * h)) == (s // (2 * h))
            t_late = (t % (2 * h)) >= h
            s_late = (s % (2 * h)) >= h
            m = same & (~t_late & s_late if reverse else t_late & ~s_late)
            out.append(m)
        for d in range(HG_BAND):
            same = (t // HG_BAND) == (s // HG_BAND)
            out.append(same & ((s == t + d) if reverse else (s == t - d)))
    return jnp.asarray(np.stack(out).astype(np.float32))


def _hg_tri():
    C = HG_TILE
    t = np.arange(C)[:, None]
    r = np.arange(C)[None, :]
    return jnp.asarray(np.stack([(r <= t), (r >= t)]).astype(np.float32), dtype=BF16)


def _hg_tile(qr, fr, v, lb, st, tri, mask_ref, reverse):
    C = HG_TILE
    n_lvl = len(HG_LEVELS)
    m0 = (n_lvl + HG_BAND) if reverse else 0
    q = qr * jax.nn.sigmoid(qr)
    f = lb + (1.0 - lb) * jax.nn.sigmoid(fr)
    k = 1.0 - f
    g = jnp.log(f)
    g1 = g.astype(BF16)
    r1 = g - g1.astype(F32)
    g2 = r1.astype(BF16)
    g3 = (r1 - g2.astype(F32)).astype(BF16)
    b3 = jnp.dot(tri, jnp.concatenate([g1, g2, g3], axis=1), preferred_element_type=F32)
    b = b3[:, 0:LANES] + b3[:, LANES:2 * LANES] + b3[:, 2 * LANES:3 * LANES]
    b_edge = b[0:1, :] if reverse else b[C - 1:C, :]
    vb = v.astype(BF16)

    qc = (q * jnp.exp(b)).astype(BF16)
    kc = (k * jnp.exp(b_edge - b)).astype(BF16)
    o = lax.dot_general(qc, st.astype(BF16), (((1,), (1,)), ((), ())), preferred_element_type=F32)
    st_new = st * jnp.exp(b_edge) + jnp.dot(v.T.astype(BF16), kc, preferred_element_type=F32)

    a = jnp.zeros((C, C), F32)
    for li, h in enumerate(HG_LEVELS):
        rows = []
        for blk in range(C // (2 * h)):
            r = blk * 2 * h + (h if reverse else h - 1)
            rows.append(jnp.broadcast_to(b[r:r + 1, :], (2 * h, LANES)))
        bref = rows[0] if len(rows) == 1 else jnp.concatenate(rows, axis=0)
        e = jnp.exp(-jnp.abs(b - bref))
        ah = lax.dot_general((q * e).astype(BF16), (k * e).astype(BF16), (((1,), (1,)), ((), ())),
                             preferred_element_type=F32)
        a = a + ah * mask_ref[m0 + li]
    def roll_rows(x, shift):
        x3 = pltpu.roll(x.reshape(C // HG_BAND, HG_BAND, LANES), shift % HG_BAND, 1)
        return x3.reshape(C, LANES)
    ed = None
    for d in range(HG_BAND):
        if d == 0:
            p = q * k
        else:
            sh = (C - d) if reverse else d
            shf = (C - (d - 1)) % C if reverse else d - 1
            fsh = f if shf == 0 else roll_rows(f, shf)
            ed = fsh if ed is None else ed * fsh
            p = q * roll_rows(k, sh) * ed
        a = a + jnp.sum(p, axis=1, keepdims=True) * mask_ref[m0 + n_lvl + d]
    o = o + jnp.dot(a.astype(BF16), vb, preferred_element_type=F32)
    return o, st_new


def _hgrn_kernel(qf_ref, qb_ref, ff_ref, fb_ref, vf_ref, vb_ref, lb_ref, tri_ref, mask_ref,
                 of_ref, ob_ref, stf_scr, stb_scr, *, n_tiles):
    @pl.when(pl.program_id(1) == 0)
    def _():
        stf_scr[...] = jnp.zeros(stf_scr.shape, F32)
        stb_scr[...] = jnp.zeros(stb_scr.shape, F32)

    C = HG_TILE
    lbf = lb_ref[0:1, :]
    lbb = lb_ref[1:2, :]

    def body(t, carry):
        rf = pl.multiple_of(t * C, C)
        rb = pl.multiple_of((n_tiles - 1 - t) * C, C)
        o_f, st_f = _hg_tile(qf_ref[pl.ds(rf, C), :], ff_ref[pl.ds(rf, C), :], vf_ref[pl.ds(rf, C), :],
                             lbf, stf_scr[...], tri_ref[0], mask_ref, False)
        of_ref[pl.ds(rf, C), :] = o_f
        stf_scr[...] = st_f
        o_b, st_b = _hg_tile(qb_ref[pl.ds(rb, C), :], fb_ref[pl.ds(rb, C), :], vb_ref[pl.ds(rb, C), :],
                             lbb, stb_scr[...], tri_ref[1], mask_ref, True)
        ob_ref[pl.ds(rb, C), :] = o_b
        stb_scr[...] = st_b
        return carry

    lax.fori_loop(0, n_tiles, body, 0, unroll=True)


def _hgrn2(proj, lb, rows):
    L = proj.shape[0]
    rows = min(rows, L)
    nb = L // rows
    H = HG_HEADS
    nm = len(HG_LEVELS) + HG_BAND
    blk = lambda col0, rev: pl.BlockSpec(
        (rows, LANES), (lambda h, i: (nb - 1 - i, col0 + h)) if rev else (lambda h, i: (i, col0 + h)))
    return pl.pallas_call(
        functools.partial(_hgrn_kernel, n_tiles=rows // HG_TILE),
        grid=(H, nb),
        in_specs=[blk(0, False), blk(0, True), blk(H, False), blk(2 * H, True),
                  blk(3 * H, False), blk(3 * H, True),
                  pl.BlockSpec((2, LANES), lambda h, i: (0, h)),
                  pl.BlockSpec((2, HG_TILE, HG_TILE), lambda h, i: (0, 0, 0)),
                  pl.BlockSpec((2 * nm, HG_TILE, HG_TILE), lambda h, i: (0, 0, 0))],
        out_specs=[blk(0, False), blk(0, True)],
        out_shape=[jax.ShapeDtypeStruct((L, HG_WIDTH), F32)] * 2,
        scratch_shapes=[pltpu.VMEM((LANES, LANES), F32)] * 2,
        compiler_params=_cp(("parallel", "arbitrary")),
        name="hgrn2",
    )(proj, proj, proj, proj, proj, proj, lb, _hg_tri(), _hg_masks())


def _outproj_kernel(x_ref, oa_ref, of_ref, ob_ref, gr_ref, ng_ref, gate_ref, lng_ref, lnb_ref, w_ref,
                    o_ref, lhs_scr):
    lhs_scr[:, 0:DIFF_WIDTH] = oa_ref[...]
    ng = ng_ref[...]
    for h in range(HG_HEADS):
        sl = slice(h * LANES, (h + 1) * LANES)
        o = of_ref[:, sl] + ob_ref[:, sl]
        ms = jnp.mean(o * o, axis=-1, keepdims=True)
        gr = gr_ref[:, sl]
        hg = o * lax.rsqrt(ms + RMS_EPS) * ng * (gr * jax.nn.sigmoid(gr))
        lhs_scr[:, DIFF_WIDTH + h * LANES:DIFF_WIDTH + (h + 1) * LANES] = hg.astype(BF16)
    y = jnp.dot(lhs_scr[...], w_ref[...], preferred_element_type=F32)
    z = ALPHA * x_ref[...] + (1.0 + gate_ref[...]) * y
    o_ref[...] = _layer_norm_rows(z, lng_ref[...], lnb_ref[...])


def _outproj_ln(x, oa, of, ob, proj_hg, norm_g, gate, ln_g, ln_b, w, tm):
    L = x.shape[0]
    tm = min(tm, L)
    row = lambda i: (i, 0)
    vec = pl.BlockSpec((1, D_MODEL), lambda i: (0, 0))
    return pl.pallas_call(
        _outproj_kernel,
        grid=(L // tm,),
        in_specs=[pl.BlockSpec((tm, D_MODEL), row),
                  pl.BlockSpec((tm, DIFF_WIDTH), row),
                  pl.BlockSpec((tm, HG_WIDTH), row),
                  pl.BlockSpec((tm, HG_WIDTH), row),
                  pl.BlockSpec((tm, HG_WIDTH), lambda i: (i, 4)),
                  pl.BlockSpec((1, LANES), lambda i: (0, 0)),
                  vec, vec, vec,
                  pl.BlockSpec((D_MODEL, D_MODEL), lambda i: (0, 0))],
        out_specs=pl.BlockSpec((tm, D_MODEL), row),
        out_shape=jax.ShapeDtypeStruct((L, D_MODEL), F32),
        scratch_shapes=[pltpu.VMEM((tm, D_MODEL), BF16)],
        compiler_params=_cp(("parallel",)),
        name="outproj_ln",
    )(x, oa, of, ob, proj_hg, norm_g, gate, ln_g, ln_b, w)


def _mlp_kernel(x_ref, sc_ref, sh_ref, gate_ref, lng_ref, lnb_ref, w1_ref, w2_ref, o_ref, h_scr, acc_scr):
    j = pl.program_id(1)

    @pl.when(j == 0)
    def _():
        h_scr[...] = (x_ref[...] * (1.0 + sc_ref[...]) + sh_ref[...]).astype(BF16)
        acc_scr[...] = jnp.zeros(acc_scr.shape, F32)

    a = jnp.maximum(jnp.dot(h_scr[...], w1_ref[...], preferred_element_type=F32), 0.0)
    acc_scr[...] += jnp.dot((a * a).astype(BF16), w2_ref[...], preferred_element_type=F32)

    @pl.when(j == pl.num_programs(1) - 1)
    def _():
        z = ALPHA * x_ref[...] + (1.0 + gate_ref[...]) * acc_scr[...]
        o_ref[...] = _layer_norm_rows(z, lng_ref[...], lnb_ref[...])


def _mlp_ln(x, scale, shift, gate, ln_g, ln_b, w1, w2, tm, tf):
    L = x.shape[0]
    tm = min(tm, L)
    vec = pl.BlockSpec((1, D_MODEL), lambda i, j: (0, 0))
    return pl.pallas_call(
        _mlp_kernel,
        grid=(L // tm, D_FF // tf),
        in_specs=[pl.BlockSpec((tm, D_MODEL), lambda i, j: (i, 0)),
                  vec, vec, vec, vec, vec,
                  pl.BlockSpec((D_MODEL, tf), lambda i, j: (0, j)),
                  pl.BlockSpec((tf, D_MODEL), lambda i, j: (j, 0))],
        out_specs=pl.BlockSpec((tm, D_MODEL), lambda i, j: (i, 0)),
        out_shape=jax.ShapeDtypeStruct((L, D_MODEL), F32),
        scratch_shapes=[pltpu.VMEM((tm, D_MODEL), BF16), pltpu.VMEM((tm, D_MODEL), F32)],
        compiler_params=_cp(("parallel", "arbitrary")),
        name="mlp_ln",
    )(x, scale, shift, gate, ln_g, ln_b, w1, w2)


def _s5_operators(lam_re, lam_im, log_dt, b_re, b_im, c_re, c_im):
    T, G, P, Hc, NB, GPB = S5_T, S5_GROUPS, S5_STATE, S5_GROUP, S5_BLOCKS, S5_GPB
    dt = jnp.exp(log_dt)[..., None]
    zr, zi = lam_re * dt, lam_im * dt
    mag = jnp.exp(zr)
    ab_re, ab_im = mag * jnp.cos(zi), mag * jnp.sin(zi)
    den = lam_re * lam_re + lam_im * lam_im
    coef_re = ((ab_re - 1.0) * lam_re + ab_im * lam_im) / den
    coef_im = (ab_im * lam_re - (ab_re - 1.0) * lam_im) / den
    bb_re = coef_re[..., None] * b_re - coef_im[..., None] * b_im
    bb_im = coef_re[..., None] * b_im + coef_im[..., None] * b_re

    def apow(k):
        k = jnp.asarray(k, F32)[:, None, None, None]
        m = jnp.exp(zr[None] * k)
        return m * jnp.cos(zi[None] * k), m * jnp.sin(zi[None] * k)

    pr, pi = apow(np.arange(T))
    ca_re = c_re[None] * pr[:, :, :, None, :] - c_im[None] * pi[:, :, :, None, :]
    ca_im = c_re[None] * pi[:, :, :, None, :] + c_im[None] * pr[:, :, :, None, :]
    bt_re, bt_im = jnp.swapaxes(bb_re, 2, 3), jnp.swapaxes(bb_im, 2, 3)
    lagk = jnp.sum(ca_re[:, :, :, :, None, :] * bt_re[None, :, :, None, :, :]
                   - ca_im[:, :, :, :, None, :] * bt_im[None, :, :, None, :, :], axis=-1)
    kc = jnp.transpose(lagk.reshape(T, 2, NB, GPB, Hc, Hc), (2, 1, 0, 5, 3, 4)).reshape(NB, 2 * T, Hc, LANES)

    def gp_lanes(v):
        return jnp.transpose(v.reshape(2, NB, GPB, Hc, P), (1, 0, 3, 2, 4)).reshape(NB, 2, Hc, GPB * P)
    bt = jnp.stack([gp_lanes(bt_re), gp_lanes(bt_im)], axis=1)
    def e_tab(v):
        v = jnp.stack([v[::-1, 0], v[:, 1]], axis=1).reshape(T, 2, NB, GPB * P)
        return jnp.transpose(v, (2, 0, 1, 3))
    et = jnp.stack([e_tab(pr), e_tab(pi)], axis=2)[:, :, :, :, None, :]

    ct = jnp.stack([gp_lanes(c_re), gp_lanes(c_im)], axis=1)
    cr, ci = apow(np.arange(1, T + 1))
    def ec_tab(v):
        v = jnp.stack([v[:, 0], v[::-1, 1]], axis=1).reshape(T, 2, NB, GPB * P)
        return jnp.transpose(v, (2, 0, 1, 3))
    ec = jnp.stack([ec_tab(cr), ec_tab(ci)], axis=2)[:, :, :, :, None, :]
    return tuple(_s5_expand(kc, bt, et, ct, ec)) + ((zr, zi),)


def _s5_expand_kernel(kc_ref, bt_ref, et_ref, ct_ref, ec_ref, wt_ref, ws_ref, wc_ref):
    n_q = S5_GPB * S5_STATE // LANES
    gpq = S5_GPB // n_q
    row = lax.broadcasted_iota(jnp.int32, (LANES, LANES), 0)
    col = lax.broadcasted_iota(jnp.int32, (LANES, LANES), 1)
    over_groups = lambda a: jnp.concatenate([a] * S5_GPB, axis=0)
    same_group = (row // S5_GROUP) == (col // S5_GROUP)

    for u in range(S5_EXPAND_SLABS):
        s = pl.program_id(1) * S5_EXPAND_SLABS + u
        rows = slice(u * LANES, (u + 1) * LANES)
        for t in range(S5_T):
            lag = t - s
            k = kc_ref[0, jnp.where(lag >= 0, lag, S5_T - lag)] + jnp.where(lag == 0, kc_ref[0, S5_T], 0.0)
            wt_ref[0, rows, t * LANES:(t + 1) * LANES] = jnp.where(same_group, over_groups(k), 0.0).astype(BF16)

        for d in range(2):
            for q in range(n_q):
                sl = slice(q * LANES, (q + 1) * LANES)
                own = (row // S5_GROUP) == (gpq * q + col // S5_STATE)
                for m_ref, p_ref, o_ref, im_sign in ((bt_ref, et_ref, ws_ref, 1.0), (ct_ref, ec_ref, wc_ref, -1.0)):
                    m_re, m_im = m_ref[0, 0, d, :, sl], m_ref[0, 1, d, :, sl]
                    p_re, p_im = p_ref[0, u, 0, d, :, sl], p_ref[0, u, 1, d, :, sl]
                    vals = (p_re * m_re - p_im * m_im, im_sign * (p_re * m_im + p_im * m_re))
                    for ri in range(2):
                        c0 = ((d * n_q + q) * 2 + ri) * LANES
                        o_ref[0, rows, c0:c0 + LANES] = jnp.where(own, over_groups(vals[ri]), 0.0).astype(BF16)


def _s5_expand(kc, bt, et, ct, ec):
    NB, T, Hc, P, ns = S5_BLOCKS, S5_T, S5_GROUP, S5_STATE, S5_EXPAND_SLABS
    slab = pl.BlockSpec((1, ns * LANES, S5_ROW), lambda b, s: (b, s, 0))
    mat = pl.BlockSpec((1, 2, 2, Hc, S5_GPB * P), lambda b, s: (b, 0, 0, 0, 0))
    powers = pl.BlockSpec((1, ns, 2, 2, 1, S5_GPB * P), lambda b, s: (b, s, 0, 0, 0, 0))
    return pl.pallas_call(
        _s5_expand_kernel,
        grid=(NB, T // ns),
        in_specs=[pl.BlockSpec((1, 2 * T, Hc, LANES), lambda b, s: (b, 0, 0, 0)), mat, powers, mat, powers],
        out_specs=[slab, slab, slab],
        out_shape=[jax.ShapeDtypeStruct((NB, S5_ROW, S5_ROW), BF16)] * 3,
        compiler_params=_cp(("parallel", "arbitrary")),
        name="s5_expand",
    )(kc, bt, et, ct, ec)


def _s5_state_lanes(re, im):
    n = re.shape[0]
    x = jnp.stack([re, im], axis=0)
    x = x.reshape(2, n, 2, S5_BLOCKS, S5_GPB * S5_STATE // LANES, LANES)
    x = jnp.transpose(x, (2, 1, 3, 4, 0, 5))
    return x.reshape(2, n, S5_BLOCKS * S5_SW)


def _s5_scan_powers(zr, zi, n_steps):
    def power(k):
        k = jnp.asarray(k, F32)[:, :, None, None]
        m = jnp.exp(zr[None] * k)
        return m * jnp.cos(zi[None] * k), m * jnp.sin(zi[None] * k)
    k_dbl = S5_T * (2.0 ** np.arange(n_steps))
    j = np.arange(SUBLANES)
    return (_s5_state_lanes(*power(np.stack([k_dbl, k_dbl], axis=1))),
            _s5_state_lanes(*power(S5_T * np.stack([j, SUBLANES - 1 - j], axis=1))))


def _s5_fold(x_ref, sc, sh, tc):
    parts = [x_ref[pl.ds(t, tc, stride=S5_T), :] * (1.0 + sc) + sh for t in range(S5_T)]
    return parts


def _s5_state_kernel(x_ref, sc_ref, sh_ref, w_ref, zf_ref, zb_ref, *, tc):
    parts = _s5_fold(x_ref, sc_ref[...], sh_ref[...], tc)
    u = jnp.concatenate([p.astype(BF16) for p in parts], axis=1)
    z = jnp.dot(u, w_ref[0], preferred_element_type=F32)
    zf_ref[...] = z[:, :S5_SW]
    zb_ref[...] = z[:, S5_SW:]


def _s5_scan_kernel(zf_ref, zb_ref, af_ref, ab_ref, pf_ref, pb_ref, sf_ref, sb_ref, loc_scr,
                    *, n_rows, n_steps):
    n_blk = n_rows // SUBLANES
    n_local = int(math.log2(SUBLANES))
    sub = lax.broadcasted_iota(jnp.int32, (n_blk, SUBLANES, LANES), 1)
    blk = lax.broadcasted_iota(jnp.int32, (n_blk, LANES), 0)

    def cmul_add(re, im, ar, ai, rs, is_):
        return re + ar * rs - ai * is_, im + ar * is_ + ai * rs

    def scan(z_ref, a_ref, p_ref, reverse):
        re = z_ref[:, 0:LANES].reshape(n_blk, SUBLANES, LANES)
        im = z_ref[:, LANES:2 * LANES].reshape(n_blk, SUBLANES, LANES)

        def in_block(v, sh):
            if reverse:
                return jnp.where(sub < SUBLANES - sh, pltpu.roll(v, SUBLANES - sh, 1), 0.0)
            return jnp.where(sub >= sh, pltpu.roll(v, sh, 1), 0.0)

        def across(v, sh):
            if reverse:
                return jnp.where(blk < n_blk - sh, pltpu.roll(v, n_blk - sh, 0), 0.0)
            return jnp.where(blk >= sh, pltpu.roll(v, sh, 0), 0.0)

        for k in range(n_local):
            ar, ai = a_ref[0, k:k + 1, 0:LANES], a_ref[0, k:k + 1, LANES:2 * LANES]
            re, im = cmul_add(re, im, ar, ai, in_block(re, 2 ** k), in_block(im, 2 ** k))
        edge = 0 if reverse else SUBLANES - 1
        loc_scr[0] = re.reshape(n_rows, LANES)
        loc_scr[1] = im.reshape(n_rows, LANES)
        cr = loc_scr[0, pl.ds(edge, n_blk, stride=SUBLANES), :]
        ci = loc_scr[1, pl.ds(edge, n_blk, stride=SUBLANES), :]
        for k in range(n_local, n_steps):
            ar, ai = a_ref[0, k:k + 1, 0:LANES], a_ref[0, k:k + 1, LANES:2 * LANES]
            sh = 2 ** (k - n_local)
            cr, ci = cmul_add(cr, ci, ar, ai, across(cr, sh), across(ci, sh))
        cr, ci = across(cr, 1), across(ci, 1)
        pr = p_ref[0, :, 0:LANES][None]
        pi = p_ref[0, :, LANES:2 * LANES][None]
        cr3 = jnp.broadcast_to(cr[:, None, :], (n_blk, SUBLANES, LANES))
        ci3 = jnp.broadcast_to(ci[:, None, :], (n_blk, SUBLANES, LANES))
        ore, oim = cmul_add(in_block(re, 1), in_block(im, 1), pr, pi, cr3, ci3)
        return ore.reshape(n_rows, LANES), oim.reshape(n_rows, LANES)

    re, im = scan(zf_ref, af_ref, pf_ref, False)
    sf_ref[:, 0:LANES] = re.astype(BF16)
    sf_ref[:, LANES:2 * LANES] = im.astype(BF16)
    re, im = scan(zb_ref, ab_ref, pb_ref, True)
    sb_ref[:, 0:LANES] = re.astype(BF16)
    sb_ref[:, LANES:2 * LANES] = im.astype(BF16)


def _s5_out_kernel(x_ref, sc_ref, sh_ref, d_ref, sf_ref, sb_ref, wt_ref, wc_ref, o_ref, *, tc, n_half):
    j = pl.program_id(2)
    sc, sh, d = sc_ref[...], sh_ref[...], d_ref[...]
    parts = _s5_fold(x_ref, sc, sh, tc)
    u = jnp.concatenate([p.astype(BF16) for p in parts], axis=1)
    s = jnp.concatenate([sf_ref[...], sb_ref[...]], axis=1)
    y = (jnp.dot(u, wt_ref[0], preferred_element_type=F32)
         + lax.dot_general(s, wc_ref[0], (((1,), (1,)), ((), ())),
                           preferred_element_type=F32))
    tph = S5_T // n_half
    for jj in range(n_half):
        @pl.when(j == jj)
        def _():
            for tl in range(tph):
                t = jj * tph + tl
                o_ref[pl.ds(t, tc, stride=S5_T), :] = y[:, tl * LANES:(tl + 1) * LANES] + d * parts[t]


def _s5_mixer_core(x, scale, shift, d_skip, ops, tc):
    w_toep, w_state, w_carry, (zr, zi) = ops
    L = x.shape[0]
    n_rows = L // S5_T
    NB = S5_BLOCKS
    n_steps = int(math.log2(n_rows))
    assert 2 ** n_steps == n_rows
    tc = min(tc, n_rows)
    vecb = pl.BlockSpec((1, LANES), lambda b, i: (0, b))

    zf, zb = pl.pallas_call(
        functools.partial(_s5_state_kernel, tc=tc),
        grid=(NB, n_rows // tc),
        in_specs=[pl.BlockSpec((tc * S5_T, LANES), lambda b, i: (i, b)), vecb, vecb,
                  pl.BlockSpec((1, S5_ROW, 2 * S5_SW), lambda b, i: (b, 0, 0))],
        out_specs=[pl.BlockSpec((tc, S5_SW), lambda b, i: (i, b))] * 2,
        out_shape=[jax.ShapeDtypeStruct((n_rows, NB * S5_SW), F32)] * 2,
        compiler_params=_cp(("parallel", "parallel")),
        name="s5_state",
    )(x, scale, shift, w_state)

    pw, pj = _s5_scan_powers(zr, zi, n_steps)
    cw = 2 * LANES
    col = lambda n: (0, n)
    sf, sb = pl.pallas_call(
        functools.partial(_s5_scan_kernel, n_rows=n_rows, n_steps=n_steps),
        grid=(NB * S5_SW // cw,),
        in_specs=[pl.BlockSpec((n_rows, cw), col),
                  pl.BlockSpec((n_rows, cw), col),
                  pl.BlockSpec((1, n_steps, cw), lambda n: (0, 0, n)),
                  pl.BlockSpec((1, n_steps, cw), lambda n: (1, 0, n)),
                  pl.BlockSpec((1, SUBLANES, cw), lambda n: (0, 0, n)),
                  pl.BlockSpec((1, SUBLANES, cw), lambda n: (1, 0, n))],
        out_specs=[pl.BlockSpec((n_rows, cw), col)] * 2,
        out_shape=[jax.ShapeDtypeStruct((n_rows, NB * S5_SW), BF16)] * 2,
        scratch_shapes=[pltpu.VMEM((2, n_rows, LANES), F32)],
        compiler_params=_cp(("parallel",)),
        name="s5_scan",
    )(zf, zb, pw, pw, pj, pj)

    n_half = 2
    nh = S5_ROW // n_half
    vec3 = pl.BlockSpec((1, LANES), lambda b, i, j: (0, b))
    return pl.pallas_call(
        functools.partial(_s5_out_kernel, tc=tc, n_half=n_half),
        grid=(NB, n_rows // tc, n_half),
        in_specs=[pl.BlockSpec((tc * S5_T, LANES), lambda b, i, j: (i, b)), vec3, vec3, vec3,
                  pl.BlockSpec((tc, S5_SW), lambda b, i, j: (i, b)),
                  pl.BlockSpec((tc, S5_SW), lambda b, i, j: (i, b)),
                  pl.BlockSpec((1, S5_ROW, nh), lambda b, i, j: (b, 0, j)),
                  pl.BlockSpec((1, nh, 2 * S5_SW), lambda b, i, j: (b, j, 0))],
        out_specs=pl.BlockSpec((tc * S5_T, LANES), lambda b, i, j: (i, b)),
        out_shape=jax.ShapeDtypeStruct((L, D_MODEL), F32),
        compiler_params=_cp(("parallel", "parallel", "arbitrary")),
        name="s5_out",
    )(x, scale, shift, d_skip, sf, sb, w_toep, w_carry)


def _glu_kernel(x_ref, y_ref, gate_ref, lng_ref, lnb_ref, wa_ref, wg_ref, o_ref, lhs_scr, mix_scr, *, tn):
    j = pl.program_id(1)

    @pl.when(j == 0)
    def _():
        lhs_scr[...] = jax.nn.gelu(y_ref[...]).astype(BF16)

    a = jnp.dot(lhs_scr[...], wa_ref[...], preferred_element_type=F32)
    g = jnp.dot(lhs_scr[...], wg_ref[...], preferred_element_type=F32)
    mix = a * jax.nn.sigmoid(g)
    for jj in range(D_MODEL // tn):
        @pl.when(j == jj)
        def _():
            mix_scr[:, jj * tn:(jj + 1) * tn] = mix

    @pl.when(j == pl.num_programs(1) - 1)
    def _():
        z = ALPHA * x_ref[...] + (1.0 + gate_ref[...]) * mix_scr[...]
        o_ref[...] = _layer_norm_rows(z, lng_ref[...], lnb_ref[...])


def _glu_ln(x, y, gate, ln_g, ln_b, w_glu, tm, tn):
    L = x.shape[0]
    tm = min(tm, L)
    nj = D_MODEL // tn
    vec = pl.BlockSpec((1, D_MODEL), lambda i, j: (0, 0))
    return pl.pallas_call(
        functools.partial(_glu_kernel, tn=tn),
        grid=(L // tm, nj),
        in_specs=[pl.BlockSpec((tm, D_MODEL), lambda i, j: (i, 0)),
                  pl.BlockSpec((tm, D_MODEL), lambda i, j: (i, 0)),
                  vec, vec, vec,
                  pl.BlockSpec((D_MODEL, tn), lambda i, j: (0, j)),
                  pl.BlockSpec((D_MODEL, tn), lambda i, j: (0, nj + j))],
        out_specs=pl.BlockSpec((tm, D_MODEL), lambda i, j: (i, 0)),
        out_shape=jax.ShapeDtypeStruct((L, D_MODEL), F32),
        scratch_shapes=[pltpu.VMEM((tm, D_MODEL), BF16), pltpu.VMEM((tm, D_MODEL), F32)],
        compiler_params=_cp(("parallel", "arbitrary")),
        name="glu_ln",
    )(x, y, gate, ln_g, ln_b, w_glu, w_glu)


def kernel(x, c, ada_w, ada_b, ln_g, ln_b, mix_w_in, mix_w_out, diff_lambda, diff_subln_g, hg_lower_bound,
           hg_norm_g, s5_lambda_re, s5_lambda_im, s5_log_dt, s5_b_re, s5_b_im, s5_c_re, s5_c_im, s5_d,
           s5_w_glu, mlp_w1, mlp_w2):
    B, L, D = x.shape
    assert B == 1 and D == D_MODEL
    xs = x.reshape(L, D)
    mod = _adaln(c, ada_w, ada_b)
    mods = lambda a: (mod[a, :, 0:D], mod[a, :, D:2 * D], mod[a, :, 2 * D:3 * D])
    lb_table = jnp.cumsum(jax.nn.softmax(hg_lower_bound.astype(F32), axis=1), axis=1)
    rot_tabs = _rotary_tables(L)

    for layer in range(DEPTH):
        shift, scale, gate = mods(2 * layer)
        lng, lnb = ln_g[layer, 0][None], ln_b[layer, 0][None]
        if layer % 2 == 0:
            e = layer // 2
            lam_init = 0.8 - 0.6 * math.exp(-0.3 * layer)
            w_in = _to_bf16(mix_w_in, e)
            qkv, proj_hg = _inproj(xs, scale, shift, w_in, 3 * DIFF_WIDTH, rot_tabs, 2, INPROJ_TM, INPROJ_TN)
            lf = diff_lambda[e].astype(F32)
            lam = jnp.exp(jnp.sum(lf[0] * lf[1])) - jnp.exp(jnp.sum(lf[2] * lf[3])) + lam_init
            oa = _diff_attention(qkv, lam.reshape(1), diff_subln_g[e][None], lam_init, ATTN_TQ, ATTN_TK)
            of, ob = _hgrn2(proj_hg, lb_table[:, layer], HG_ROWS)
            xs = _outproj_ln(xs, oa, of, ob, proj_hg, hg_norm_g[e][None], gate, lng, lnb,
                             _to_bf16(mix_w_out, e), OUTPROJ_TM)
        else:
            o = layer // 2
            ops = _s5_operators(s5_lambda_re[o], s5_lambda_im[o], s5_log_dt[o], s5_b_re[o], s5_b_im[o],
                                s5_c_re[o], s5_c_im[o])
            y = _s5_mixer_core(xs, scale, shift, s5_d[o][None], ops, S5_TC)
            xs = _glu_ln(xs, y, gate, lng, lnb, _to_bf16(s5_w_glu, o), GLU_TM, GLU_TN)
        shift, scale, gate = mods(2 * layer + 1)
        xs = _mlp_ln(xs, scale, shift, gate, ln_g[layer, 1][None], ln_b[layer, 1][None],
                     _to_bf16(mlp_w1, layer), _to_bf16(mlp_w2, layer), MLP_TM, MLP_TF)
    return xs.reshape(B, L, D)
```

```python
import functools
import math

import numpy as np
import jax
import jax.numpy as jnp
from jax import lax
from jax.experimental import pallas as pl
from jax.experimental.pallas import tpu as pltpu

F32 = jnp.float32
BF16 = jnp.bfloat16

D_MODEL = 2048
DEPTH = 2
LANES = 128
SUBLANES = 8

DIFF_HEADS = 8
DIFF_QK_DIM = 64
DIFF_WIDTH = 1024
ROT_DIM = DIFF_QK_DIM // 4
ROPE_THETA = 500000.0
ATTN_ROWS = 128

HG_HEADS = 8
HG_WIDTH = 1024
HG_TILE = 128
HG_BAND = SUBLANES
HG_LEVELS = (64, 32, 16, 8)

S5_GROUP = 16
S5_GROUPS = D_MODEL // S5_GROUP
S5_STATE = 64
S5_T = 16
S5_GPB = LANES // S5_GROUP
S5_BLOCKS = D_MODEL // LANES
S5_ROW = S5_T * LANES
S5_SW = S5_GPB * S5_STATE * 2

D_FF = 4 * D_MODEL
ALPHA = (2 * DEPTH) ** 0.25
LN_EPS = 1e-5
RMS_EPS = 1e-6

VMEM_LIMIT = 56 * 1024 * 1024
CAST_BLOCK_ELEMS = 2 * 1024 * 1024

ADALN_TN = 768
INPROJ_TM, INPROJ_TN = 1024, 1024
ATTN_TQ, ATTN_TK = 2048, 1024
HG_ROWS = 1024
OUTPROJ_TM = 256
MLP_TM, MLP_TF = 512, 1024
S5_TC = 512
S5_EXPAND_SLABS = 4
GLU_TM, GLU_TN = 512, 1024


def _cp(sem):
    return pltpu.CompilerParams(dimension_semantics=sem, vmem_limit_bytes=VMEM_LIMIT)


def _bf16_kernel(w_ref, o_ref):
    o_ref[...] = w_ref[0].astype(BF16)


def _to_bf16(w, layer):
    _, r, c = w.shape
    tr = max(SUBLANES, CAST_BLOCK_ELEMS // c)
    assert r % tr == 0
    return pl.pallas_call(
        _bf16_kernel, grid=(r // tr,),
        in_specs=[pl.BlockSpec((1, tr, c), lambda i: (layer, i, 0))],
        out_specs=pl.BlockSpec((tr, c), lambda i: (i, 0)),
        out_shape=jax.ShapeDtypeStruct((r, c), BF16), compiler_params=_cp(("parallel",)), name="to_bf16",
    )(w)


def _layer_norm_rows(z, g, b):
    mu = jnp.mean(z, axis=-1, keepdims=True)
    zc = z - mu
    var = jnp.mean(zc * zc, axis=-1, keepdims=True)
    return zc * lax.rsqrt(var + LN_EPS) * g + b


def _adaln_kernel(c_ref, w_ref, b_ref, o_ref):
    c = c_ref[...]
    ca = c * jax.nn.sigmoid(c)
    w = w_ref[0]
    reps = w.shape[1] // LANES
    cb = jnp.concatenate([ca] * reps, axis=1)
    o_ref[0] = jnp.sum(w * cb, axis=0, keepdims=True) + b_ref[0]


def _adaln(c, ada_w, ada_b):
    n = 3 * D_MODEL
    tn = ADALN_TN
    w = ada_w.reshape(2 * DEPTH, D_MODEL, n)
    b = ada_b.reshape(2 * DEPTH, 1, n)
    c_rep = jnp.broadcast_to(c.reshape(D_MODEL, 1), (D_MODEL, LANES))
    return pl.pallas_call(
        _adaln_kernel,
        grid=(2 * DEPTH, n // tn),
        in_specs=[pl.BlockSpec((D_MODEL, LANES), lambda a, j: (0, 0)),
                  pl.BlockSpec((1, D_MODEL, tn), lambda a, j: (a, 0, j)),
                  pl.BlockSpec((1, 1, tn), lambda a, j: (a, 0, j))],
        out_specs=pl.BlockSpec((1, 1, tn), lambda a, j: (a, 0, j)),
        out_shape=jax.ShapeDtypeStruct((2 * DEPTH, 1, n), F32),
        compiler_params=_cp(("parallel", "parallel")),
        name="adaln",
    )(c_rep, w, b)


def _inproj_kernel(x_ref, sc_ref, sh_ref, w_ref, cos_ref, sin_ref, oa_ref, og_ref, h_scr, *, n_rot, n_a):
    j = pl.program_id(1)

    @pl.when(j == 0)
    def _():
        h_scr[...] = (x_ref[...] * (1.0 + sc_ref[...]) + sh_ref[...]).astype(BF16)

    y = jnp.dot(h_scr[...], w_ref[...], preferred_element_type=F32)

    @pl.when(j < n_rot)
    def _():
        half = ROT_DIM // 2
        cs, sn = cos_ref[...], sin_ref[...]
        pos = lax.broadcasted_iota(jnp.int32, sn.shape, 1) % DIFF_QK_DIM
        sa = jnp.where(pos >= half, sn, 0.0)
        sb = jnp.where(pos < half, -sn, 0.0)
        for g in range(y.shape[1] // LANES):
            yg = y[:, g * LANES:(g + 1) * LANES]
            rot = (yg * cs + pltpu.roll(yg, ROT_DIM // 2, 1) * sa
                   + pltpu.roll(yg, LANES - ROT_DIM // 2, 1) * sb)
            oa_ref[:, g * LANES:(g + 1) * LANES] = rot.astype(oa_ref.dtype)

    @pl.when((j >= n_rot) & (j < n_a))
    def _():
        oa_ref[...] = y.astype(oa_ref.dtype)

    @pl.when(j >= n_a)
    def _():
        og_ref[...] = y


def _inproj(x, scale, shift, w, n_a_cols, rot_tabs, n_rot, tm, tn):
    L = x.shape[0]
    n = w.shape[1]
    tm = min(tm, L)
    n_a = n_a_cols // tn
    return pl.pallas_call(
        functools.partial(_inproj_kernel, n_rot=n_rot, n_a=n_a),
        grid=(L // tm, n // tn),
        in_specs=[pl.BlockSpec((tm, D_MODEL), lambda i, j: (i, 0)),
                  pl.BlockSpec((1, D_MODEL), lambda i, j: (0, 0)),
                  pl.BlockSpec((1, D_MODEL), lambda i, j: (0, 0)),
                  pl.BlockSpec((D_MODEL, tn), lambda i, j: (0, j)),
                  pl.BlockSpec((tm, LANES), lambda i, j: (i, 0)),
                  pl.BlockSpec((tm, LANES), lambda i, j: (i, 0))],
        out_specs=[pl.BlockSpec((tm, tn), lambda i, j: (i, jnp.minimum(j, n_a - 1))),
                   pl.BlockSpec((tm, tn), lambda i, j: (i, jnp.maximum(j - n_a, 0)))],
        out_shape=[jax.ShapeDtypeStruct((L, n_a_cols), BF16),
                   jax.ShapeDtypeStruct((L, n - n_a_cols), F32)],
        scratch_shapes=[pltpu.VMEM((tm, D_MODEL), BF16)],
        compiler_params=_cp(("parallel", "arbitrary")),
        name="inproj",
    )(x, scale, shift, w, *rot_tabs)


def _rotary_tables(L):
    half = ROT_DIM // 2
    pos = jnp.arange(L, dtype=F32)
    inv_freq = 1.0 / (ROPE_THETA ** (jnp.arange(0, ROT_DIM, 2, dtype=F32) / ROT_DIM))
    lane = np.arange(LANES) % DIFF_QK_DIM
    rotated = jnp.asarray(lane < ROT_DIM)[None, :]
    ang = pos[:, None] * inv_freq[lane % half][None, :]
    return jnp.where(rotated, jnp.cos(ang), 1.0), jnp.where(rotated, jnp.sin(ang), 0.0)


def _attn_kernel(lam_ref, q_ref, k_ref, v_ref, g_ref, o_ref, q2_scr, m_scr, acc_scr, s_scr,
                 *, tq, tk, out_scale):
    n_rows = 2 * tq
    n_chunks = n_rows // ATTN_ROWS
    n_kv = k_ref.shape[0] // tk
    q = q_ref[...]
    lane = lax.broadcasted_iota(jnp.int32, q.shape, 1)
    qs = q * (DIFF_QK_DIM ** -0.5)
    q2_scr[0:tq, :] = jnp.where(lane < DIFF_QK_DIM, qs, 0).astype(BF16)
    q2_scr[tq:n_rows, :] = jnp.where(lane >= DIFF_QK_DIM, qs, 0).astype(BF16)
    m_scr[...] = jnp.full(m_scr.shape, -jnp.inf, F32)
    acc_scr[...] = jnp.zeros(acc_scr.shape, F32)
    ones = jnp.ones((tk, LANES), BF16)
    n_col = tk // LANES

    def scores(r, kc):
        rs = slice(r * ATTN_ROWS, (r + 1) * ATTN_ROWS)
        return lax.dot_general(q2_scr[rs, :], kc, (((1,), (1,)), ((), ())), preferred_element_type=F32)

    kc0 = k_ref[0:tk, :]
    for r in range(n_chunks):
        s_scr[r * ATTN_ROWS:(r + 1) * ATTN_ROWS, :] = scores(r, kc0)

    def step(j, prefetch):
        c0 = pl.multiple_of(j * tk, tk)
        if prefetch:
            k_next = k_ref[pl.ds(pl.multiple_of((j + 1) * tk, tk), tk), :]
        v1 = jnp.concatenate([v_ref[pl.ds(c0, tk), :], ones], axis=1)
        for r in range(n_chunks):
            rs = slice(r * ATTN_ROWS, (r + 1) * ATTN_ROWS)
            s = s_scr[rs, :]
            if prefetch:
                s_scr[rs, :] = scores(r, k_next)
            mx = s[:, 0:LANES]
            for c in range(1, n_col):
                mx = jnp.maximum(mx, s[:, c * LANES:(c + 1) * LANES])
            m_prev = m_scr[rs, :]
            m_new = jnp.maximum(m_prev, jnp.max(mx, axis=1, keepdims=True))
            alpha = jnp.exp(m_prev - m_new)
            p = jnp.exp(s - jnp.concatenate([m_new] * n_col, axis=1)).astype(BF16)
            pv = jnp.dot(p, v1, preferred_element_type=F32)
            acc_scr[rs, :] = jnp.concatenate([alpha, alpha], axis=1) * acc_scr[rs, :] + pv
            m_scr[rs, :] = m_new

    def body(j, carry):
        step(j, True)
        return carry

    lax.fori_loop(0, n_kv - 1, body, 0)
    step(n_kv - 1, False)

    acc = acc_scr[...]
    o2 = acc[:, 0:LANES] / acc[:, LANES:2 * LANES]
    o = o2[0:tq, :] - lam_ref[0] * o2[tq:n_rows, :]
    ms = jnp.mean(o * o, axis=-1, keepdims=True)
    o_ref[...] = (o * lax.rsqrt(ms + RMS_EPS) * g_ref[...] * out_scale).astype(o_ref.dtype)


def _diff_attention(qkv, lam, subln_g, lam_init, tq, tk):
    L = qkv.shape[0]
    tq, tk = min(tq, L), min(tk, L)
    H = DIFF_HEADS
    return pl.pallas_call(
        functools.partial(_attn_kernel, tq=tq, tk=tk, out_scale=1.0 - lam_init),
        grid=(H, L // tq),
        in_specs=[pl.BlockSpec(memory_space=pltpu.SMEM),
                  pl.BlockSpec((tq, LANES), lambda h, i: (i, h)),
                  pl.BlockSpec((L, LANES), lambda h, i: (0, H + h)),
                  pl.BlockSpec((L, LANES), lambda h, i: (0, 2 * H + h)),
                  pl.BlockSpec((1, LANES), lambda h, i: (0, 0))],
        out_specs=pl.BlockSpec((tq, LANES), lambda h, i: (i, h)),
        out_shape=jax.ShapeDtypeStruct((L, DIFF_WIDTH), BF16),
        scratch_shapes=[pltpu.VMEM((2 * tq, LANES), BF16),
                        pltpu.VMEM((2 * tq, LANES), F32),
                        pltpu.VMEM((2 * tq, 2 * LANES), F32),
                        pltpu.VMEM((2 * tq, tk), F32)],
        compiler_params=_cp(("parallel", "parallel")),
        name="diff_attn",
    )(lam, qkv, qkv, qkv, subln_g)


def _hg_masks():
    C = HG_TILE
    t = np.arange(C)[:, None]
    s = np.arange(C)[None, :]
    out = []
    for reverse in (False, True):
        for h in HG_LEVELS:
            same = (t // (2 * h)) == (s // (2 * h))
            t_late = (t % (2 * h)) >= h
            s_late = (s % (2 * h)) >= h
            m = same & (~t_late & s_late if reverse else t_late & ~s_late)
            out.append(m)
        for d in range(HG_BAND):
            same = (t // HG_BAND) == (s // HG_BAND)
            out.append(same & ((s == t + d) if reverse else (s == t - d)))
    return jnp.asarray(np.stack(out).astype(np.float32))


def _hg_tri():
    C = HG_TILE
    t = np.arange(C)[:, None]
    r = np.arange(C)[None, :]
    return jnp.asarray(np.stack([(r <= t), (r >= t)]).astype(np.float32), dtype=BF16)


def _hg_tile(qr, fr, v, lb, st, tri, mask_ref, reverse):
    C = HG_TILE
    n_lvl = len(HG_LEVELS)
    m0 = (n_lvl + HG_BAND) if reverse else 0
    q = qr * jax.nn.sigmoid(qr)
    f = lb + (1.0 - lb) * jax.nn.sigmoid(fr)
    k = 1.0 - f
    g = jnp.log(f)
    g1 = g.astype(BF16)
    r1 = g - g1.astype(F32)
    g2 = r1.astype(BF16)
    g3 = (r1 - g2.astype(F32)).astype(BF16)
    b3 = jnp.dot(tri, jnp.concatenate([g1, g2, g3], axis=1), preferred_element_type=F32)
    b = b3[:, 0:LANES] + b3[:, LANES:2 * LANES] + b3[:, 2 * LANES:3 * LANES]
    b_edge = b[0:1, :] if reverse else b[C - 1:C, :]
    vb = v.astype(BF16)

    qc = (q * jnp.exp(b)).astype(BF16)
    kc = (k * jnp.exp(b_edge - b)).astype(BF16)
    o = lax.dot_general(qc, st.astype(BF16), (((1,), (1,)), ((), ())), preferred_element_type=F32)
    st_new = st * jnp.exp(b_edge) + jnp.dot(v.T.astype(BF16), kc, preferred_element_type=F32)

    a = jnp.zeros((C, C), F32)
    for li, h in enumerate(HG_LEVELS):
        rows = []
        for blk in range(C // (2 * h)):
            r = blk * 2 * h + (h if reverse else h - 1)
            rows.append(jnp.broadcast_to(b[r:r + 1, :], (2 * h, LANES)))
        bref = rows[0] if len(rows) == 1 else jnp.concatenate(rows, axis=0)
        e = jnp.exp(-jnp.abs(b - bref))
        ah = lax.dot_general((q * e).astype(BF16), (k * e).astype(BF16), (((1,), (1,)), ((), ())),
                             preferred_element_type=F32)
        a = a + ah * mask_ref[m0 + li]
    def roll_rows(x, shift):
        x3 = pltpu.roll(x.reshape(C // HG_BAND, HG_BAND, LANES), shift % HG_BAND, 1)
        return x3.reshape(C, LANES)
    ed = None
    for d in range(HG_BAND):
        if d == 0:
            p = q * k
        else:
            sh = (C - d) if reverse else d
            shf = (C - (d - 1)) % C if reverse else d - 1
            fsh = f if shf == 0 else roll_rows(f, shf)
            ed = fsh if ed is None else ed * fsh
            p = q * roll_rows(k, sh) * ed
        a = a + jnp.sum(p, axis=1, keepdims=True) * mask_ref[m0 + n_lvl + d]
    o = o + jnp.dot(a.astype(BF16), vb, preferred_element_type=F32)
    return o, st_new


def _hgrn_kernel(qf_ref, qb_ref, ff_ref, fb_ref, vf_ref, vb_ref, lb_ref, tri_ref, mask_ref,
                 of_ref, ob_ref, stf_scr, stb_scr, *, n_tiles):
    @pl.when(pl.program_id(1) == 0)
    def _():
        stf_scr[...] = jnp.zeros(stf_scr.shape, F32)
        stb_scr[...] = jnp.zeros(stb_scr.shape, F32)

    C = HG_TILE
    lbf = lb_ref[0:1, :]
    lbb = lb_ref[1:2, :]

    def body(t, carry):
        rf = pl.multiple_of(t * C, C)
        rb = pl.multiple_of((n_tiles - 1 - t) * C, C)
        o_f, st_f = _hg_tile(qf_ref[pl.ds(rf, C), :], ff_ref[pl.ds(rf, C), :], vf_ref[pl.ds(rf, C), :],
                             lbf, stf_scr[...], tri_ref[0], mask_ref, False)
        of_ref[pl.ds(rf, C), :] = o_f
        stf_scr[...] = st_f
        o_b, st_b = _hg_tile(qb_ref[pl.ds(rb, C), :], fb_ref[pl.ds(rb, C), :], vb_ref[pl.ds(rb, C), :],
                             lbb, stb_scr[...], tri_ref[1], mask_ref, True)
        ob_ref[pl.ds(rb, C), :] = o_b
        stb_scr[...] = st_b
        return carry

    lax.fori_loop(0, n_tiles, body, 0, unroll=True)


def _hgrn2(proj, lb, rows):
    L = proj.shape[0]
    rows = min(rows, L)
    nb = L // rows
    H = HG_HEADS
    nm = len(HG_LEVELS) + HG_BAND
    blk = lambda col0, rev: pl.BlockSpec(
        (rows, LANES), (lambda h, i: (nb - 1 - i, col0 + h)) if rev else (lambda h, i: (i, col0 + h)))
    return pl.pallas_call(
        functools.partial(_hgrn_kernel, n_tiles=rows // HG_TILE),
        grid=(H, nb),
        in_specs=[blk(0, False), blk(0, True), blk(H, False), blk(2 * H, True),
                  blk(3 * H, False), blk(3 * H, True),
                  pl.BlockSpec((2, LANES), lambda h, i: (0, h)),
                  pl.BlockSpec((2, HG_TILE, HG_TILE), lambda h, i: (0, 0, 0)),
                  pl.BlockSpec((2 * nm, HG_TILE, HG_TILE), lambda h, i: (0, 0, 0))],
        out_specs=[blk(0, False), blk(0, True)],
        out_shape=[jax.ShapeDtypeStruct((L, HG_WIDTH), F32)] * 2,
        scratch_shapes=[pltpu.VMEM((LANES, LANES), F32)] * 2,
        compiler_params=_cp(("parallel", "arbitrary")),
        name="hgrn2",
    )(proj, proj, proj, proj, proj, proj, lb, _hg_tri(), _hg_masks())


def _outproj_kernel(x_ref, oa_ref, of_ref, ob_ref, gr_ref, ng_ref, gate_ref, lng_ref, lnb_ref, w_ref,
                    o_ref, lhs_scr):
    lhs_scr[:, 0:DIFF_WIDTH] = oa_ref[...]
    ng = ng_ref[...]
    for h in range(HG_HEADS):
        sl = slice(h * LANES, (h + 1) * LANES)
        o = of_ref[:, sl] + ob_ref[:, sl]
        ms = jnp.mean(o * o, axis=-1, keepdims=True)
        gr = gr_ref[:, sl]
        hg = o * lax.rsqrt(ms + RMS_EPS) * ng * (gr * jax.nn.sigmoid(gr))
        lhs_scr[:, DIFF_WIDTH + h * LANES:DIFF_WIDTH + (h + 1) * LANES] = hg.astype(BF16)
    y = jnp.dot(lhs_scr[...], w_ref[...], preferred_element_type=F32)
    z = ALPHA * x_ref[...] + (1.0 + gate_ref[...]) * y
    o_ref[...] = _layer_norm_rows(z, lng_ref[...], lnb_ref[...])


def _outproj_ln(x, oa, of, ob, proj_hg, norm_g, gate, ln_g, ln_b, w, tm):
    L = x.shape[0]
    tm = min(tm, L)
    row = lambda i: (i, 0)
    vec = pl.BlockSpec((1, D_MODEL), lambda i: (0, 0))
    return pl.pallas_call(
        _outproj_kernel,
        grid=(L // tm,),
        in_specs=[pl.BlockSpec((tm, D_MODEL), row),
                  pl.BlockSpec((tm, DIFF_WIDTH), row),
                  pl.BlockSpec((tm, HG_WIDTH), row),
                  pl.BlockSpec((tm, HG_WIDTH), row),
                  pl.BlockSpec((tm, HG_WIDTH), lambda i: (i, 4)),
                  pl.BlockSpec((1, LANES), lambda i: (0, 0)),
                  vec, vec, vec,
                  pl.BlockSpec((D_MODEL, D_MODEL), lambda i: (0, 0))],
        out_specs=pl.BlockSpec((tm, D_MODEL), row),
        out_shape=jax.ShapeDtypeStruct((L, D_MODEL), F32),
        scratch_shapes=[pltpu.VMEM((tm, D_MODEL), BF16)],
        compiler_params=_cp(("parallel",)),
        name="outproj_ln",
    )(x, oa, of, ob, proj_hg, norm_g, gate, ln_g, ln_b, w)


def _mlp_kernel(x_ref, sc_ref, sh_ref, gate_ref, lng_ref, lnb_ref, w1_ref, w2_ref, o_ref, h_scr, acc_scr):
    j = pl.program_id(1)

    @pl.when(j == 0)
    def _():
        h_scr[...] = (x_ref[...] * (1.0 + sc_ref[...]) + sh_ref[...]).astype(BF16)
        acc_scr[...] = ALPHA * x_ref[...]

    a = jnp.maximum(jnp.dot(h_scr[...], w1_ref[...], preferred_element_type=F32), 0.0)
    acc_scr[...] += (1.0 + gate_ref[...]) * jnp.dot((a * a).astype(BF16), w2_ref[...], preferred_element_type=F32)

    @pl.when(j == pl.num_programs(1) - 1)
    def _():
        o_ref[...] = _layer_norm_rows(acc_scr[...], lng_ref[...], lnb_ref[...])


def _mlp_ln(x, scale, shift, gate, ln_g, ln_b, w1, w2, tm, tf):
    L = x.shape[0]
    tm = min(tm, L)
    vec = pl.BlockSpec((1, D_MODEL), lambda i, j: (0, 0))
    return pl.pallas_call(
        _mlp_kernel,
        grid=(L // tm, D_FF // tf),
        in_specs=[pl.BlockSpec((tm, D_MODEL), lambda i, j: (i, 0)),
                  vec, vec, vec, vec, vec,
                  pl.BlockSpec((D_MODEL, tf), lambda i, j: (0, j)),
                  pl.BlockSpec((tf, D_MODEL), lambda i, j: (j, 0))],
        out_specs=pl.BlockSpec((tm, D_MODEL), lambda i, j: (i, 0)),
        out_shape=jax.ShapeDtypeStruct((L, D_MODEL), F32),
        scratch_shapes=[pltpu.VMEM((tm, D_MODEL), BF16), pltpu.VMEM((tm, D_MODEL), F32)],
        compiler_params=_cp(("parallel", "arbitrary")),
        name="mlp_ln",
    )(x, scale, shift, gate, ln_g, ln_b, w1, w2)


def _s5_operators(lam_re, lam_im, log_dt, b_re, b_im, c_re, c_im):
    T, G, P, Hc, NB, GPB = S5_T, S5_GROUPS, S5_STATE, S5_GROUP, S5_BLOCKS, S5_GPB
    dt = jnp.exp(log_dt)[..., None]
    zr, zi = lam_re * dt, lam_im * dt
    mag = jnp.exp(zr)
    ab_re, ab_im = mag * jnp.cos(zi), mag * jnp.sin(zi)
    den = lam_re * lam_re + lam_im * lam_im
    coef_re = ((ab_re - 1.0) * lam_re + ab_im * lam_im) / den
    coef_im = (ab_im * lam_re - (ab_re - 1.0) * lam_im) / den
    bb_re = coef_re[..., None] * b_re - coef_im[..., None] * b_im
    bb_im = coef_re[..., None] * b_im + coef_im[..., None] * b_re

    def apow(k):
        k = jnp.asarray(k, F32)[:, None, None, None]
        m = jnp.exp(zr[None] * k)
        return m * jnp.cos(zi[None] * k), m * jnp.sin(zi[None] * k)

    pr, pi = apow(np.arange(T))
    ca_re = c_re[None] * pr[:, :, :, None, :] - c_im[None] * pi[:, :, :, None, :]
    ca_im = c_re[None] * pi[:, :, :, None, :] + c_im[None] * pr[:, :, :, None, :]
    bt_re, bt_im = jnp.swapaxes(bb_re, 2, 3), jnp.swapaxes(bb_im, 2, 3)
    lagk = jnp.sum(ca_re[:, :, :, :, None, :] * bt_re[None, :, :, None, :, :]
                   - ca_im[:, :, :, :, None, :] * bt_im[None, :, :, None, :, :], axis=-1)
    kc = jnp.transpose(lagk.reshape(T, 2, NB, GPB, Hc, Hc), (2, 1, 0, 5, 3, 4)).reshape(NB, 2 * T, Hc, LANES)

    def gp_lanes(v):
        return jnp.transpose(v.reshape(2, NB, GPB, Hc, P), (1, 0, 3, 2, 4)).reshape(NB, 2, Hc, GPB * P)
    bt = jnp.stack([gp_lanes(bt_re), gp_lanes(bt_im)], axis=1)
    def e_tab(v):
        v = jnp.stack([v[::-1, 0], v[:, 1]], axis=1).reshape(T, 2, NB, GPB * P)
        return jnp.transpose(v, (2, 0, 1, 3))
    et = jnp.stack([e_tab(pr), e_tab(pi)], axis=2)[:, :, :, :, None, :]

    ct = jnp.stack([gp_lanes(c_re), gp_lanes(c_im)], axis=1)
    cr, ci = apow(np.arange(1, T + 1))
    def ec_tab(v):
        v = jnp.stack([v[:, 0], v[::-1, 1]], axis=1).reshape(T, 2, NB, GPB * P)
        return jnp.transpose(v, (2, 0, 1, 3))
    ec = jnp.stack([ec_tab(cr), ec_tab(ci)], axis=2)[:, :, :, :, None, :]
    return tuple(_s5_expand(kc, bt, et, ct, ec)) + ((zr, zi),)


def _s5_expand_kernel(kc_ref, bt_ref, et_ref, ct_ref, ec_ref, wt_ref, ws_ref, wc_ref):
    n_q = S5_GPB * S5_STATE // LANES
    gpq = S5_GPB // n_q
    row = lax.broadcasted_iota(jnp.int32, (LANES, LANES), 0)
    col = lax.broadcasted_iota(jnp.int32, (LANES, LANES), 1)
    over_groups = lambda a: jnp.concatenate([a] * S5_GPB, axis=0)
    same_group = (row // S5_GROUP) == (col // S5_GROUP)

    for u in range(S5_EXPAND_SLABS):
        s = pl.program_id(1) * S5_EXPAND_SLABS + u
        rows = slice(u * LANES, (u + 1) * LANES)
        for t in range(S5_T):
            lag = t - s
            k = kc_ref[0, jnp.where(lag >= 0, lag, S5_T - lag)] + jnp.where(lag == 0, kc_ref[0, S5_T], 0.0)
            wt_ref[0, rows, t * LANES:(t + 1) * LANES] = jnp.where(same_group, over_groups(k), 0.0).astype(BF16)

        for d in range(2):
            for q in range(n_q):
                sl = slice(q * LANES, (q + 1) * LANES)
                own = (row // S5_GROUP) == (gpq * q + col // S5_STATE)
                for m_ref, p_ref, o_ref, im_sign in ((bt_ref, et_ref, ws_ref, 1.0), (ct_ref, ec_ref, wc_ref, -1.0)):
                    m_re, m_im = m_ref[0, 0, d, :, sl], m_ref[0, 1, d, :, sl]
                    p_re, p_im = p_ref[0, u, 0, d, :, sl], p_ref[0, u, 1, d, :, sl]
                    vals = (p_re * m_re - p_im * m_im, im_sign * (p_re * m_im + p_im * m_re))
                    for ri in range(2):
                        c0 = ((d * n_q + q) * 2 + ri) * LANES
                        o_ref[0, rows, c0:c0 + LANES] = jnp.where(own, over_groups(vals[ri]), 0.0).astype(BF16)


def _s5_expand(kc, bt, et, ct, ec):
    NB, T, Hc, P, ns = S5_BLOCKS, S5_T, S5_GROUP, S5_STATE, S5_EXPAND_SLABS
    slab = pl.BlockSpec((1, ns * LANES, S5_ROW), lambda b, s: (b, s, 0))
    mat = pl.BlockSpec((1, 2, 2, Hc, S5_GPB * P), lambda b, s: (b, 0, 0, 0, 0))
    powers = pl.BlockSpec((1, ns, 2, 2, 1, S5_GPB * P), lambda b, s: (b, s, 0, 0, 0, 0))
    return pl.pallas_call(
        _s5_expand_kernel,
        grid=(NB, T // ns),
        in_specs=[pl.BlockSpec((1, 2 * T, Hc, LANES), lambda b, s: (b, 0, 0, 0)), mat, powers, mat, powers],
        out_specs=[slab, slab, slab],
        out_shape=[jax.ShapeDtypeStruct((NB, S5_ROW, S5_ROW), BF16)] * 3,
        compiler_params=_cp(("parallel", "arbitrary")),
        name="s5_expand",
    )(kc, bt, et, ct, ec)


def _s5_state_lanes(re, im):
    n = re.shape[0]
    x = jnp.stack([re, im], axis=0)
    x = x.reshape(2, n, 2, S5_BLOCKS, S5_GPB * S5_STATE // LANES, LANES)
    x = jnp.transpose(x, (2, 1, 3, 4, 0, 5))
    return x.reshape(2, n, S5_BLOCKS * S5_SW)


def _s5_scan_powers(zr, zi, n_steps):
    def power(k):
        k = jnp.asarray(k, F32)[:, :, None, None]
        m = jnp.exp(zr[None] * k)
        return m * jnp.cos(zi[None] * k), m * jnp.sin(zi[None] * k)
    k_dbl = S5_T * (2.0 ** np.arange(n_steps))
    j = np.arange(SUBLANES)
    return (_s5_state_lanes(*power(np.stack([k_dbl, k_dbl], axis=1))),
            _s5_state_lanes(*power(S5_T * np.stack([j, SUBLANES - 1 - j], axis=1))))


def _s5_fold(x_ref, sc, sh, tc):
    parts = [x_ref[pl.ds(t, tc, stride=S5_T), :] * (1.0 + sc) + sh for t in range(S5_T)]
    return parts


def _s5_state_kernel(x_ref, sc_ref, sh_ref, w_ref, zf_ref, zb_ref, *, tc):
    parts = _s5_fold(x_ref, sc_ref[...], sh_ref[...], tc)
    u = jnp.concatenate([p.astype(BF16) for p in parts], axis=1)
    z = jnp.dot(u, w_ref[0], preferred_element_type=F32)
    zf_ref[...] = z[:, :S5_SW]
    zb_ref[...] = z[:, S5_SW:]


def _s5_scan_kernel(zf_ref, zb_ref, af_ref, ab_ref, pf_ref, pb_ref, sf_ref, sb_ref, loc_scr,
                    *, n_rows, n_steps):
    n_blk = n_rows // SUBLANES
    n_local = int(math.log2(SUBLANES))
    sub = lax.broadcasted_iota(jnp.int32, (n_blk, SUBLANES, LANES), 1)
    blk = lax.broadcasted_iota(jnp.int32, (n_blk, LANES), 0)

    def cmul_add(re, im, ar, ai, rs, is_):
        return re + ar * rs - ai * is_, im + ar * is_ + ai * rs

    def scan(z_ref, a_ref, p_ref, reverse):
        re = z_ref[:, 0:LANES].reshape(n_blk, SUBLANES, LANES)
        im = z_ref[:, LANES:2 * LANES].reshape(n_blk, SUBLANES, LANES)

        def in_block(v, sh):
            if reverse:
                return jnp.where(sub < SUBLANES - sh, pltpu.roll(v, SUBLANES - sh, 1), 0.0)
            return jnp.where(sub >= sh, pltpu.roll(v, sh, 1), 0.0)

        def across(v, sh):
            if reverse:
                return jnp.where(blk < n_blk - sh, pltpu.roll(v, n_blk - sh, 0), 0.0)
            return jnp.where(blk >= sh, pltpu.roll(v, sh, 0), 0.0)

        for k in range(n_local):
            ar, ai = a_ref[0, k:k + 1, 0:LANES], a_ref[0, k:k + 1, LANES:2 * LANES]
            re, im = cmul_add(re, im, ar, ai, in_block(re, 2 ** k), in_block(im, 2 ** k))
        edge = 0 if reverse else SUBLANES - 1
        loc_scr[0] = re.reshape(n_rows, LANES)
        loc_scr[1] = im.reshape(n_rows, LANES)
        cr = loc_scr[0, pl.ds(edge, n_blk, stride=SUBLANES), :]
        ci = loc_scr[1, pl.ds(edge, n_blk, stride=SUBLANES), :]
        for k in range(n_local, n_steps):
            ar, ai = a_ref[0, k:k + 1, 0:LANES], a_ref[0, k:k + 1, LANES:2 * LANES]
            sh = 2 ** (k - n_local)
            cr, ci = cmul_add(cr, ci, ar, ai, across(cr, sh), across(ci, sh))
        cr, ci = across(cr, 1), across(ci, 1)
        pr = p_ref[0, :, 0:LANES][None]
        pi = p_ref[0, :, LANES:2 * LANES][None]
        cr3 = jnp.broadcast_to(cr[:, None, :], (n_blk, SUBLANES, LANES))
        ci3 = jnp.broadcast_to(ci[:, None, :], (n_blk, SUBLANES, LANES))
        ore, oim = cmul_add(in_block(re, 1), in_block(im, 1), pr, pi, cr3, ci3)
        return ore.reshape(n_rows, LANES), oim.reshape(n_rows, LANES)

    re, im = scan(zf_ref, af_ref, pf_ref, False)
    sf_ref[:, 0:LANES] = re.astype(BF16)
    sf_ref[:, LANES:2 * LANES] = im.astype(BF16)
    re, im = scan(zb_ref, ab_ref, pb_ref, True)
    sb_ref[:, 0:LANES] = re.astype(BF16)
    sb_ref[:, LANES:2 * LANES] = im.astype(BF16)


def _s5_out_kernel(x_ref, sc_ref, sh_ref, d_ref, sf_ref, sb_ref, wt_ref, wc_ref, o_ref, *, tc, n_half):
    j = pl.program_id(2)
    sc, sh, d = sc_ref[...], sh_ref[...], d_ref[...]
    parts = _s5_fold(x_ref, sc, sh, tc)
    u = jnp.concatenate([p.astype(BF16) for p in parts], axis=1)
    s = jnp.concatenate([sf_ref[...], sb_ref[...]], axis=1)
    y = (jnp.dot(u, wt_ref[0], preferred_element_type=F32)
         + lax.dot_general(s, wc_ref[0], (((1,), (1,)), ((), ())),
                           preferred_element_type=F32))
    tph = S5_T // n_half
    for jj in range(n_half):
        @pl.when(j == jj)
        def _():
            for tl in range(tph):
                t = jj * tph + tl
                o_ref[pl.ds(t, tc, stride=S5_T), :] = y[:, tl * LANES:(tl + 1) * LANES] + d * parts[t]


def _s5_mixer_core(x, scale, shift, d_skip, ops, tc):
    w_toep, w_state, w_carry, (zr, zi) = ops
    L = x.shape[0]
    n_rows = L // S5_T
    NB = S5_BLOCKS
    n_steps = int(math.log2(n_rows))
    assert 2 ** n_steps == n_rows
    tc = min(tc, n_rows)
    vecb = pl.BlockSpec((1, LANES), lambda b, i: (0, b))

    zf, zb = pl.pallas_call(
        functools.partial(_s5_state_kernel, tc=tc),
        grid=(NB, n_rows // tc),
        in_specs=[pl.BlockSpec((tc * S5_T, LANES), lambda b, i: (i, b)), vecb, vecb,
                  pl.BlockSpec((1, S5_ROW, 2 * S5_SW), lambda b, i: (b, 0, 0))],
        out_specs=[pl.BlockSpec((tc, S5_SW), lambda b, i: (i, b))] * 2,
        out_shape=[jax.ShapeDtypeStruct((n_rows, NB * S5_SW), F32)] * 2,
        compiler_params=_cp(("parallel", "parallel")),
        name="s5_state",
    )(x, scale, shift, w_state)

    pw, pj = _s5_scan_powers(zr, zi, n_steps)
    cw = 2 * LANES
    col = lambda n: (0, n)
    sf, sb = pl.pallas_call(
        functools.partial(_s5_scan_kernel, n_rows=n_rows, n_steps=n_steps),
        grid=(NB * S5_SW // cw,),
        in_specs=[pl.BlockSpec((n_rows, cw), col),
                  pl.BlockSpec((n_rows, cw), col),
                  pl.BlockSpec((1, n_steps, cw), lambda n: (0, 0, n)),
                  pl.BlockSpec((1, n_steps, cw), lambda n: (1, 0, n)),
                  pl.BlockSpec((1, SUBLANES, cw), lambda n: (0, 0, n)),
                  pl.BlockSpec((1, SUBLANES, cw), lambda n: (1, 0, n))],
        out_specs=[pl.BlockSpec((n_rows, cw), col)] * 2,
        out_shape=[jax.ShapeDtypeStruct((n_rows, NB * S5_SW), BF16)] * 2,
        scratch_shapes=[pltpu.VMEM((2, n_rows, LANES), F32)],
        compiler_params=_cp(("parallel",)),
        name="s5_scan",
    )(zf, zb, pw, pw, pj, pj)

    n_half = 2
    nh = S5_ROW // n_half
    vec3 = pl.BlockSpec((1, LANES), lambda b, i, j: (0, b))
    return pl.pallas_call(
        functools.partial(_s5_out_kernel, tc=tc, n_half=n_half),
        grid=(NB, n_rows // tc, n_half),
        in_specs=[pl.BlockSpec((tc * S5_T, LANES), lambda b, i, j: (i, b)), vec3, vec3, vec3,
                  pl.BlockSpec((tc, S5_SW), lambda b, i, j: (i, b)),
                  pl.BlockSpec((tc, S5_SW), lambda b, i, j: (i, b)),
                  pl.BlockSpec((1, S5_ROW, nh), lambda b, i, j: (b, 0, j)),
                  pl.BlockSpec((1, nh, 2 * S5_SW), lambda b, i, j: (b, j, 0))],
        out_specs=pl.BlockSpec((tc * S5_T, LANES), lambda b, i, j: (i, b)),
        out_shape=jax.ShapeDtypeStruct((L, D_MODEL), F32),
        compiler_params=_cp(("parallel", "parallel", "arbitrary")),
        name="s5_out",
    )(x, scale, shift, d_skip, sf, sb, w_toep, w_carry)


def _glu_kernel(x_ref, y_ref, gate_ref, lng_ref, lnb_ref, wa_ref, wg_ref, o_ref, lhs_scr, mix_scr, *, tn):
    j = pl.program_id(1)

    @pl.when(j == 0)
    def _():
        lhs_scr[...] = jax.nn.gelu(y_ref[...]).astype(BF16)

    a = jnp.dot(lhs_scr[...], wa_ref[...], preferred_element_type=F32)
    g = jnp.dot(lhs_scr[...], wg_ref[...], preferred_element_type=F32)
    mix = a * jax.nn.sigmoid(g)
    for jj in range(D_MODEL // tn):
        @pl.when(j == jj)
        def _():
            mix_scr[:, jj * tn:(jj + 1) * tn] = mix

    @pl.when(j == pl.num_programs(1) - 1)
    def _():
        z = ALPHA * x_ref[...] + (1.0 + gate_ref[...]) * mix_scr[...]
        o_ref[...] = _layer_norm_rows(z, lng_ref[...], lnb_ref[...])


def _glu_ln(x, y, gate, ln_g, ln_b, w_glu, tm, tn):
    L = x.shape[0]
    tm = min(tm, L)
    nj = D_MODEL // tn
    vec = pl.BlockSpec((1, D_MODEL), lambda i, j: (0, 0))
    return pl.pallas_call(
        functools.partial(_glu_kernel, tn=tn),
        grid=(L // tm, nj),
        in_specs=[pl.BlockSpec((tm, D_MODEL), lambda i, j: (i, 0)),
                  pl.BlockSpec((tm, D_MODEL), lambda i, j: (i, 0)),
                  vec, vec, vec,
                  pl.BlockSpec((D_MODEL, tn), lambda i, j: (0, j)),
                  pl.BlockSpec((D_MODEL, tn), lambda i, j: (0, nj + j))],
        out_specs=pl.BlockSpec((tm, D_MODEL), lambda i, j: (i, 0)),
        out_shape=jax.ShapeDtypeStruct((L, D_MODEL), F32),
        scratch_shapes=[pltpu.VMEM((tm, D_MODEL), BF16), pltpu.VMEM((tm, D_MODEL), F32)],
        compiler_params=_cp(("parallel", "arbitrary")),
        name="glu_ln",
    )(x, y, gate, ln_g, ln_b, w_glu, w_glu)


def kernel(x, c, ada_w, ada_b, ln_g, ln_b, mix_w_in, mix_w_out, diff_lambda, diff_subln_g, hg_lower_bound,
           hg_norm_g, s5_lambda_re, s5_lambda_im, s5_log_dt, s5_b_re, s5_b_im, s5_c_re, s5_c_im, s5_d,
           s5_w_glu, mlp_w1, mlp_w2):
    B, L, D = x.shape
    assert B == 1 and D == D_MODEL
    xs = x.reshape(L, D)
    mod = _adaln(c, ada_w, ada_b)
    mods = lambda a: (mod[a, :, 0:D], mod[a, :, D:2 * D], mod[a, :, 2 * D:3 * D])
    lb_table = jnp.cumsum(jax.nn.softmax(hg_lower_bound.astype(F32), axis=1), axis=1)
    rot_tabs = _rotary_tables(L)

    for layer in range(DEPTH):
        shift, scale, gate = mods(2 * layer)
        lng, lnb = ln_g[layer, 0][None], ln_b[layer, 0][None]
        if layer % 2 == 0:
            e = layer // 2
            lam_init = 0.8 - 0.6 * math.exp(-0.3 * layer)
            w_in = _to_bf16(mix_w_in, e)
            qkv, proj_hg = _inproj(xs, scale, shift, w_in, 3 * DIFF_WIDTH, rot_tabs, 2, INPROJ_TM, INPROJ_TN)
            lf = diff_lambda[e].astype(F32)
            lam = jnp.exp(jnp.sum(lf[0] * lf[1])) - jnp.exp(jnp.sum(lf[2] * lf[3])) + lam_init
            oa = _diff_attention(qkv, lam.reshape(1), diff_subln_g[e][None], lam_init, ATTN_TQ, ATTN_TK)
            of, ob = _hgrn2(proj_hg, lb_table[:, layer], HG_ROWS)
            xs = _outproj_ln(xs, oa, of, ob, proj_hg, hg_norm_g[e][None], gate, lng, lnb,
                             _to_bf16(mix_w_out, e), OUTPROJ_TM)
        else:
            o = layer // 2
            ops = _s5_operators(s5_lambda_re[o], s5_lambda_im[o], s5_log_dt[o], s5_b_re[o], s5_b_im[o],
                                s5_c_re[o], s5_c_im[o])
            y = _s5_mixer_core(xs, scale, shift, s5_d[o][None], ops, S5_TC)
            xs = _glu_ln(xs, y, gate, lng, lnb, _to_bf16(s5_w_glu, o), GLU_TM, GLU_TN)
        shift, scale, gate = mods(2 * layer + 1)
        xs = _mlp_ln(xs, scale, shift, gate, ln_g[layer, 1][None], ln_b[layer, 1][None],
                     _to_bf16(mlp_w1, layer), _to_bf16(mlp_w2, layer), MLP_TM, MLP_TF)
    return xs.reshape(B, L, D)
```
